```python
import jax, jax.numpy as jnp
from jax import lax
import numpy as np

D_MODEL = 2048
BATCH = 4
SEQ = 4096
DEPTH = 1
DEC_BATCH = 8
DEC_SEQ = 16
PAST_LEN = 1024

CHUNK = 64
HEAD_DIM = 64
A_HEADS = 16
A_PREV_CHUNKS = 8
A_REL_CLIP = 128
B_Q_HEADS = 16
B_KV_HEADS = 2
B_WINDOW = 128
B_PREV_CHUNKS = B_WINDOW // CHUNK
A_WIDTH = A_HEADS * HEAD_DIM
B_WIDTH = B_Q_HEADS * HEAD_DIM
B_KV_WIDTH = B_KV_HEADS * HEAD_DIM
IN_SPLITS = (A_WIDTH, A_WIDTH, A_WIDTH, B_WIDTH, B_KV_WIDTH, B_KV_WIDTH, D_MODEL, D_MODEL)
IN_COLS = sum(IN_SPLITS)
PEER_HEADS = 8
PEER_KEYS = 128
PEER_EXPERTS = PEER_KEYS * PEER_KEYS
PEER_QDIM = 256
PEER_TOPK = 16
PEER_BLOCK = 128
LN_EPS = 1e-5
DEEPNORM_ALPHA = (2 * DEPTH) ** 0.25
DEEPNORM_BETA = (8 * DEPTH) ** -0.25

kernel_name = "hybrid_chunk_band_attn_peer_streaming_step"


def layer_norm(x, g, b):
    xf = x.astype(jnp.float32)
    mu = jnp.mean(xf, axis=-1, keepdims=True)
    var = jnp.mean(jnp.square(xf - mu), axis=-1, keepdims=True)
    return ((xf - mu) * lax.rsqrt(var + LN_EPS) * g.astype(jnp.float32) + b.astype(jnp.float32)).astype(x.dtype)


def split_projection(x, w_in):
    p = jnp.einsum('bsd,de->bse', x, w_in)
    offs = np.cumsum(IN_SPLITS)[:-1].tolist()
    qa, ka, va, qb, kb, vb, ga, gb = jnp.split(p, offs, axis=-1)
    hd = lambda t: t.reshape(t.shape[:2] + (-1, HEAD_DIM))
    return hd(qa), hd(ka), hd(va), hd(qb), hd(kb), hd(vb), ga, gb


def rel_bias(rel_table, dist):
    idx = jnp.clip(dist, -A_REL_CLIP, A_REL_CLIP) + A_REL_CLIP
    return jnp.take(rel_table, idx, axis=1).astype(jnp.float32)


def alibi_bias(dist):
    slopes = jnp.exp2(-8.0 * jnp.arange(1, B_Q_HEADS + 1, dtype=jnp.float32) / B_Q_HEADS)
    return -slopes[:, None, None] * jnp.abs(dist).astype(jnp.float32)[None]


def band_attention_core(q, kb, vb, bias, valid, sink):
    b, n, tq, h, dh = q.shape
    hk = kb.shape[3]
    g = h // hk
    qg = q.reshape(b, n, tq, hk, g, dh)
    s = jnp.einsum('bnqkgd,bnskd->bkgnqs', qg, kb).astype(jnp.float32) * (dh ** -0.5)
    s = s + bias.reshape(hk, g, 1, tq, -1)
    s = jnp.where(valid[:, None, :], s, -jnp.inf)
    m = jnp.max(s, axis=-1, keepdims=True)
    if sink is not None:
        sk = sink.astype(jnp.float32).reshape(hk, g, 1, 1, 1)
        m = jnp.maximum(m, sk)
        p = jnp.exp(s - m)
        den = jnp.sum(p, axis=-1, keepdims=True) + jnp.exp(sk - m)
    else:
        p = jnp.exp(s - m)
        den = jnp.sum(p, axis=-1, keepdims=True)
    p = (p / den).astype(vb.dtype)
    o = jnp.einsum('bkgnqs,bnskd->bnqkgd', p, vb)
    return o.reshape(b, n, tq, h * dh)


def band_attention_prompt(q, k, v, n_prev, bias_fn, sink):
    b, s, h, dh = q.shape
    hk = k.shape[2]
    nc = s // CHUNK
    pad = n_prev * CHUNK
    tk = (n_prev + 1) * CHUNK
    kp = jnp.pad(k, ((0, 0), (pad, 0), (0, 0), (0, 0))).reshape(b, nc + n_prev, CHUNK, hk, dh)
    vp = jnp.pad(v, ((0, 0), (pad, 0), (0, 0), (0, 0))).reshape(b, nc + n_prev, CHUNK, hk, dh)
    idx = jnp.arange(nc)[:, None] + jnp.arange(n_prev + 1)[None, :]
    kb = kp[:, idx].reshape(b, nc, tk, hk, dh)
    vb = vp[:, idx].reshape(b, nc, tk, hk, dh)
    key_pos = (jnp.arange(nc)[:, None] - n_prev) * CHUNK + jnp.arange(tk)[None, :]
    valid = key_pos >= 0
    dist = jnp.arange(CHUNK)[:, None] + pad - jnp.arange(tk)[None, :]
    o = band_attention_core(q.reshape(b, nc, CHUNK, h, dh), kb, vb, bias_fn(dist), valid, sink)
    return o.reshape(b, s, h * dh)


def band_attention_sample(q, k_new, v_new, k_cache, v_cache, bias_fn, sink):
    b, t, h, dh = q.shape
    L = k_cache.shape[1]
    kb = jnp.concatenate([k_cache.astype(k_new.dtype), k_new], axis=1)[:, None]
    vb = jnp.concatenate([v_cache.astype(v_new.dtype), v_new], axis=1)[:, None]
    dist = jnp.arange(t)[:, None] + L - jnp.arange(L + t)[None, :]
    valid = jnp.ones((1, L + t), dtype=bool)
    o = band_attention_core(q[:, None], kb, vb, bias_fn(dist), valid, sink)
    return o.reshape(b, t, h * dh)


def mixer_merge(ya, yb, ga, gb, w_proj_a, w_proj_b, w_out):
    za = jnp.einsum('bse,ed->bsd', ya, w_proj_a)
    zb = jnp.einsum('bse,ed->bsd', yb, w_proj_b)
    m = jax.nn.sigmoid(ga) * za + jax.nn.sigmoid(gb) * zb
    return jnp.einsum('bsd,de->bse', m, w_out)


def peer_tokens(xt, w_query, sub_keys_1, sub_keys_2, u_table, v_table):
    n = xt.shape[0]
    q = (xt @ w_query).reshape(n, PEER_HEADS, 2, PEER_QDIM // 2)
    s1 = jnp.einsum('nhd,kd->nhk', q[:, :, 0], sub_keys_1).astype(jnp.float32)
    s2 = jnp.einsum('nhd,kd->nhk', q[:, :, 1], sub_keys_2).astype(jnp.float32)
    v1, i1 = lax.top_k(s1, PEER_TOPK)
    v2, i2 = lax.top_k(s2, PEER_TOPK)
    cand = (v1[..., :, None] + v2[..., None, :]).reshape(n, PEER_HEADS, PEER_TOPK * PEER_TOPK)
    sc, ci = lax.top_k(cand, PEER_TOPK)
    e1 = jnp.take_along_axis(i1, ci // PEER_TOPK, axis=-1)
    e2 = jnp.take_along_axis(i2, ci % PEER_TOPK, axis=-1)
    experts = e1 * PEER_KEYS + e2
    gate = jax.nn.softmax(sc, axis=-1)
    u = jnp.take(u_table, experts, axis=0)
    act = jax.nn.gelu(jnp.einsum('nhkd,nd->nhk', u, xt).astype(jnp.float32), approximate=False)
    w = (gate * act).astype(xt.dtype)
    v = jnp.take(v_table, experts, axis=0)
    return jnp.einsum('nhk,nhkd->nd', w, v)


def peer_layer(x, w_query, sub_keys_1, sub_keys_2, u_table, v_table):
    xb = x.reshape(-1, PEER_BLOCK, x.shape[-1])
    out = lax.map(lambda blk: peer_tokens(blk, w_query, sub_keys_1, sub_keys_2, u_table, v_table), xb)
    return out.reshape(x.shape)


def setup_inputs(seed: int = 0) -> dict:
    key = jax.random.key(seed)
    ks = jax.random.split(key, 24)
    f32 = jnp.float32
    nrm = lambda k, shape: jax.random.normal(k, shape, dtype=f32)
    la = min(A_PREV_CHUNKS * CHUNK, PAST_LEN)
    lb = min(B_WINDOW, PAST_LEN)
    col_scale = np.ones((IN_COLS,), dtype=np.float32)
    offs = np.cumsum((0,) + IN_SPLITS)
    col_scale[offs[2]:offs[3]] = DEEPNORM_BETA
    col_scale[offs[5]:offs[6]] = DEEPNORM_BETA
    w_in = nrm(ks[4], (D_MODEL, IN_COLS)) * (D_MODEL ** -0.5) * jnp.asarray(col_scale)
    return {
        "x_prompt": nrm(ks[0], (BATCH, SEQ, D_MODEL)),
        "x_sample": nrm(ks[1], (DEC_BATCH, DEC_SEQ, D_MODEL)),
        "cache_a_k": nrm(ks[2], (DEC_BATCH, la, A_HEADS, HEAD_DIM)),
        "cache_a_v": nrm(ks[3], (DEC_BATCH, la, A_HEADS, HEAD_DIM)) * DEEPNORM_BETA,
        "cache_b_k": nrm(ks[5], (DEC_BATCH, lb, B_KV_HEADS, HEAD_DIM)),
        "cache_b_v": nrm(ks[6], (DEC_BATCH, lb, B_KV_HEADS, HEAD_DIM)) * DEEPNORM_BETA,
        "w_in": w_in,
        "rel_bias_table": 0.5 * nrm(ks[7], (A_HEADS, 2 * A_REL_CLIP + 1)),
        "b_sinks": nrm(ks[8], (B_Q_HEADS,)),
        "w_proj_a": nrm(ks[9], (A_WIDTH, D_MODEL)) * (A_WIDTH ** -0.5) * DEEPNORM_BETA,
        "w_proj_b": nrm(ks[10], (B_WIDTH, D_MODEL)) * (B_WIDTH ** -0.5) * DEEPNORM_BETA,
        "w_out": nrm(ks[11], (D_MODEL, D_MODEL)) * (D_MODEL ** -0.5) * DEEPNORM_BETA,
        "ln1_g": 1.0 + 0.05 * nrm(ks[12], (D_MODEL,)),
        "ln1_b": 0.02 * nrm(ks[13], (D_MODEL,)),
        "peer_w_query": nrm(ks[14], (D_MODEL, PEER_HEADS * PEER_QDIM)) * (D_MODEL ** -0.5),
        "peer_sub_keys_1": nrm(ks[15], (PEER_KEYS, PEER_QDIM // 2)) * ((PEER_QDIM // 2) ** -0.5),
        "peer_sub_keys_2": nrm(ks[16], (PEER_KEYS, PEER_QDIM // 2)) * ((PEER_QDIM // 2) ** -0.5),
        "peer_u": nrm(ks[17], (PEER_EXPERTS, D_MODEL)) * (D_MODEL ** -0.5),
        "peer_v": nrm(ks[18], (PEER_EXPERTS, D_MODEL)) * DEEPNORM_BETA,
        "ln2_g": 1.0 + 0.05 * nrm(ks[19], (D_MODEL,)),
        "ln2_b": 0.02 * nrm(ks[20], (D_MODEL,)),
    }


def reference(x_prompt, x_sample, cache_a_k, cache_a_v, cache_b_k, cache_b_v, w_in, rel_bias_table, b_sinks,
              w_proj_a, w_proj_b, w_out, ln1_g, ln1_b, peer_w_query, peer_sub_keys_1, peer_sub_keys_2,
              peer_u, peer_v, ln2_g, ln2_b):
    bias_a = lambda d: rel_bias(rel_bias_table, d)
    peer_args = (peer_w_query, peer_sub_keys_1, peer_sub_keys_2, peer_u, peer_v)

    qa, ka, va, qb, kb, vb, ga, gb = split_projection(x_prompt, w_in)
    ya = band_attention_prompt(qa, ka, va, A_PREV_CHUNKS, bias_a, None)
    yb = band_attention_prompt(qb, kb, vb, B_PREV_CHUNKS, alibi_bias, b_sinks)
    h_p = layer_norm(DEEPNORM_ALPHA * x_prompt + mixer_merge(ya, yb, ga, gb, w_proj_a, w_proj_b, w_out), ln1_g, ln1_b)
    y_prompt = layer_norm(DEEPNORM_ALPHA * h_p + peer_layer(h_p, *peer_args), ln2_g, ln2_b)
    la_p = min(A_PREV_CHUNKS * CHUNK, x_prompt.shape[1])
    lb_p = min(B_WINDOW, x_prompt.shape[1])

    qa_s, ka_s, va_s, qb_s, kb_s, vb_s, ga_s, gb_s = split_projection(x_sample, w_in)
    ya_s = band_attention_sample(qa_s, ka_s, va_s, cache_a_k, cache_a_v, bias_a, None)
    yb_s = band_attention_sample(qb_s, kb_s, vb_s, cache_b_k, cache_b_v, alibi_bias, b_sinks)
    h_s = layer_norm(DEEPNORM_ALPHA * x_sample + mixer_merge(ya_s, yb_s, ga_s, gb_s, w_proj_a, w_proj_b, w_out), ln1_g, ln1_b)
    n_s = h_s.shape[0] * h_s.shape[1]
    ffn_s = peer_tokens(h_s.reshape(n_s, D_MODEL), *peer_args).reshape(h_s.shape)
    y_sample = layer_norm(DEEPNORM_ALPHA * h_s + ffn_s, ln2_g, ln2_b)

    return (y_prompt, y_sample, ka[:, -la_p:], va[:, -la_p:], kb[:, -lb_p:], vb[:, -lb_p:], ka_s, va_s, kb_s, vb_s)
```

```python
import functools
import math

import jax
import jax.numpy as jnp
import numpy as np
from jax import lax
from jax.experimental import pallas as pl
from jax.experimental.pallas import tpu as pltpu

F32 = jnp.float32
BF16 = jnp.bfloat16

D_MODEL = 2048
CHUNK = 64
HEAD_DIM = 64
A_HEADS = 16
A_PREV_CHUNKS = 8
A_REL_CLIP = 128
B_Q_HEADS = 16
B_KV_HEADS = 2
B_PREV_CHUNKS = 2
A_WIDTH = A_HEADS * HEAD_DIM
B_WIDTH = B_Q_HEADS * HEAD_DIM
B_KV_WIDTH = B_KV_HEADS * HEAD_DIM
PEER_HEADS = 8
PEER_KEYS = 128
PEER_QDIM = 256
PEER_TOPK = 16
PEER_PICKS = PEER_HEADS * PEER_TOPK
LN_EPS = 1e-5
DEEPNORM_ALPHA = 2.0 ** 0.25

LANES = 128
HEAD_PAIR = 2 * HEAD_DIM
VMEM_LIMIT = 56 * 1024 * 1024

COL_GA = 0
COL_GB = COL_GA + D_MODEL
COL_QA = COL_GB + D_MODEL
COL_QB = COL_QA + A_WIDTH
COL_KA = COL_QB + B_WIDTH
COL_VA = COL_KA + A_WIDTH
COL_KB = COL_VA + A_WIDTH
COL_VB = COL_KB + B_KV_WIDTH
IN_COLS = COL_VB + B_KV_WIDTH
KV_COLS = IN_COLS - COL_KA
PROJ_TN = 768
ATT_BLOCK = A_PREV_CHUNKS * CHUNK


def _compiler_params(semantics):
    return pltpu.CompilerParams(dimension_semantics=semantics, vmem_limit_bytes=VMEM_LIMIT)


def _const_spec(shape):
    zeros = (0,) * len(shape)
    return pl.BlockSpec(shape, lambda *_: zeros, pipeline_mode=pl.Buffered(1))


def _in_proj_kernel(x_ref, w_ref, p_ref, kv_ref, xb_ref):
    j = pl.program_id(1)

    @pl.when(j == 0)
    def _():
        xb_ref[...] = x_ref[...].astype(BF16)

    acc = jnp.dot(xb_ref[...], w_ref[...], preferred_element_type=F32)
    p_ref[...] = acc.astype(BF16)

    @pl.when(j >= COL_KA // PROJ_TN)
    def _():
        kv_ref[...] = acc


def _in_proj(x, w_bf16, tm):
    n = x.shape[0]
    first_kv = COL_KA // PROJ_TN
    return pl.pallas_call(
        _in_proj_kernel,
        grid=(n // tm, IN_COLS // PROJ_TN),
        in_specs=[
            pl.BlockSpec((tm, D_MODEL), lambda i, j: (i, 0)),
            pl.BlockSpec((D_MODEL, PROJ_TN), lambda i, j: (0, j)),
        ],
        out_specs=[
            pl.BlockSpec((tm, PROJ_TN), lambda i, j: (i, j)),
            pl.BlockSpec((tm, PROJ_TN), lambda i, j: (i, jnp.maximum(j - first_kv, 0))),
        ],
        out_shape=[
            jax.ShapeDtypeStruct((n, IN_COLS), BF16),
            jax.ShapeDtypeStruct((n, KV_COLS), F32),
        ],
        scratch_shapes=[pltpu.VMEM((tm, D_MODEL), BF16)],
        compiler_params=_compiler_params(("parallel", "arbitrary")),
    )(x, w_bf16)


def _softmax_pv(s, v2, sink):
    m = jnp.max(s, axis=-1, keepdims=True)
    if sink is not None:
        m = jnp.maximum(m, sink)
    p = jnp.exp(s - m)
    den = jnp.sum(p, axis=-1, keepdims=True)
    if sink is not None:
        den = den + jnp.exp(sink - m)
    o = jnp.dot(p.astype(BF16), v2, preferred_element_type=F32)
    return o / den


def _qk(qm, k2):
    return lax.dot_general(qm, k2, (((1,), (1,)), ((), ())), preferred_element_type=F32) * (HEAD_DIM ** -0.5)


def _mixer_a_pairs(q_of, k_of, v_of, bias_of, key_thresh, store):
    q_shape = None
    for j in range(A_HEADS // 2):
        q2, k2, v2 = q_of(j), k_of(j), v_of(j)
        if q_shape is None:
            q_shape = q2.shape
            lane_hi = lax.broadcasted_iota(jnp.int32, q_shape, 1) >= HEAD_DIM
            col = lax.broadcasted_iota(jnp.int32, (q_shape[0], k2.shape[0]), 1)
            valid = None if key_thresh is None else col >= key_thresh
        out2 = None
        for hh in range(2):
            sel = lane_hi if hh else jnp.logical_not(lane_hi)
            qm = jnp.where(sel, q2, jnp.zeros_like(q2))
            s = _qk(qm, k2) + bias_of(2 * j + hh)
            if valid is not None:
                s = jnp.where(valid, s, -jnp.inf)
            o = _softmax_pv(s, v2, None)
            out2 = o if out2 is None else jnp.where(lane_hi, o, out2)
        store(j, out2)


def _alibi_slope(h):
    return 2.0 ** (-8.0 * (h + 1) / B_Q_HEADS)


def _mixer_b_pairs(q_of, k2, v2, absdist, sink_of, key_thresh, store):
    group = B_Q_HEADS // B_KV_HEADS
    lane_hi = None
    for j in range(B_Q_HEADS // 2):
        q2 = q_of(j)
        if lane_hi is None:
            lane_hi = lax.broadcasted_iota(jnp.int32, q2.shape, 1) >= HEAD_DIM
            col = lax.broadcasted_iota(jnp.int32, (q2.shape[0], k2.shape[0]), 1)
            valid = None if key_thresh is None else col >= key_thresh
        q2f = q2.astype(F32)
        q2_swapped = pltpu.roll(q2f, HEAD_DIM, 1).astype(BF16)
        out2 = None
        for hh in range(2):
            h = 2 * j + hh
            kv = h // group
            kv_sel = lane_hi if kv else jnp.logical_not(lane_hi)
            src = q2 if hh == kv else q2_swapped
            qm = jnp.where(kv_sel, src, jnp.zeros_like(src))
            s = _qk(qm, k2) - _alibi_slope(h) * absdist
            if valid is not None:
                s = jnp.where(valid, s, -jnp.inf)
            o = _softmax_pv(s, v2, sink_of(h))
            if hh != kv:
                o = pltpu.roll(o, HEAD_DIM, 1)
            out2 = o if out2 is None else jnp.where(lane_hi, o, out2)
        store(j, out2)


def _attn_prompt_kernel(sink_ref, qa_ref, kap_ref, kac_ref, vap_ref, vac_ref,
                        qb_ref, kbp_ref, kbc_ref, vbp_ref, vbc_ref, bias_ref,
                        ya_ref, yb_ref, ka_s, va_s, kb_s, vb_s):
    i = pl.program_id(1)
    blk = ATT_BLOCK
    ka_s[0:blk, :] = kap_ref[...]
    ka_s[blk:2 * blk, :] = kac_ref[...]
    va_s[0:blk, :] = vap_ref[...]
    va_s[blk:2 * blk, :] = vac_ref[...]
    kb_s[0:blk, :] = kbp_ref[...]
    kb_s[blk:2 * blk, :] = kbc_ref[...]
    vb_s[0:blk, :] = vbp_ref[...]
    vb_s[blk:2 * blk, :] = vbc_ref[...]

    tk_a = (A_PREV_CHUNKS + 1) * CHUNK
    tk_b = (B_PREV_CHUNKS + 1) * CHUNK
    b_off = blk - B_PREV_CHUNKS * CHUNK
    tq = lax.broadcasted_iota(jnp.int32, (CHUNK, tk_b), 0)
    ts = lax.broadcasted_iota(jnp.int32, (CHUNK, tk_b), 1)
    absdist = jnp.abs(tq + B_PREV_CHUNKS * CHUNK - ts).astype(F32)

    def chunk_body(c, carry):
        row0 = pl.multiple_of(c * CHUNK, CHUNK)
        thresh_a = jnp.where(i > 0, 0, blk - c * CHUNK)
        thresh_b = jnp.where(i > 0, 0, B_PREV_CHUNKS * CHUNK - c * CHUNK)

        def store_a(j, val):
            ya_ref[pl.ds(row0, CHUNK), j * HEAD_PAIR:(j + 1) * HEAD_PAIR] = val.astype(ya_ref.dtype)

        def store_b(j, val):
            yb_ref[pl.ds(row0, CHUNK), j * HEAD_PAIR:(j + 1) * HEAD_PAIR] = val.astype(yb_ref.dtype)

        _mixer_a_pairs(
            lambda j: qa_ref[pl.ds(row0, CHUNK), j * HEAD_PAIR:(j + 1) * HEAD_PAIR],
            lambda j: ka_s[pl.ds(row0, tk_a), j * HEAD_PAIR:(j + 1) * HEAD_PAIR],
            lambda j: va_s[pl.ds(row0, tk_a), j * HEAD_PAIR:(j + 1) * HEAD_PAIR],
            lambda h: bias_ref[h],
            thresh_a, store_a)

        rowb = pl.multiple_of(b_off + c * CHUNK, CHUNK)
        _mixer_b_pairs(
            lambda j: qb_ref[pl.ds(row0, CHUNK), j * HEAD_PAIR:(j + 1) * HEAD_PAIR],
            kb_s[pl.ds(rowb, tk_b), :], vb_s[pl.ds(rowb, tk_b), :],
            absdist, lambda h: sink_ref[h], thresh_b, store_b)
        return carry

    lax.fori_loop(0, blk // CHUNK, chunk_body, 0)


def _attn_prompt(p, bias_a, sinks, batch, seq):
    blk = ATT_BLOCK
    nb = seq // blk
    n = batch * seq

    def cur(col_block):
        return lambda b, i: (b * nb + i, col_block)

    def prev(col_block):
        return lambda b, i: (b * nb + jnp.maximum(i - 1, 0), col_block)

    wa, wb = A_WIDTH, B_KV_WIDTH
    in_specs = [
        pl.BlockSpec(memory_space=pltpu.SMEM),
        pl.BlockSpec((blk, wa), cur(COL_QA // wa)),
        pl.BlockSpec((blk, wa), prev(COL_KA // wa)),
        pl.BlockSpec((blk, wa), cur(COL_KA // wa)),
        pl.BlockSpec((blk, wa), prev(COL_VA // wa)),
        pl.BlockSpec((blk, wa), cur(COL_VA // wa)),
        pl.BlockSpec((blk, wa), cur(COL_QB // wa)),
        pl.BlockSpec((blk, wb), prev(COL_KB // wb)),
        pl.BlockSpec((blk, wb), cur(COL_KB // wb)),
        pl.BlockSpec((blk, wb), prev(COL_VB // wb)),
        pl.BlockSpec((blk, wb), cur(COL_VB // wb)),
        _const_spec(bias_a.shape),
    ]
    out_spec = pl.BlockSpec((blk, wa), lambda b, i: (b * nb + i, 0))
    return pl.pallas_call(
        _attn_prompt_kernel,
        grid=(batch, nb),
        in_specs=in_specs,
        out_specs=[out_spec, out_spec],
        out_shape=[jax.ShapeDtypeStruct((n, A_WIDTH), BF16), jax.ShapeDtypeStruct((n, B_WIDTH), BF16)],
        scratch_shapes=[
            pltpu.VMEM((2 * blk, wa), BF16), pltpu.VMEM((2 * blk, wa), BF16),
            pltpu.VMEM((2 * blk, wb), BF16), pltpu.VMEM((2 * blk, wb), BF16),
        ],
        compiler_params=_compiler_params(("parallel", "arbitrary")),
    )(sinks, p, p, p, p, p, p, p, p, p, p, bias_a)


def _attn_sample_kernel(sink_ref, qa_ref, ka_ref, va_ref, qb_ref, kb_ref, vb_ref, bias_ref, ya_ref, yb_ref):
    t, tk_b = qb_ref.shape[0], kb_ref.shape[1]

    def store_a(j, val):
        ya_ref[:, j * HEAD_PAIR:(j + 1) * HEAD_PAIR] = val.astype(ya_ref.dtype)

    def store_b(j, val):
        yb_ref[:, j * HEAD_PAIR:(j + 1) * HEAD_PAIR] = val.astype(yb_ref.dtype)

    _mixer_a_pairs(
        lambda j: qa_ref[:, j * HEAD_PAIR:(j + 1) * HEAD_PAIR],
        lambda j: ka_ref[0, :, j * HEAD_PAIR:(j + 1) * HEAD_PAIR],
        lambda j: va_ref[0, :, j * HEAD_PAIR:(j + 1) * HEAD_PAIR],
        lambda h: bias_ref[h], None, store_a)

    tq = lax.broadcasted_iota(jnp.int32, (t, tk_b), 0)
    ts = lax.broadcasted_iota(jnp.int32, (t, tk_b), 1)
    absdist = jnp.abs(tq + (tk_b - t) - ts).astype(F32)
    _mixer_b_pairs(
        lambda j: qb_ref[:, j * HEAD_PAIR:(j + 1) * HEAD_PAIR],
        kb_ref[0], vb_ref[0], absdist, lambda h: sink_ref[h], None, store_b)


def _attn_sample(p, kcat_a, vcat_a, kcat_b, vcat_b, bias_a, sinks, batch, t):
    n = batch * t
    wa, wb = A_WIDTH, B_KV_WIDTH
    tk_a, tk_b = kcat_a.shape[1], kcat_b.shape[1]
    in_specs = [
        pl.BlockSpec(memory_space=pltpu.SMEM),
        pl.BlockSpec((t, wa), lambda b: (b, COL_QA // wa)),
        pl.BlockSpec((1, tk_a, wa), lambda b: (b, 0, 0)),
        pl.BlockSpec((1, tk_a, wa), lambda b: (b, 0, 0)),
        pl.BlockSpec((t, wa), lambda b: (b, COL_QB // wa)),
        pl.BlockSpec((1, tk_b, wb), lambda b: (b, 0, 0)),
        pl.BlockSpec((1, tk_b, wb), lambda b: (b, 0, 0)),
        _const_spec(bias_a.shape),
    ]
    out_spec = pl.BlockSpec((t, wa), lambda b: (b, 0))
    return pl.pallas_call(
        _attn_sample_kernel,
        grid=(batch,),
        in_specs=in_specs,
        out_specs=[out_spec, out_spec],
        out_shape=[jax.ShapeDtypeStruct((n, A_WIDTH), BF16), jax.ShapeDtypeStruct((n, B_WIDTH), BF16)],
        compiler_params=_compiler_params(("parallel",)),
    )(sinks, p, kcat_a, vcat_a, p, kcat_b, vcat_b, bias_a)


def _layer_norm(x, g, b):
    mu = jnp.mean(x, axis=-1, keepdims=True)
    xc = x - mu
    var = jnp.mean(xc * xc, axis=-1, keepdims=True)
    return xc * lax.rsqrt(var + LN_EPS) * g + b


def _merge_kernel(ya_ref, yb_ref, ga_ref, gb_ref, x_ref, wpa_ref, wpb_ref, wout_ref, g_ref, b_ref, h_ref):
    za = jnp.dot(ya_ref[...], wpa_ref[...], preferred_element_type=F32)
    zb = jnp.dot(yb_ref[...], wpb_ref[...], preferred_element_type=F32)
    m = jax.nn.sigmoid(ga_ref[...].astype(F32)) * za + jax.nn.sigmoid(gb_ref[...].astype(F32)) * zb
    r = jnp.dot(m.astype(BF16), wout_ref[...], preferred_element_type=F32)
    h_ref[...] = _layer_norm(DEEPNORM_ALPHA * x_ref[...] + r, g_ref[...], b_ref[...])


def _merge(ya, yb, p, x, wpa, wpb, wout, ln_g, ln_b, tm):
    n = x.shape[0]
    row = lambda i: (i, 0)
    return pl.pallas_call(
        _merge_kernel,
        grid=(n // tm,),
        in_specs=[
            pl.BlockSpec((tm, A_WIDTH), row),
            pl.BlockSpec((tm, B_WIDTH), row),
            pl.BlockSpec((tm, D_MODEL), lambda i: (i, COL_GA // D_MODEL)),
            pl.BlockSpec((tm, D_MODEL), lambda i: (i, COL_GB // D_MODEL)),
            pl.BlockSpec((tm, D_MODEL), row),
            _const_spec(wpa.shape), _const_spec(wpb.shape), _const_spec(wout.shape),
            _const_spec(ln_g.shape), _const_spec(ln_b.shape),
        ],
        out_specs=pl.BlockSpec((tm, D_MODEL), row),
        out_shape=jax.ShapeDtypeStruct((n, D_MODEL), F32),
        compiler_params=_compiler_params(("parallel",)),
    )(ya, yb, p, p, x, wpa, wpb, wout, ln_g, ln_b)


def _topk_rows(s, k):
    r = s.shape[0]
    row = lax.broadcasted_iota(jnp.int32, s.shape, 0)
    vals, idxs = [], []
    for _ in range(k):
        m = jnp.max(s, axis=0, keepdims=True)
        ix = jnp.min(jnp.where(s == m, row, r), axis=0, keepdims=True)
        s = jnp.where(row == ix, -jnp.inf, s)
        vals.append(m)
        idxs.append(ix)
    return jnp.concatenate(vals, axis=0), jnp.concatenate(idxs, axis=0)


def _route_kernel(h_ref, wq_ref, sk1_ref, sk2_ref, idx_ref, gate_ref):
    q = jnp.dot(h_ref[...].astype(BF16), wq_ref[...], preferred_element_type=F32).astype(BF16)
    half = PEER_QDIM // 2
    nt = (((1,), (1,)), ((), ()))
    experts, gates = [], []
    for hd in range(PEER_HEADS):
        q1 = q[:, hd * PEER_QDIM:hd * PEER_QDIM + half]
        q2 = q[:, hd * PEER_QDIM + half:(hd + 1) * PEER_QDIM]
        s1 = lax.dot_general(sk1_ref[...], q1, nt, preferred_element_type=F32)
        s2 = lax.dot_general(sk2_ref[...], q2, nt, preferred_element_type=F32)
        v1, i1 = _topk_rows(s1, PEER_TOPK)
        v2, i2 = _topk_rows(s2, PEER_TOPK)
        cand = jnp.concatenate([v1[a:a + 1] + v2 for a in range(PEER_TOPK)], axis=0)
        sc, ci = _topk_rows(cand, PEER_TOPK)
        ca = lax.shift_right_logical(ci, int(math.log2(PEER_TOPK)))
        cb = lax.bitwise_and(ci, PEER_TOPK - 1)
        e1 = jnp.zeros_like(ci)
        e2 = jnp.zeros_like(ci)
        for a in range(PEER_TOPK):
            e1 = jnp.where(ca == a, i1[a:a + 1], e1)
            e2 = jnp.where(cb == a, i2[a:a + 1], e2)
        experts.append(e1 * PEER_KEYS + e2)
        e = jnp.exp(sc - sc[0:1])
        gates.append(e / jnp.sum(e, axis=0, keepdims=True))
    idx_ref[...] = jnp.concatenate(experts, axis=0).astype(F32).T.astype(jnp.int32)
    gate_ref[...] = jnp.concatenate(gates, axis=0).T


def _route(h, wq, sk1, sk2, tm):
    n = h.shape[0]
    row = lambda i: (i, 0)
    return pl.pallas_call(
        _route_kernel,
        grid=(n // tm,),
        in_specs=[pl.BlockSpec((tm, D_MODEL), row), _const_spec(wq.shape),
                  _const_spec(sk1.shape), _const_spec(sk2.shape)],
        out_specs=[pl.BlockSpec((tm, PEER_PICKS), row), pl.BlockSpec((tm, PEER_PICKS), row)],
        out_shape=[jax.ShapeDtypeStruct((n, PEER_PICKS), jnp.int32),
                   jax.ShapeDtypeStruct((n, PEER_PICKS), F32)],
        compiler_params=_compiler_params(("parallel",)),
    )(h, wq, sk1, sk2)


PEER_TB = 32
PEER_SLOTS = 4
ISSUE_UNROLL = 16


def _gelu_exact(x):
    return 0.5 * x * (1.0 + lax.erf(x * (2.0 ** -0.5)))


def _peer_kernel(idx_ref, idxn_ref, gate_ref, h_ref, uv_ref, g_ref, b_ref, y_ref, buf, sem, ffn):
    i = pl.program_id(0)
    last = pl.num_programs(0) - 1
    tb = h_ref.shape[0]
    lookahead = PEER_SLOTS - 1

    def row_copy(e, slot, k):
        return pltpu.make_async_copy(uv_ref.at[pl.ds(e, 1), :], buf.at[slot, pl.ds(k, 1), :], sem.at[slot])

    def issue(ids, t, slot):
        def body(kk, carry):
            for u in range(ISSUE_UNROLL):
                k = kk * ISSUE_UNROLL + u
                row_copy(ids[t, k], slot, k).start()
            return carry
        lax.fori_loop(0, PEER_PICKS // ISSUE_UNROLL, body, 0)

    def wait_slot(slot):
        def body(k, carry):
            row_copy(0, slot, k).wait()
            return carry
        lax.fori_loop(0, PEER_PICKS, body, 0)

    @pl.when(i == 0)
    def _():
        for t in range(lookahead):
            issue(idx_ref, t, t % PEER_SLOTS)

    eye = (lax.broadcasted_iota(jnp.int32, (PEER_PICKS, PEER_PICKS), 0)
           == lax.broadcasted_iota(jnp.int32, (PEER_PICKS, PEER_PICKS), 1))

    def token_body(t, carry):
        g = i * tb + t
        slot = lax.rem(g, PEER_SLOTS)
        ahead = t + lookahead
        ahead_slot = lax.rem(g + lookahead, PEER_SLOTS)

        @pl.when(ahead < tb)
        def _():
            issue(idx_ref, ahead, ahead_slot)

        @pl.when(jnp.logical_and(ahead >= tb, i < last))
        def _():
            issue(idxn_ref, ahead - tb, ahead_slot)

        wait_slot(slot)
        h_row = h_ref[pl.ds(t, 1), :]
        u = buf[slot, :, 0:D_MODEL]
        a_col = jnp.sum(u * h_row, axis=1, keepdims=True)
        a_row = jnp.sum(jnp.where(eye, a_col, 0.0), axis=0, keepdims=True)
        w_row = gate_ref[pl.ds(t, 1), :] * _gelu_exact(a_row)
        w_col = jnp.sum(jnp.where(eye, w_row, 0.0), axis=1, keepdims=True)
        v = buf[slot, :, D_MODEL:2 * D_MODEL]
        ffn[pl.ds(t, 1), :] = jnp.sum(w_col * v, axis=0, keepdims=True)
        return carry

    lax.fori_loop(0, tb, token_body, 0)
    y_ref[...] = _layer_norm(DEEPNORM_ALPHA * h_ref[...] + ffn[...], g_ref[...], b_ref[...])


def _peer(h, idx, gate, uv, ln_g, ln_b):
    n = h.shape[0]
    tb = min(PEER_TB, n)
    nb = n // tb
    row = lambda i: (i, 0)
    return pl.pallas_call(
        _peer_kernel,
        grid=(nb,),
        in_specs=[
            pl.BlockSpec((tb, PEER_PICKS), row, memory_space=pltpu.SMEM),
            pl.BlockSpec((tb, PEER_PICKS), lambda i: (jnp.minimum(i + 1, nb - 1), 0), memory_space=pltpu.SMEM),
            pl.BlockSpec((tb, PEER_PICKS), row),
            pl.BlockSpec((tb, D_MODEL), row),
            pl.BlockSpec(memory_space=pl.ANY),
            _const_spec(ln_g.shape), _const_spec(ln_b.shape),
        ],
        out_specs=pl.BlockSpec((tb, D_MODEL), row),
        out_shape=jax.ShapeDtypeStruct((n, D_MODEL), F32),
        scratch_shapes=[
            pltpu.VMEM((PEER_SLOTS, PEER_PICKS, 2 * D_MODEL), F32),
            pltpu.SemaphoreType.DMA((PEER_SLOTS,)),
            pltpu.VMEM((tb, D_MODEL), F32),
        ],
        compiler_params=_compiler_params(("arbitrary",)),
    )(idx, idx, gate, h, uv, ln_g, ln_b)


def _rel_bias(rel_table, t, tk):
    dist = jnp.arange(t)[:, None] + (tk - t) - jnp.arange(tk)[None, :]
    idx = jnp.clip(dist, -A_REL_CLIP, A_REL_CLIP) + A_REL_CLIP
    return jnp.take(rel_table, idx, axis=1).astype(F32)


def _permute_in_proj(w_in):
    offs = np.cumsum([0, A_WIDTH, A_WIDTH, A_WIDTH, B_WIDTH, B_KV_WIDTH, B_KV_WIDTH, D_MODEL, D_MODEL])
    qa, ka, va, qb, kb, vb, ga, gb = [w_in[:, offs[s]:offs[s + 1]] for s in range(8)]
    return jnp.concatenate([ga, gb, qa, qb, ka, va, kb, vb], axis=1).astype(BF16)


def _split_kv(kv, batch, t):
    ka = kv[:, 0:A_WIDTH].reshape(batch, t, A_HEADS, HEAD_DIM)
    va = kv[:, A_WIDTH:2 * A_WIDTH].reshape(batch, t, A_HEADS, HEAD_DIM)
    kb = kv[:, 2 * A_WIDTH:2 * A_WIDTH + B_KV_WIDTH].reshape(batch, t, B_KV_HEADS, HEAD_DIM)
    vb = kv[:, 2 * A_WIDTH + B_KV_WIDTH:].reshape(batch, t, B_KV_HEADS, HEAD_DIM)
    return ka, va, kb, vb


def kernel(x_prompt, x_sample, cache_a_k, cache_a_v, cache_b_k, cache_b_v, w_in, rel_bias_table, b_sinks,
           w_proj_a, w_proj_b, w_out, ln1_g, ln1_b, peer_w_query, peer_sub_keys_1, peer_sub_keys_2,
           peer_u, peer_v, ln2_g, ln2_b):
    batch, seq, _ = x_prompt.shape
    dbatch, dseq, _ = x_sample.shape
    assert seq % ATT_BLOCK == 0 and seq >= ATT_BLOCK

    w_in_p = _permute_in_proj(w_in)
    wpa, wpb, wout = w_proj_a.astype(BF16), w_proj_b.astype(BF16), w_out.astype(BF16)
    wq = peer_w_query.astype(BF16)
    sk1, sk2 = peer_sub_keys_1.astype(BF16), peer_sub_keys_2.astype(BF16)
    uv = jnp.concatenate([peer_u, peer_v], axis=1)
    ln1 = (ln1_g.reshape(1, D_MODEL), ln1_b.reshape(1, D_MODEL))
    ln2 = (ln2_g.reshape(1, D_MODEL), ln2_b.reshape(1, D_MODEL))
    sinks = b_sinks.astype(F32)

    def tail(xf, ya, yb, p, tm):
        h = _merge(ya, yb, p, xf, wpa, wpb, wout, *ln1, tm=tm)
        idx, gate = _route(h, wq, sk1, sk2, tm=tm)
        return _peer(h, idx, gate, uv, *ln2)

    xp = x_prompt.reshape(batch * seq, D_MODEL)
    p, kv = _in_proj(xp, w_in_p, tm=512)
    bias_p = _rel_bias(rel_bias_table, CHUNK, (A_PREV_CHUNKS + 1) * CHUNK)
    ya, yb = _attn_prompt(p, bias_p, sinks, batch, seq)
    y_prompt = tail(xp, ya, yb, p, 256).reshape(batch, seq, D_MODEL)
    ka, va, kb, vb = _split_kv(kv, batch, seq)
    la = min(A_PREV_CHUNKS * CHUNK, seq)
    lb = min(B_PREV_CHUNKS * CHUNK, seq)

    ns = dbatch * dseq
    xs = x_sample.reshape(ns, D_MODEL)
    p_s, kv_s = _in_proj(xs, w_in_p, tm=ns)
    ka_s, va_s, kb_s, vb_s = _split_kv(kv_s, dbatch, dseq)

    def with_cache(cache, new_cols, width):
        new = p_s[:, new_cols:new_cols + width].reshape(dbatch, dseq, width)
        return jnp.concatenate([cache.reshape(dbatch, -1, width).astype(BF16), new], axis=1)

    kcat_a = with_cache(cache_a_k, COL_KA, A_WIDTH)
    vcat_a = with_cache(cache_a_v, COL_VA, A_WIDTH)
    kcat_b = with_cache(cache_b_k, COL_KB, B_KV_WIDTH)
    vcat_b = with_cache(cache_b_v, COL_VB, B_KV_WIDTH)
    bias_s = _rel_bias(rel_bias_table, dseq, kcat_a.shape[1])
    ya_s, yb_s = _attn_sample(p_s, kcat_a, vcat_a, kcat_b, vcat_b, bias_s, sinks, dbatch, dseq)
    y_sample = tail(xs, ya_s, yb_s, p_s, ns).reshape(dbatch, dseq, D_MODEL)

    return (y_prompt, y_sample, ka[:, -la:], va[:, -la:], kb[:, -lb:], vb[:, -lb:], ka_s, va_s, kb_s, vb_s)
```

```python
import functools
import math

import jax
import jax.numpy as jnp
import numpy as np
from jax import lax
from jax.experimental import pallas as pl
from jax.experimental.pallas import tpu as pltpu

F32 = jnp.float32
BF16 = jnp.bfloat16

D_MODEL = 2048
CHUNK = 64
HEAD_DIM = 64
A_HEADS = 16
A_PREV_CHUNKS = 8
A_REL_CLIP = 128
B_Q_HEADS = 16
B_KV_HEADS = 2
B_PREV_CHUNKS = 2
A_WIDTH = A_HEADS * HEAD_DIM
B_WIDTH = B_Q_HEADS * HEAD_DIM
B_KV_WIDTH = B_KV_HEADS * HEAD_DIM
PEER_HEADS = 8
PEER_KEYS = 128
PEER_QDIM = 256
PEER_TOPK = 16
PEER_PICKS = PEER_HEADS * PEER_TOPK
LN_EPS = 1e-5
DEEPNORM_ALPHA = 2.0 ** 0.25

LANES = 128
HEAD_PAIR = 2 * HEAD_DIM
VMEM_LIMIT = 56 * 1024 * 1024

COL_GA = 0
COL_GB = COL_GA + D_MODEL
COL_QA = COL_GB + D_MODEL
COL_QB = COL_QA + A_WIDTH
COL_KA = COL_QB + B_WIDTH
COL_VA = COL_KA + A_WIDTH
COL_KB = COL_VA + A_WIDTH
COL_VB = COL_KB + B_KV_WIDTH
IN_COLS = COL_VB + B_KV_WIDTH
KV_COLS = IN_COLS - COL_KA
PROJ_TN = 768
ATT_BLOCK = A_PREV_CHUNKS * CHUNK


def _compiler_params(semantics):
    return pltpu.CompilerParams(dimension_semantics=semantics, vmem_limit_bytes=VMEM_LIMIT)


def _const_spec(shape):
    zeros = (0,) * len(shape)
    return pl.BlockSpec(shape, lambda *_: zeros, pipeline_mode=pl.Buffered(1))


def _in_proj_kernel(x_ref, w_ref, p_ref, kv_ref, xb_ref):
    j = pl.program_id(1)

    @pl.when(j == 0)
    def _():
        xb_ref[...] = x_ref[...].astype(BF16)

    acc = jnp.dot(xb_ref[...], w_ref[...], preferred_element_type=F32)
    p_ref[...] = acc.astype(BF16)

    @pl.when(j >= COL_KA // PROJ_TN)
    def _():
        kv_ref[...] = acc


def _in_proj(x, w_bf16, tm):
    n = x.shape[0]
    first_kv = COL_KA // PROJ_TN
    return pl.pallas_call(
        _in_proj_kernel,
        grid=(n // tm, IN_COLS // PROJ_TN),
        in_specs=[
            pl.BlockSpec((tm, D_MODEL), lambda i, j: (i, 0)),
            pl.BlockSpec((D_MODEL, PROJ_TN), lambda i, j: (0, j)),
        ],
        out_specs=[
            pl.BlockSpec((tm, PROJ_TN), lambda i, j: (i, j)),
            pl.BlockSpec((tm, PROJ_TN), lambda i, j: (i, jnp.maximum(j - first_kv, 0))),
        ],
        out_shape=[
            jax.ShapeDtypeStruct((n, IN_COLS), BF16),
            jax.ShapeDtypeStruct((n, KV_COLS), F32),
        ],
        scratch_shapes=[pltpu.VMEM((tm, D_MODEL), BF16)],
        compiler_params=_compiler_params(("parallel", "arbitrary")),
    )(x, w_bf16)


def _softmax_pv(s, v2, sink):
    m = jnp.max(s, axis=-1, keepdims=True)
    if sink is not None:
        m = jnp.maximum(m, sink)
    p = jnp.exp(s - m)
    den = jnp.sum(p, axis=-1, keepdims=True)
    if sink is not None:
        den = den + jnp.exp(sink - m)
    o = jnp.dot(p.astype(BF16), v2, preferred_element_type=F32)
    return o / den


def _qk(qm, k2):
    return lax.dot_general(qm, k2, (((1,), (1,)), ((), ())), preferred_element_type=F32) * (HEAD_DIM ** -0.5)


def _mixer_a_pairs(q_of, k_of, v_of, bias_of, key_thresh, store):
    q_shape = None
    for j in range(A_HEADS // 2):
        q2, k2, v2 = q_of(j), k_of(j), v_of(j)
        if q_shape is None:
            q_shape = q2.shape
            lane_hi = lax.broadcasted_iota(jnp.int32, q_shape, 1) >= HEAD_DIM
            col = lax.broadcasted_iota(jnp.int32, (q_shape[0], k2.shape[0]), 1)
            valid = None if key_thresh is None else col >= key_thresh
        out2 = None
        for hh in range(2):
            sel = lane_hi if hh else jnp.logical_not(lane_hi)
            qm = jnp.where(sel, q2, jnp.zeros_like(q2))
            s = _qk(qm, k2) + bias_of(2 * j + hh)
            if valid is not None:
                s = jnp.where(valid, s, -jnp.inf)
            o = _softmax_pv(s, v2, None)
            out2 = o if out2 is None else jnp.where(lane_hi, o, out2)
        store(j, out2)


def _alibi_slope(h):
    return 2.0 ** (-8.0 * (h + 1) / B_Q_HEADS)


def _mixer_b_pairs(q_of, k2, v2, absdist, sink_of, key_thresh, store):
    group = B_Q_HEADS // B_KV_HEADS
    lane_hi = None
    for j in range(B_Q_HEADS // 2):
        q2 = q_of(j)
        if lane_hi is None:
            lane_hi = lax.broadcasted_iota(jnp.int32, q2.shape, 1) >= HEAD_DIM
            col = lax.broadcasted_iota(jnp.int32, (q2.shape[0], k2.shape[0]), 1)
            valid = None if key_thresh is None else col >= key_thresh
        q2f = q2.astype(F32)
        q2_swapped = pltpu.roll(q2f, HEAD_DIM, 1).astype(BF16)
        out2 = None
        for hh in range(2):
            h = 2 * j + hh
            kv = h // group
            kv_sel = lane_hi if kv else jnp.logical_not(lane_hi)
            src = q2 if hh == kv else q2_swapped
            qm = jnp.where(kv_sel, src, jnp.zeros_like(src))
            s = _qk(qm, k2) - _alibi_slope(h) * absdist
            if valid is not None:
                s = jnp.where(valid, s, -jnp.inf)
            o = _softmax_pv(s, v2, sink_of(h))
            if hh != kv:
                o = pltpu.roll(o, HEAD_DIM, 1)
            out2 = o if out2 is None else jnp.where(lane_hi, o, out2)
        store(j, out2)


def _attn_prompt_kernel(sink_ref, qa_ref, kap_ref, kac_ref, vap_ref, vac_ref,
                        qb_ref, kbp_ref, kbc_ref, vbp_ref, vbc_ref, bias_ref,
                        ya_ref, yb_ref, ka_s, va_s, kb_s, vb_s):
    i = pl.program_id(1)
    blk = ATT_BLOCK
    ka_s[0:blk, :] = kap_ref[...]
    ka_s[blk:2 * blk, :] = kac_ref[...]
    va_s[0:blk, :] = vap_ref[...]
    va_s[blk:2 * blk, :] = vac_ref[...]
    kb_s[0:blk, :] = kbp_ref[...]
    kb_s[blk:2 * blk, :] = kbc_ref[...]
    vb_s[0:blk, :] = vbp_ref[...]
    vb_s[blk:2 * blk, :] = vbc_ref[...]

    tk_a = (A_PREV_CHUNKS + 1) * CHUNK
    tk_b = (B_PREV_CHUNKS + 1) * CHUNK
    b_off = blk - B_PREV_CHUNKS * CHUNK
    tq = lax.broadcasted_iota(jnp.int32, (CHUNK, tk_b), 0)
    ts = lax.broadcasted_iota(jnp.int32, (CHUNK, tk_b), 1)
    absdist = jnp.abs(tq + B_PREV_CHUNKS * CHUNK - ts).astype(F32)

    def chunk_body(c, carry):
        row0 = pl.multiple_of(c * CHUNK, CHUNK)
        thresh_a = jnp.where(i > 0, 0, blk - c * CHUNK)
        thresh_b = jnp.where(i > 0, 0, B_PREV_CHUNKS * CHUNK - c * CHUNK)

        def store_a(j, val):
            ya_ref[pl.ds(row0, CHUNK), j * HEAD_PAIR:(j + 1) * HEAD_PAIR] = val.astype(ya_ref.dtype)

        def store_b(j, val):
            yb_ref[pl.ds(row0, CHUNK), j * HEAD_PAIR:(j + 1) * HEAD_PAIR] = val.astype(yb_ref.dtype)

        _mixer_a_pairs(
            lambda j: qa_ref[pl.ds(row0, CHUNK), j * HEAD_PAIR:(j + 1) * HEAD_PAIR],
            lambda j: ka_s[pl.ds(row0, tk_a), j * HEAD_PAIR:(j + 1) * HEAD_PAIR],
            lambda j: va_s[pl.ds(row0, tk_a), j * HEAD_PAIR:(j + 1) * HEAD_PAIR],
            lambda h: bias_ref[h],
            thresh_a, store_a)

        rowb = pl.multiple_of(b_off + c * CHUNK, CHUNK)
        _mixer_b_pairs(
            lambda j: qb_ref[pl.ds(row0, CHUNK), j * HEAD_PAIR:(j + 1) * HEAD_PAIR],
            kb_s[pl.ds(rowb, tk_b), :], vb_s[pl.ds(rowb, tk_b), :],
            absdist, lambda h: sink_ref[h], thresh_b, store_b)
        return carry

    lax.fori_loop(0, blk // CHUNK, chunk_body, 0)


def _attn_prompt(p, bias_a, sinks, batch, seq):
    blk = ATT_BLOCK
    nb = seq // blk
    n = batch * seq

    def cur(col_block):
        return lambda b, i: (b * nb + i, col_block)

    def prev(col_block):
        return lambda b, i: (b * nb + jnp.maximum(i - 1, 0), col_block)

    wa, wb = A_WIDTH, B_KV_WIDTH
    in_specs = [
        pl.BlockSpec(memory_space=pltpu.SMEM),
        pl.BlockSpec((blk, wa), cur(COL_QA // wa)),
        pl.BlockSpec((blk, wa), prev(COL_KA // wa)),
        pl.BlockSpec((blk, wa), cur(COL_KA // wa)),
        pl.BlockSpec((blk, wa), prev(COL_VA // wa)),
        pl.BlockSpec((blk, wa), cur(COL_VA // wa)),
        pl.BlockSpec((blk, wa), cur(COL_QB // wa)),
        pl.BlockSpec((blk, wb), prev(COL_KB // wb)),
        pl.BlockSpec((blk, wb), cur(COL_KB // wb)),
        pl.BlockSpec((blk, wb), prev(COL_VB // wb)),
        pl.BlockSpec((blk, wb), cur(COL_VB // wb)),
        _const_spec(bias_a.shape),
    ]
    out_spec = pl.BlockSpec((blk, wa), lambda b, i: (b * nb + i, 0))
    return pl.pallas_call(
        _attn_prompt_kernel,
        grid=(batch, nb),
        in_specs=in_specs,
        out_specs=[out_spec, out_spec],
        out_shape=[jax.ShapeDtypeStruct((n, A_WIDTH), BF16), jax.ShapeDtypeStruct((n, B_WIDTH), BF16)],
        scratch_shapes=[
            pltpu.VMEM((2 * blk, wa), BF16), pltpu.VMEM((2 * blk, wa), BF16),
            pltpu.VMEM((2 * blk, wb), BF16), pltpu.VMEM((2 * blk, wb), BF16),
        ],
        compiler_params=_compiler_params(("parallel", "arbitrary")),
    )(sinks, p, p, p, p, p, p, p, p, p, p, bias_a)


def _attn_sample_kernel(sink_ref, qa_ref, ka_ref, va_ref, qb_ref, kb_ref, vb_ref, bias_ref, ya_ref, yb_ref):
    t, tk_b = qb_ref.shape[0], kb_ref.shape[1]

    def store_a(j, val):
        ya_ref[:, j * HEAD_PAIR:(j + 1) * HEAD_PAIR] = val.astype(ya_ref.dtype)

    def store_b(j, val):
        yb_ref[:, j * HEAD_PAIR:(j + 1) * HEAD_PAIR] = val.astype(yb_ref.dtype)

    _mixer_a_pairs(
        lambda j: qa_ref[:, j * HEAD_PAIR:(j + 1) * HEAD_PAIR],
        lambda j: ka_ref[0, :, j * HEAD_PAIR:(j + 1) * HEAD_PAIR],
        lambda j: va_ref[0, :, j * HEAD_PAIR:(j + 1) * HEAD_PAIR],
        lambda h: bias_ref[h], None, store_a)

    tq = lax.broadcasted_iota(jnp.int32, (t, tk_b), 0)
    ts = lax.broadcasted_iota(jnp.int32, (t, tk_b), 1)
    absdist = jnp.abs(tq + (tk_b - t) - ts).astype(F32)
    _mixer_b_pairs(
        lambda j: qb_ref[:, j * HEAD_PAIR:(j + 1) * HEAD_PAIR],
        kb_ref[0], vb_ref[0], absdist, lambda h: sink_ref[h], None, store_b)


def _attn_sample(p, kcat_a, vcat_a, kcat_b, vcat_b, bias_a, sinks, batch, t):
    n = batch * t
    wa, wb = A_WIDTH, B_KV_WIDTH
    tk_a, tk_b = kcat_a.shape[1], kcat_b.shape[1]
    in_specs = [
        pl.BlockSpec(memory_space=pltpu.SMEM),
        pl.BlockSpec((t, wa), lambda b: (b, COL_QA // wa)),
        pl.BlockSpec((1, tk_a, wa), lambda b: (b, 0, 0)),
        pl.BlockSpec((1, tk_a, wa), lambda b: (b, 0, 0)),
        pl.BlockSpec((t, wa), lambda b: (b, COL_QB // wa)),
        pl.BlockSpec((1, tk_b, wb), lambda b: (b, 0, 0)),
        pl.BlockSpec((1, tk_b, wb), lambda b: (b, 0, 0)),
        _const_spec(bias_a.shape),
    ]
    out_spec = pl.BlockSpec((t, wa), lambda b: (b, 0))
    return pl.pallas_call(
        _attn_sample_kernel,
        grid=(batch,),
        in_specs=in_specs,
        out_specs=[out_spec, out_spec],
        out_shape=[jax.ShapeDtypeStruct((n, A_WIDTH), BF16), jax.ShapeDtypeStruct((n, B_WIDTH), BF16)],
        compiler_params=_compiler_params(("parallel",)),
    )(sinks, p, kcat_a, vcat_a, p, kcat_b, vcat_b, bias_a)


def _layer_norm(x, g, b):
    mu = jnp.mean(x, axis=-1, keepdims=True)
    xc = x - mu
    var = jnp.mean(xc * xc, axis=-1, keepdims=True)
    return xc * lax.rsqrt(var + LN_EPS) * g + b


def _merge_kernel(ya_ref, yb_ref, ga_ref, gb_ref, x_ref, wpa_ref, wpb_ref, wout_ref, g_ref, b_ref, h_ref):
    za = jnp.dot(ya_ref[...], wpa_ref[...], preferred_element_type=F32)
    zb = jnp.dot(yb_ref[...], wpb_ref[...], preferred_element_type=F32)
    m = jax.nn.sigmoid(ga_ref[...].astype(F32)) * za + jax.nn.sigmoid(gb_ref[...].astype(F32)) * zb
    r = jnp.dot(m.astype(BF16), wout_ref[...], preferred_element_type=F32)
    h_ref[...] = _layer_norm(DEEPNORM_ALPHA * x_ref[...] + r, g_ref[...], b_ref[...])


def _merge(ya, yb, p, x, wpa, wpb, wout, ln_g, ln_b, tm):
    n = x.shape[0]
    row = lambda i: (i, 0)
    return pl.pallas_call(
        _merge_kernel,
        grid=(n // tm,),
        in_specs=[
            pl.BlockSpec((tm, A_WIDTH), row),
            pl.BlockSpec((tm, B_WIDTH), row),
            pl.BlockSpec((tm, D_MODEL), lambda i: (i, COL_GA // D_MODEL)),
            pl.BlockSpec((tm, D_MODEL), lambda i: (i, COL_GB // D_MODEL)),
            pl.BlockSpec((tm, D_MODEL), row),
            _const_spec(wpa.shape), _const_spec(wpb.shape), _const_spec(wout.shape),
            _const_spec(ln_g.shape), _const_spec(ln_b.shape),
        ],
        out_specs=pl.BlockSpec((tm, D_MODEL), row),
        out_shape=jax.ShapeDtypeStruct((n, D_MODEL), F32),
        compiler_params=_compiler_params(("parallel",)),
    )(ya, yb, p, p, x, wpa, wpb, wout, ln_g, ln_b)


def _topk_rows(s, k):
    r = s.shape[0]
    row = lax.broadcasted_iota(jnp.int32, s.shape, 0)
    vals, idxs = [], []
    for _ in range(k):
        m = jnp.max(s, axis=0, keepdims=True)
        ix = jnp.min(jnp.where(s == m, row, r), axis=0, keepdims=True)
        s = jnp.where(row == ix, -jnp.inf, s)
        vals.append(m)
        idxs.append(ix)
    return jnp.concatenate(vals, axis=0), jnp.concatenate(idxs, axis=0)


def _route_kernel(h_ref, wq_ref, sk1_ref, sk2_ref, idx_ref, gate_ref):
    q = jnp.dot(h_ref[...].astype(BF16), wq_ref[...], preferred_element_type=F32).astype(BF16)
    half = PEER_QDIM // 2
    nt = (((1,), (1,)), ((), ()))
    experts, gates = [], []
    for hd in range(PEER_HEADS):
        q1 = q[:, hd * PEER_QDIM:hd * PEER_QDIM + half]
        q2 = q[:, hd * PEER_QDIM + half:(hd + 1) * PEER_QDIM]
        s1 = lax.dot_general(sk1_ref[...], q1, nt, preferred_element_type=F32)
        s2 = lax.dot_general(sk2_ref[...], q2, nt, preferred_element_type=F32)
        v1, i1 = _topk_rows(s1, PEER_TOPK)
        v2, i2 = _topk_rows(s2, PEER_TOPK)
        cand = jnp.concatenate([v1[a:a + 1] + v2 for a in range(PEER_TOPK)], axis=0)
        sc, ci = _topk_rows(cand, PEER_TOPK)
        ca = lax.shift_right_logical(ci, int(math.log2(PEER_TOPK)))
        cb = lax.bitwise_and(ci, PEER_TOPK - 1)
        e1 = jnp.zeros_like(ci)
        e2 = jnp.zeros_like(ci)
        for a in range(PEER_TOPK):
            e1 = jnp.where(ca == a, i1[a:a + 1], e1)
            e2 = jnp.where(cb == a, i2[a:a + 1], e2)
        experts.append(e1 * PEER_KEYS + e2)
        e = jnp.exp(sc - sc[0:1])
        gates.append(e / jnp.sum(e, axis=0, keepdims=True))
    idx_ref[...] = jnp.concatenate(experts, axis=0).astype(F32).T.astype(jnp.int32)
    gate_ref[...] = jnp.concatenate(gates, axis=0).T


def _route(h, wq, sk1, sk2, tm):
    n = h.shape[0]
    row = lambda i: (i, 0)
    return pl.pallas_call(
        _route_kernel,
        grid=(n // tm,),
        in_specs=[pl.BlockSpec((tm, D_MODEL), row), _const_spec(wq.shape),
                  _const_spec(sk1.shape), _const_spec(sk2.shape)],
        out_specs=[pl.BlockSpec((tm, PEER_PICKS), row), pl.BlockSpec((tm, PEER_PICKS), row)],
        out_shape=[jax.ShapeDtypeStruct((n, PEER_PICKS), jnp.int32),
                   jax.ShapeDtypeStruct((n, PEER_PICKS), F32)],
        compiler_params=_compiler_params(("parallel",)),
    )(h, wq, sk1, sk2)


PEER_TB = 32
PEER_SLOTS = 8
PEER_LOOKAHEAD = PEER_SLOTS - 1


def _gelu_exact(x):
    return 0.5 * x * (1.0 + lax.erf(x * (2.0 ** -0.5)))


def _peer_kernel(idx_ref, gate_ref, h_ref, uv_ref, g_ref, b_ref, y_ref, buf, sem, ffn):
    i = pl.program_id(0)
    last = pl.num_programs(0) - 1
    tb = h_ref.shape[0]

    def issue(t, slot):
        for k in range(PEER_PICKS):
            e = idx_ref[0, t, k]
            pltpu.make_async_copy(uv_ref.at[pl.ds(e, 1), :], buf.at[slot, pl.ds(k, 1), :], sem.at[slot]).start()

    def wait_slot(slot):
        pltpu.make_async_copy(uv_ref.at[pl.ds(0, PEER_PICKS), :], buf.at[slot], sem.at[slot]).wait()

    @pl.when(i == 0)
    def _():
        for t in range(PEER_LOOKAHEAD):
            issue(t, t)

    eye = (lax.broadcasted_iota(jnp.int32, (PEER_PICKS, PEER_PICKS), 0)
           == lax.broadcasted_iota(jnp.int32, (PEER_PICKS, PEER_PICKS), 1))

    def token_body(t, carry):
        g = i * tb + t
        slot = lax.bitwise_and(g, PEER_SLOTS - 1)
        wait_slot(slot)
        issue(t + PEER_LOOKAHEAD, lax.bitwise_and(g + PEER_LOOKAHEAD, PEER_SLOTS - 1))
        h_row = h_ref[pl.ds(t, 1), :]
        u = buf[slot, :, 0:D_MODEL]
        a_col = jnp.sum(u * h_row, axis=1, keepdims=True)
        a_row = jnp.sum(jnp.where(eye, a_col, 0.0), axis=0, keepdims=True)
        w_row = gate_ref[pl.ds(t, 1), :] * _gelu_exact(a_row)
        w_col = jnp.sum(jnp.where(eye, w_row, 0.0), axis=1, keepdims=True)
        v = buf[slot, :, D_MODEL:2 * D_MODEL]
        ffn[pl.ds(t, 1), :] = jnp.sum(w_col * v, axis=0, keepdims=True)
        return carry

    lax.fori_loop(0, tb, token_body, 0)

    @pl.when(i == last)
    def _():
        for t in range(PEER_LOOKAHEAD):
            wait_slot(lax.bitwise_and(i * tb + tb + t, PEER_SLOTS - 1))

    y_ref[...] = _layer_norm(DEEPNORM_ALPHA * h_ref[...] + ffn[...], g_ref[...], b_ref[...])


def _peer(h, idx, gate, uv, ln_g, ln_b):
    n = h.shape[0]
    tb = min(PEER_TB, n)
    nb = n // tb
    row = lambda i: (i, 0)
    idx_pad = jnp.concatenate([idx, jnp.zeros((tb, PEER_PICKS), idx.dtype)], axis=0)
    idx_next = idx_pad[tb:].reshape(nb, tb, PEER_PICKS)[:, :PEER_LOOKAHEAD]
    idx_ext = jnp.concatenate([idx.reshape(nb, tb, PEER_PICKS), idx_next], axis=1)
    return pl.pallas_call(
        _peer_kernel,
        grid=(nb,),
        in_specs=[
            pl.BlockSpec((1, tb + PEER_LOOKAHEAD, PEER_PICKS), lambda i: (i, 0, 0), memory_space=pltpu.SMEM),
            pl.BlockSpec((tb, PEER_PICKS), row),
            pl.BlockSpec((tb, D_MODEL), row),
            pl.BlockSpec(memory_space=pl.ANY),
            _const_spec(ln_g.shape), _const_spec(ln_b.shape),
        ],
        out_specs=pl.BlockSpec((tb, D_MODEL), row),
        out_shape=jax.ShapeDtypeStruct((n, D_MODEL), F32),
        scratch_shapes=[
            pltpu.VMEM((PEER_SLOTS, PEER_PICKS, 2 * D_MODEL), F32),
            pltpu.SemaphoreType.DMA((PEER_SLOTS,)),
            pltpu.VMEM((tb, D_MODEL), F32),
        ],
        compiler_params=_compiler_params(("arbitrary",)),
    )(idx_ext, gate, h, uv, ln_g, ln_b)


def _rel_bias(rel_table, t, tk):
    dist = jnp.arange(t)[:, None] + (tk - t) - jnp.arange(tk)[None, :]
    idx = jnp.clip(dist, -A_REL_CLIP, A_REL_CLIP) + A_REL_CLIP
    return jnp.take(rel_table, idx, axis=1).astype(F32)


def _permute_in_proj(w_in):
    offs = np.cumsum([0, A_WIDTH, A_WIDTH, A_WIDTH, B_WIDTH, B_KV_WIDTH, B_KV_WIDTH, D_MODEL, D_MODEL])
    qa, ka, va, qb, kb, vb, ga, gb = [w_in[:, offs[s]:offs[s + 1]] for s in range(8)]
    return jnp.concatenate([ga, gb, qa, qb, ka, va, kb, vb], axis=1).astype(BF16)


def _split_kv(kv, batch, t):
    ka = kv[:, 0:A_WIDTH].reshape(batch, t, A_HEADS, HEAD_DIM)
    va = kv[:, A_WIDTH:2 * A_WIDTH].reshape(batch, t, A_HEADS, HEAD_DIM)
    kb = kv[:, 2 * A_WIDTH:2 * A_WIDTH + B_KV_WIDTH].reshape(batch, t, B_KV_HEADS, HEAD_DIM)
    vb = kv[:, 2 * A_WIDTH + B_KV_WIDTH:].reshape(batch, t, B_KV_HEADS, HEAD_DIM)
    return ka, va, kb, vb


def kernel(x_prompt, x_sample, cache_a_k, cache_a_v, cache_b_k, cache_b_v, w_in, rel_bias_table, b_sinks,
           w_proj_a, w_proj_b, w_out, ln1_g, ln1_b, peer_w_query, peer_sub_keys_1, peer_sub_keys_2,
           peer_u, peer_v, ln2_g, ln2_b):
    batch, seq, _ = x_prompt.shape
    dbatch, dseq, _ = x_sample.shape
    assert seq % ATT_BLOCK == 0 and seq >= ATT_BLOCK

    w_in_p = _permute_in_proj(w_in)
    wpa, wpb, wout = w_proj_a.astype(BF16), w_proj_b.astype(BF16), w_out.astype(BF16)
    wq = peer_w_query.astype(BF16)
    sk1, sk2 = peer_sub_keys_1.astype(BF16), peer_sub_keys_2.astype(BF16)
    uv = jnp.concatenate([peer_u, peer_v], axis=1)
    ln1 = (ln1_g.reshape(1, D_MODEL), ln1_b.reshape(1, D_MODEL))
    ln2 = (ln2_g.reshape(1, D_MODEL), ln2_b.reshape(1, D_MODEL))
    sinks = b_sinks.astype(F32)

    def tail(xf, ya, yb, p, tm):
        h = _merge(ya, yb, p, xf, wpa, wpb, wout, *ln1, tm=tm)
        idx, gate = _route(h, wq, sk1, sk2, tm=tm)
        return _peer(h, idx, gate, uv, *ln2)

    ns = dbatch * dseq
    xs = x_sample.reshape(ns, D_MODEL)
    p_s, kv_s = _in_proj(xs, w_in_p, tm=ns)
    ka_s, va_s, kb_s, vb_s = _split_kv(kv_s, dbatch, dseq)

    def with_cache(cache, new_cols, width):
        new = p_s[:, new_cols:new_cols + width].reshape(dbatch, dseq, width)
        return jnp.concatenate([cache.reshape(dbatch, -1, width).astype(BF16), new], axis=1)

    kcat_a = with_cache(cache_a_k, COL_KA, A_WIDTH)
    vcat_a = with_cache(cache_a_v, COL_VA, A_WIDTH)
    kcat_b = with_cache(cache_b_k, COL_KB, B_KV_WIDTH)
    vcat_b = with_cache(cache_b_v, COL_VB, B_KV_WIDTH)
    bias_s = _rel_bias(rel_bias_table, dseq, kcat_a.shape[1])
    ya_s, yb_s = _attn_sample(p_s, kcat_a, vcat_a, kcat_b, vcat_b, bias_s, sinks, dbatch, dseq)
    y_sample = tail(xs, ya_s, yb_s, p_s, ns).reshape(dbatch, dseq, D_MODEL)

    xp = x_prompt.reshape(batch * seq, D_MODEL)
    p, kv = _in_proj(xp, w_in_p, tm=512)
    bias_p = _rel_bias(rel_bias_table, CHUNK, (A_PREV_CHUNKS + 1) * CHUNK)
    ya, yb = _attn_prompt(p, bias_p, sinks, batch, seq)
    y_prompt = tail(xp, ya, yb, p, 256).reshape(batch, seq, D_MODEL)
    ka, va, kb, vb = _split_kv(kv, batch, seq)
    la = min(A_PREV_CHUNKS * CHUNK, seq)
    lb = min(B_PREV_CHUNKS * CHUNK, seq)

    return (y_prompt, y_sample, ka[:, -la:], va[:, -la:], kb[:, -lb:], vb[:, -lb:], ka_s, va_s, kb_s, vb_s)
```

```python
import math

import jax
import jax.numpy as jnp
import numpy as np
from jax import lax
from jax.experimental import pallas as pl
from jax.experimental.pallas import tpu as pltpu

F32 = jnp.float32
BF16 = jnp.bfloat16

D_MODEL = 2048
CHUNK = 64
HEAD_DIM = 64
A_HEADS = 16
A_PREV_CHUNKS = 8
A_REL_CLIP = 128
B_Q_HEADS = 16
B_KV_HEADS = 2
B_PREV_CHUNKS = 2
A_WIDTH = A_HEADS * HEAD_DIM
B_WIDTH = B_Q_HEADS * HEAD_DIM
B_KV_WIDTH = B_KV_HEADS * HEAD_DIM
PEER_HEADS = 8
PEER_KEYS = 128
PEER_QDIM = 256
PEER_TOPK = 16
PEER_PICKS = PEER_HEADS * PEER_TOPK
LN_EPS = 1e-5
DEEPNORM_ALPHA = 2.0 ** 0.25

LANES = 128
SUBLANES = 8
HEAD_PAIR = 2 * HEAD_DIM
VMEM_LIMIT = 56 * 1024 * 1024

COL_GA = 0
COL_GB = COL_GA + D_MODEL
COL_QA = COL_GB + D_MODEL
COL_QB = COL_QA + A_WIDTH
COL_KA = COL_QB + B_WIDTH
COL_VA = COL_KA + A_WIDTH
COL_KB = COL_VA + A_WIDTH
COL_VB = COL_KB + B_KV_WIDTH
IN_COLS = COL_VB + B_KV_WIDTH
KV_COLS = IN_COLS - COL_KA
PROJ_TN = 768
ATT_BLOCK = A_PREV_CHUNKS * CHUNK

NT_DIMS = (((1,), (1,)), ((), ()))


def _compiler_params(semantics):
    return pltpu.CompilerParams(dimension_semantics=semantics, vmem_limit_bytes=VMEM_LIMIT)


def _const_spec(shape):
    zeros = (0,) * len(shape)
    return pl.BlockSpec(shape, lambda *_: zeros, pipeline_mode=pl.Buffered(1))


def _in_proj_kernel(x_ref, w_ref, p_ref, kv_ref, xb_ref):
    j = pl.program_id(1)

    @pl.when(j == 0)
    def _():
        xb_ref[...] = x_ref[...].astype(BF16)

    acc = jnp.dot(xb_ref[...], w_ref[...], preferred_element_type=F32)
    p_ref[...] = acc.astype(BF16)

    @pl.when(j >= COL_KA // PROJ_TN)
    def _():
        kv_ref[...] = acc


def _in_proj(x, w_bf16, tm):
    n = x.shape[0]
    first_kv = COL_KA // PROJ_TN
    return pl.pallas_call(
        _in_proj_kernel,
        grid=(n // tm, IN_COLS // PROJ_TN),
        in_specs=[
            pl.BlockSpec((tm, D_MODEL), lambda i, j: (i, 0)),
            pl.BlockSpec((D_MODEL, PROJ_TN), lambda i, j: (0, j)),
        ],
        out_specs=[
            pl.BlockSpec((tm, PROJ_TN), lambda i, j: (i, j)),
            pl.BlockSpec((tm, PROJ_TN), lambda i, j: (i, jnp.maximum(j - first_kv, 0))),
        ],
        out_shape=[
            jax.ShapeDtypeStruct((n, IN_COLS), BF16),
            jax.ShapeDtypeStruct((n, KV_COLS), F32),
        ],
        scratch_shapes=[pltpu.VMEM((tm, D_MODEL), BF16)],
        compiler_params=_compiler_params(("parallel", "arbitrary")),
    )(x, w_bf16)


def _softmax_pv(s, v2, sink):
    m = jnp.max(s, axis=-1, keepdims=True)
    if sink is not None:
        m = jnp.maximum(m, sink)
    p = jnp.exp(s - m)
    den = jnp.sum(p, axis=-1, keepdims=True)
    if sink is not None:
        den = den + jnp.exp(sink - m)
    o = jnp.dot(p.astype(BF16), v2, preferred_element_type=F32)
    return o / den


def _scores(qq, k2, bias, key_thresh):
    s = lax.dot_general(qq, k2, NT_DIMS, preferred_element_type=F32) * (HEAD_DIM ** -0.5) + bias
    if key_thresh is not None:
        col = lax.broadcasted_iota(jnp.int32, s.shape, 1)
        s = jnp.where(col >= key_thresh, s, -jnp.inf)
    return s


def _mixer_a(q_of, k_of, v_of, bias_of, key_thresh, store):
    for j in range(A_HEADS // 2):
        q2 = q_of(j)
        tq = q2.shape[0]
        lane_hi = lax.broadcasted_iota(jnp.int32, q2.shape, 1) >= HEAD_DIM
        zero = jnp.zeros_like(q2)
        qq = jnp.concatenate([jnp.where(lane_hi, zero, q2), jnp.where(lane_hi, q2, zero)], axis=0)
        o = _softmax_pv(_scores(qq, k_of(j), bias_of(j), key_thresh), v_of(j), None)
        store(j, jnp.where(lane_hi, o[tq:2 * tq], o[0:tq]))


def _alibi_rows(absdist):
    return jnp.concatenate([(-(2.0 ** (-8.0 * (h + 1) / B_Q_HEADS))) * absdist for h in range(B_Q_HEADS)], axis=0)


def _sink_rows(sink_ref, tq):
    return jnp.concatenate([jnp.full((tq, 1), sink_ref[h], F32) for h in range(B_Q_HEADS)], axis=0)


def _mixer_b(q_of, k2, v2, bias_rows, sink_rows, key_thresh, store):
    group = B_Q_HEADS // B_KV_HEADS
    pieces = []
    for j in range(B_Q_HEADS // 2):
        q2 = q_of(j)
        tq = q2.shape[0]
        lane_hi = lax.broadcasted_iota(jnp.int32, q2.shape, 1) >= HEAD_DIM
        zero = jnp.zeros_like(q2)
        q2_swapped = pltpu.roll(q2.astype(F32), HEAD_DIM, 1).astype(BF16)
        for hh in range(2):
            kv = (2 * j + hh) // group
            src = q2 if hh == kv else q2_swapped
            pieces.append(jnp.where(lane_hi, src, zero) if kv else jnp.where(lane_hi, zero, src))
    qq = jnp.concatenate(pieces, axis=0)
    o = _softmax_pv(_scores(qq, k2, bias_rows, key_thresh), v2, sink_rows)
    lane_hi = lax.broadcasted_iota(jnp.int32, (tq, HEAD_PAIR), 1) >= HEAD_DIM
    for j in range(B_Q_HEADS // 2):
        halves = []
        for hh in range(2):
            h = 2 * j + hh
            o_h = o[h * tq:(h + 1) * tq]
            halves.append(o_h if hh == h // group else pltpu.roll(o_h, HEAD_DIM, 1))
        store(j, jnp.where(lane_hi, halves[1], halves[0]))


def _attn_prompt_kernel(sink_ref, qa_ref, kap_ref, kac_ref, vap_ref, vac_ref,
                        qb_ref, kbp_ref, kbc_ref, vbp_ref, vbc_ref, bias_ref,
                        ya_ref, yb_ref, ka_s, va_s, kb_s, vb_s, bias_b_s, sink_s):
    i = pl.program_id(1)
    blk = ATT_BLOCK
    ka_s[0:blk, :] = kap_ref[...]
    ka_s[blk:2 * blk, :] = kac_ref[...]
    va_s[0:blk, :] = vap_ref[...]
    va_s[blk:2 * blk, :] = vac_ref[...]
    kb_s[0:blk, :] = kbp_ref[...]
    kb_s[blk:2 * blk, :] = kbc_ref[...]
    vb_s[0:blk, :] = vbp_ref[...]
    vb_s[blk:2 * blk, :] = vbc_ref[...]

    tk_a = (A_PREV_CHUNKS + 1) * CHUNK
    tk_b = (B_PREV_CHUNKS + 1) * CHUNK
    b_off = blk - B_PREV_CHUNKS * CHUNK
    tq = lax.broadcasted_iota(jnp.int32, (CHUNK, tk_b), 0)
    ts = lax.broadcasted_iota(jnp.int32, (CHUNK, tk_b), 1)
    bias_b_s[...] = _alibi_rows(jnp.abs(tq + B_PREV_CHUNKS * CHUNK - ts).astype(F32))
    sink_s[...] = _sink_rows(sink_ref, CHUNK)

    def chunk_body(c, carry):
        row0 = pl.multiple_of(c * CHUNK, CHUNK)
        thresh_a = jnp.where(i > 0, 0, blk - c * CHUNK)
        thresh_b = jnp.where(i > 0, 0, B_PREV_CHUNKS * CHUNK - c * CHUNK)

        def store_a(j, val):
            ya_ref[pl.ds(row0, CHUNK), j * HEAD_PAIR:(j + 1) * HEAD_PAIR] = val.astype(ya_ref.dtype)

        def store_b(j, val):
            yb_ref[pl.ds(row0, CHUNK), j * HEAD_PAIR:(j + 1) * HEAD_PAIR] = val.astype(yb_ref.dtype)

        _mixer_a(
            lambda j: qa_ref[pl.ds(row0, CHUNK), j * HEAD_PAIR:(j + 1) * HEAD_PAIR],
            lambda j: ka_s[pl.ds(row0, tk_a), j * HEAD_PAIR:(j + 1) * HEAD_PAIR],
            lambda j: va_s[pl.ds(row0, tk_a), j * HEAD_PAIR:(j + 1) * HEAD_PAIR],
            lambda j: bias_ref[j], thresh_a, store_a)

        rowb = pl.multiple_of(b_off + c * CHUNK, CHUNK)
        _mixer_b(
            lambda j: qb_ref[pl.ds(row0, CHUNK), j * HEAD_PAIR:(j + 1) * HEAD_PAIR],
            kb_s[pl.ds(rowb, tk_b), :], vb_s[pl.ds(rowb, tk_b), :],
            bias_b_s[...], sink_s[...], thresh_b, store_b)
        return carry

    lax.fori_loop(0, blk // CHUNK, chunk_body, 0)


def _attn_prompt(p, bias_a, sinks, batch, seq):
    blk = ATT_BLOCK
    nb = seq // blk
    n = batch * seq

    def cur(col_block):
        return lambda b, i: (b * nb + i, col_block)

    def prev(col_block):
        return lambda b, i: (b * nb + jnp.maximum(i - 1, 0), col_block)

    wa, wb = A_WIDTH, B_KV_WIDTH
    in_specs = [
        pl.BlockSpec(memory_space=pltpu.SMEM),
        pl.BlockSpec((blk, wa), cur(COL_QA // wa)),
        pl.BlockSpec((blk, wa), prev(COL_KA // wa)),
        pl.BlockSpec((blk, wa), cur(COL_KA // wa)),
        pl.BlockSpec((blk, wa), prev(COL_VA // wa)),
        pl.BlockSpec((blk, wa), cur(COL_VA // wa)),
        pl.BlockSpec((blk, wa), cur(COL_QB // wa)),
        pl.BlockSpec((blk, wb), prev(COL_KB // wb)),
        pl.BlockSpec((blk, wb), cur(COL_KB // wb)),
        pl.BlockSpec((blk, wb), prev(COL_VB // wb)),
        pl.BlockSpec((blk, wb), cur(COL_VB // wb)),
        _const_spec(bias_a.shape),
    ]
    out_spec = pl.BlockSpec((blk, wa), lambda b, i: (b * nb + i, 0))
    rows_b = B_Q_HEADS * CHUNK
    return pl.pallas_call(
        _attn_prompt_kernel,
        grid=(batch, nb),
        in_specs=in_specs,
        out_specs=[out_spec, out_spec],
        out_shape=[jax.ShapeDtypeStruct((n, A_WIDTH), BF16), jax.ShapeDtypeStruct((n, B_WIDTH), BF16)],
        scratch_shapes=[
            pltpu.VMEM((2 * blk, wa), BF16), pltpu.VMEM((2 * blk, wa), BF16),
            pltpu.VMEM((2 * blk, wb), BF16), pltpu.VMEM((2 * blk, wb), BF16),
            pltpu.VMEM((rows_b, (B_PREV_CHUNKS + 1) * CHUNK), F32), pltpu.VMEM((rows_b, 1), F32),
        ],
        compiler_params=_compiler_params(("parallel", "arbitrary")),
    )(sinks, p, p, p, p, p, p, p, p, p, p, bias_a)


def _attn_sample_kernel(sink_ref, qa_ref, ka_ref, va_ref, qb_ref, kb_ref, vb_ref, bias_ref, ya_ref, yb_ref):
    t, tk_b = qb_ref.shape[0], kb_ref.shape[1]

    def store_a(j, val):
        ya_ref[:, j * HEAD_PAIR:(j + 1) * HEAD_PAIR] = val.astype(ya_ref.dtype)

    def store_b(j, val):
        yb_ref[:, j * HEAD_PAIR:(j + 1) * HEAD_PAIR] = val.astype(yb_ref.dtype)

    _mixer_a(
        lambda j: qa_ref[:, j * HEAD_PAIR:(j + 1) * HEAD_PAIR],
        lambda j: ka_ref[0, :, j * HEAD_PAIR:(j + 1) * HEAD_PAIR],
        lambda j: va_ref[0, :, j * HEAD_PAIR:(j + 1) * HEAD_PAIR],
        lambda j: bias_ref[j], None, store_a)

    tq = lax.broadcasted_iota(jnp.int32, (t, tk_b), 0)
    ts = lax.broadcasted_iota(jnp.int32, (t, tk_b), 1)
    bias_rows = _alibi_rows(jnp.abs(tq + (tk_b - t) - ts).astype(F32))
    _mixer_b(
        lambda j: qb_ref[:, j * HEAD_PAIR:(j + 1) * HEAD_PAIR],
        kb_ref[0], vb_ref[0], bias_rows, _sink_rows(sink_ref, t), None, store_b)


def _attn_sample(p, kcat_a, vcat_a, kcat_b, vcat_b, bias_a, sinks, batch, t):
    n = batch * t
    wa, wb = A_WIDTH, B_KV_WIDTH
    tk_a, tk_b = kcat_a.shape[1], kcat_b.shape[1]
    in_specs = [
        pl.BlockSpec(memory_space=pltpu.SMEM),
        pl.BlockSpec((t, wa), lambda b: (b, COL_QA // wa)),
        pl.BlockSpec((1, tk_a, wa), lambda b: (b, 0, 0)),
        pl.BlockSpec((1, tk_a, wa), lambda b: (b, 0, 0)),
        pl.BlockSpec((t, wa), lambda b: (b, COL_QB // wa)),
        pl.BlockSpec((1, tk_b, wb), lambda b: (b, 0, 0)),
        pl.BlockSpec((1, tk_b, wb), lambda b: (b, 0, 0)),
        _const_spec(bias_a.shape),
    ]
    out_spec = pl.BlockSpec((t, wa), lambda b: (b, 0))
    return pl.pallas_call(
        _attn_sample_kernel,
        grid=(batch,),
        in_specs=in_specs,
        out_specs=[out_spec, out_spec],
        out_shape=[jax.ShapeDtypeStruct((n, A_WIDTH), BF16), jax.ShapeDtypeStruct((n, B_WIDTH), BF16)],
        compiler_params=_compiler_params(("parallel",)),
    )(sinks, p, kcat_a, vcat_a, p, kcat_b, vcat_b, bias_a)


def _layer_norm(x, g, b):
    mu = jnp.mean(x, axis=-1, keepdims=True)
    xc = x - mu
    var = jnp.mean(xc * xc, axis=-1, keepdims=True)
    return xc * lax.rsqrt(var + LN_EPS) * g + b


def _merge_kernel(ya_ref, yb_ref, ga_ref, gb_ref, x_ref, wpa_ref, wpb_ref, wout_ref, g_ref, b_ref, h_ref):
    za = jnp.dot(ya_ref[...], wpa_ref[...], preferred_element_type=F32)
    zb = jnp.dot(yb_ref[...], wpb_ref[...], preferred_element_type=F32)
    m = jax.nn.sigmoid(ga_ref[...].astype(F32)) * za + jax.nn.sigmoid(gb_ref[...].astype(F32)) * zb
    r = jnp.dot(m.astype(BF16), wout_ref[...], preferred_element_type=F32)
    h_ref[...] = _layer_norm(DEEPNORM_ALPHA * x_ref[...] + r, g_ref[...], b_ref[...])


def _merge(ya, yb, p, x, wpa, wpb, wout, ln_g, ln_b, tm):
    n = x.shape[0]
    row = lambda i: (i, 0)
    return pl.pallas_call(
        _merge_kernel,
        grid=(n // tm,),
        in_specs=[
            pl.BlockSpec((tm, A_WIDTH), row),
            pl.BlockSpec((tm, B_WIDTH), row),
            pl.BlockSpec((tm, D_MODEL), lambda i: (i, COL_GA // D_MODEL)),
            pl.BlockSpec((tm, D_MODEL), lambda i: (i, COL_GB // D_MODEL)),
            pl.BlockSpec((tm, D_MODEL), row),
            _const_spec(wpa.shape), _const_spec(wpb.shape), _const_spec(wout.shape),
            _const_spec(ln_g.shape), _const_spec(ln_b.shape),
        ],
        out_specs=pl.BlockSpec((tm, D_MODEL), row),
        out_shape=jax.ShapeDtypeStruct((n, D_MODEL), F32),
        compiler_params=_compiler_params(("parallel",)),
    )(ya, yb, p, p, x, wpa, wpb, wout, ln_g, ln_b)


ROUTE_TM = 128
ID_SENTINEL = 1 << 30


def _topk_rows(s, ids, k):
    vals, picked = [], []
    for _ in range(k):
        m = jnp.max(s, axis=0, keepdims=True)
        ix = jnp.min(jnp.where(s == m, ids, ID_SENTINEL), axis=0, keepdims=True)
        s = jnp.where(ids == ix, -jnp.inf, s)
        vals.append(m)
        picked.append(ix)
    return jnp.concatenate(vals, axis=0), jnp.concatenate(picked, axis=0)


def _candidate_blocks(v1, v2):
    t = v1.shape[1]
    sub = lax.broadcasted_iota(jnp.int32, (SUBLANES, t), 0)
    neg = jnp.full((SUBLANES, t), -jnp.inf, F32)
    k = PEER_TOPK
    blocks = []
    blocks.append((v1[0:1] + v2[0:8], sub))
    blocks.append((v1[0:1] + v2[8:16], sub + 8))
    blocks.append((v1[1:2] + v2[0:8], sub + k))
    blocks.append((jnp.where(sub < 5, v1[2:3] + v2[0:8], neg), sub + 2 * k))
    blocks.append((jnp.where(sub < 4, v1[3:4] + v2[0:8], neg), sub + 3 * k))
    blocks.append((v1[8:16] + v2[0:1], (sub + 8) * k))
    a47 = jnp.concatenate([v1[4:8], v1[4:8]], axis=0)
    blocks.append((a47 + jnp.where(sub < 4, v2[0:1], v2[1:2]),
                   (4 + lax.bitwise_and(sub, 3)) * k + lax.shift_right_logical(sub, 2)))
    blocks.append((jnp.where(sub == 0, v1[4:5] + v2[2:3], neg), sub + 4 * k + 2))
    vals = jnp.concatenate([b[0] for b in blocks], axis=0)
    ids = jnp.concatenate([b[1] for b in blocks], axis=0)
    return vals, ids


def _route_kernel(h_ref, wq_ref, sk1_ref, sk2_ref, idx_ref, gate_ref):
    q = jnp.dot(h_ref[...].astype(BF16), wq_ref[...], preferred_element_type=F32).astype(BF16)
    half = PEER_QDIM // 2
    key_ids = lax.broadcasted_iota(jnp.int32, (PEER_KEYS, h_ref.shape[0]), 0)
    experts, gates = [], []
    for hd in range(PEER_HEADS):
        q1 = q[:, hd * PEER_QDIM:hd * PEER_QDIM + half]
        q2 = q[:, hd * PEER_QDIM + half:(hd + 1) * PEER_QDIM]
        s1 = lax.dot_general(sk1_ref[...], q1, NT_DIMS, preferred_element_type=F32)
        s2 = lax.dot_general(sk2_ref[...], q2, NT_DIMS, preferred_element_type=F32)
        v1, i1 = _topk_rows(s1, key_ids, PEER_TOPK)
        v2, i2 = _topk_rows(s2, key_ids, PEER_TOPK)
        sc, ci = _topk_rows(*_candidate_blocks(v1, v2), PEER_TOPK)
        ca = lax.shift_right_logical(ci, int(math.log2(PEER_TOPK)))
        cb = lax.bitwise_and(ci, PEER_TOPK - 1)
        e1 = jnp.zeros_like(ci)
        e2 = jnp.zeros_like(ci)
        for a in range(PEER_TOPK):
            e1 = jnp.where(ca == a, i1[a:a + 1], e1)
            e2 = jnp.where(cb == a, i2[a:a + 1], e2)
        experts.append(e1 * PEER_KEYS + e2)
        e = jnp.exp(sc - sc[0:1])
        gates.append(e / jnp.sum(e, axis=0, keepdims=True))
    idx_ref[...] = jnp.concatenate(experts, axis=0).astype(F32).T.astype(jnp.int32)
    gate_ref[...] = jnp.concatenate(gates, axis=0).T


def _route(h, wq, sk1, sk2):
    n = h.shape[0]
    tm = ROUTE_TM
    row = lambda i: (i, 0)
    return pl.pallas_call(
        _route_kernel,
        grid=(n // tm,),
        in_specs=[pl.BlockSpec((tm, D_MODEL), row), _const_spec(wq.shape),
                  _const_spec(sk1.shape), _const_spec(sk2.shape)],
        out_specs=[pl.BlockSpec((tm, PEER_PICKS), row), pl.BlockSpec((tm, PEER_PICKS), row)],
        out_shape=[jax.ShapeDtypeStruct((n, PEER_PICKS), jnp.int32),
                   jax.ShapeDtypeStruct((n, PEER_PICKS), F32)],
        compiler_params=_compiler_params(("parallel",)),
    )(h, wq, sk1, sk2)


PEER_TB = 32
PEER_SLOTS = 8
PEER_LOOKAHEAD = PEER_SLOTS - 1

def _gelu_exact(x):
    return 0.5 * x * (1.0 + lax.erf(x * (2.0 ** -0.5)))


def _peer_kernel(idx_ref, gate_ref, h_ref, uv_ref, g_ref, b_ref, y_ref, *scratch):
    bufs, (sem, ffn) = scratch[:PEER_SLOTS], scratch[PEER_SLOTS:]
    i = pl.program_id(0)
    last = pl.num_programs(0) - 1
    tb = h_ref.shape[0]

    def issue(t, slot):
        for k in range(PEER_PICKS):
            e = idx_ref[0, t, k]
            pltpu.make_async_copy(uv_ref.at[pl.ds(e, 1), :], bufs[slot].at[pl.ds(k, 1), :], sem.at[slot]).start()

    def wait_slot(slot):
        pltpu.make_async_copy(uv_ref.at[pl.ds(0, PEER_PICKS), :], bufs[slot], sem.at[slot]).wait()

    @pl.when(i == 0)
    def _():
        for t in range(PEER_LOOKAHEAD):
            issue(t, t)

    eye = (lax.broadcasted_iota(jnp.int32, (PEER_PICKS, PEER_PICKS), 0)
           == lax.broadcasted_iota(jnp.int32, (PEER_PICKS, PEER_PICKS), 1))

    def apply(t, slot):
        h_row = h_ref[pl.ds(t, 1), :]
        u = bufs[slot][:, 0:D_MODEL]
        a_col = jnp.sum(u * h_row, axis=1, keepdims=True)
        a_row = jnp.sum(jnp.where(eye, a_col, 0.0), axis=0, keepdims=True)
        w_row = gate_ref[pl.ds(t, 1), :] * _gelu_exact(a_row)
        w_col = jnp.sum(jnp.where(eye, w_row, 0.0), axis=1, keepdims=True)
        v = bufs[slot][:, D_MODEL:2 * D_MODEL]
        ffn[pl.ds(t, 1), :] = jnp.sum(w_col * v, axis=0, keepdims=True)

    def slot_round(r, carry):
        for slot in range(PEER_SLOTS):
            t = r * PEER_SLOTS + slot
            wait_slot(slot)
            issue(t + PEER_LOOKAHEAD, (slot + PEER_LOOKAHEAD) % PEER_SLOTS)
            apply(t, slot)
        return carry

    lax.fori_loop(0, tb // PEER_SLOTS, slot_round, 0)

    @pl.when(i == last)
    def _():
        for t in range(PEER_LOOKAHEAD):
            wait_slot(t)

    y_ref[...] = _layer_norm(DEEPNORM_ALPHA * h_ref[...] + ffn[...], g_ref[...], b_ref[...])


def _peer(h, idx, gate, uv, ln_g, ln_b):
    n = h.shape[0]
    tb = min(PEER_TB, n)
    assert n % tb == 0 and tb % PEER_SLOTS == 0
    nb = n // tb
    row = lambda i: (i, 0)
    idx_pad = jnp.concatenate([idx, jnp.zeros((tb, PEER_PICKS), idx.dtype)], axis=0)
    idx_next = idx_pad[tb:].reshape(nb, tb, PEER_PICKS)[:, :PEER_LOOKAHEAD]
    idx_ext = jnp.concatenate([idx.reshape(nb, tb, PEER_PICKS), idx_next], axis=1)
    return pl.pallas_call(
        _peer_kernel,
        grid=(nb,),
        in_specs=[
            pl.BlockSpec((1, tb + PEER_LOOKAHEAD, PEER_PICKS), lambda i: (i, 0, 0), memory_space=pltpu.SMEM),
            pl.BlockSpec((tb, PEER_PICKS), row),
            pl.BlockSpec((tb, D_MODEL), row),
            pl.BlockSpec(memory_space=pl.ANY),
            _const_spec(ln_g.shape), _const_spec(ln_b.shape),
        ],
        out_specs=pl.BlockSpec((tb, D_MODEL), row),
        out_shape=jax.ShapeDtypeStruct((n, D_MODEL), F32),
        scratch_shapes=(
            [pltpu.VMEM((PEER_PICKS, 2 * D_MODEL), F32) for _ in range(PEER_SLOTS)]
            + [pltpu.SemaphoreType.DMA((PEER_SLOTS,)), pltpu.VMEM((tb, D_MODEL), F32)]
        ),
        compiler_params=_compiler_params(("arbitrary",)),
    )(idx_ext, gate, h, uv, ln_g, ln_b)


def _rel_bias_pairs(rel_table, t, tk):
    dist = jnp.arange(t)[:, None] + (tk - t) - jnp.arange(tk)[None, :]
    idx = jnp.clip(dist, -A_REL_CLIP, A_REL_CLIP) + A_REL_CLIP
    return jnp.take(rel_table, idx, axis=1).astype(F32).reshape(A_HEADS // 2, 2 * t, tk)


def _permute_in_proj(w_in):
    offs = np.cumsum([0, A_WIDTH, A_WIDTH, A_WIDTH, B_WIDTH, B_KV_WIDTH, B_KV_WIDTH, D_MODEL, D_MODEL])
    qa, ka, va, qb, kb, vb, ga, gb = [w_in[:, offs[s]:offs[s + 1]] for s in range(8)]
    return jnp.concatenate([ga, gb, qa, qb, ka, va, kb, vb], axis=1).astype(BF16)


def _split_kv(kv, batch, t):
    ka = kv[:, 0:A_WIDTH].reshape(batch, t, A_HEADS, HEAD_DIM)
    va = kv[:, A_WIDTH:2 * A_WIDTH].reshape(batch, t, A_HEADS, HEAD_DIM)
    kb = kv[:, 2 * A_WIDTH:2 * A_WIDTH + B_KV_WIDTH].reshape(batch, t, B_KV_HEADS, HEAD_DIM)
    vb = kv[:, 2 * A_WIDTH + B_KV_WIDTH:].reshape(batch, t, B_KV_HEADS, HEAD_DIM)
    return ka, va, kb, vb


def kernel(x_prompt, x_sample, cache_a_k, cache_a_v, cache_b_k, cache_b_v, w_in, rel_bias_table, b_sinks,
           w_proj_a, w_proj_b, w_out, ln1_g, ln1_b, peer_w_query, peer_sub_keys_1, peer_sub_keys_2,
           peer_u, peer_v, ln2_g, ln2_b):
    batch, seq, _ = x_prompt.shape
    dbatch, dseq, _ = x_sample.shape
    assert seq % ATT_BLOCK == 0 and seq >= ATT_BLOCK

    w_in_p = _permute_in_proj(w_in)
    wpa, wpb, wout = w_proj_a.astype(BF16), w_proj_b.astype(BF16), w_out.astype(BF16)
    wq = peer_w_query.astype(BF16)
    sk1, sk2 = peer_sub_keys_1.astype(BF16), peer_sub_keys_2.astype(BF16)
    uv = jnp.concatenate([peer_u, peer_v], axis=1)
    ln1 = (ln1_g.reshape(1, D_MODEL), ln1_b.reshape(1, D_MODEL))
    ln2 = (ln2_g.reshape(1, D_MODEL), ln2_b.reshape(1, D_MODEL))
    sinks = b_sinks.astype(F32)

    def tail(xf, ya, yb, p, tm):
        h = _merge(ya, yb, p, xf, wpa, wpb, wout, *ln1, tm=tm)
        idx, gate = _route(h, wq, sk1, sk2)
        return _peer(h, idx, gate, uv, *ln2)

    ns = dbatch * dseq
    xs = x_sample.reshape(ns, D_MODEL)
    p_s, kv_s = _in_proj(xs, w_in_p, tm=ns)
    ka_s, va_s, kb_s, vb_s = _split_kv(kv_s, dbatch, dseq)

    def with_cache(cache, new_cols, width):
        new = p_s[:, new_cols:new_cols + width].reshape(dbatch, dseq, width)
        return jnp.concatenate([cache.reshape(dbatch, -1, width).astype(BF16), new], axis=1)

    kcat_a = with_cache(cache_a_k, COL_KA, A_WIDTH)
    vcat_a = with_cache(cache_a_v, COL_VA, A_WIDTH)
    kcat_b = with_cache(cache_b_k, COL_KB, B_KV_WIDTH)
    vcat_b = with_cache(cache_b_v, COL_VB, B_KV_WIDTH)
    bias_s = _rel_bias_pairs(rel_bias_table, dseq, kcat_a.shape[1])
    ya_s, yb_s = _attn_sample(p_s, kcat_a, vcat_a, kcat_b, vcat_b, bias_s, sinks, dbatch, dseq)
    y_sample = tail(xs, ya_s, yb_s, p_s, ns).reshape(dbatch, dseq, D_MODEL)

    xp = x_prompt.reshape(batch * seq, D_MODEL)
    p, kv = _in_proj(xp, w_in_p, tm=512)
    bias_p = _rel_bias_pairs(rel_bias_table, CHUNK, (A_PREV_CHUNKS + 1) * CHUNK)
    ya, yb = _attn_prompt(p, bias_p, sinks, batch, seq)
    y_prompt = tail(xp, ya, yb, p, 256).reshape(batch, seq, D_MODEL)
    ka, va, kb, vb = _split_kv(kv, batch, seq)
    la = min(A_PREV_CHUNKS * CHUNK, seq)
    lb = min(B_PREV_CHUNKS * CHUNK, seq)

    return (y_prompt, y_sample, ka[:, -la:], va[:, -la:], kb[:, -lb:], vb[:, -lb:], ka_s, va_s, kb_s, vb_s)
```

```python
import math

import jax
import jax.numpy as jnp
import numpy as np
from jax import lax
from jax.experimental import pallas as pl
from jax.experimental.pallas import tpu as pltpu

F32 = jnp.float32
BF16 = jnp.bfloat16

D_MODEL = 2048
CHUNK = 64
HEAD_DIM = 64
A_HEADS = 16
A_PREV_CHUNKS = 8
A_REL_CLIP = 128
B_Q_HEADS = 16
B_KV_HEADS = 2
B_PREV_CHUNKS = 2
A_WIDTH = A_HEADS * HEAD_DIM
B_WIDTH = B_Q_HEADS * HEAD_DIM
B_KV_WIDTH = B_KV_HEADS * HEAD_DIM
PEER_HEADS = 8
PEER_KEYS = 128
PEER_QDIM = 256
PEER_TOPK = 16
PEER_PICKS = PEER_HEADS * PEER_TOPK
LN_EPS = 1e-5
DEEPNORM_ALPHA = 2.0 ** 0.25

LANES = 128
SUBLANES = 8
HEAD_PAIR = 2 * HEAD_DIM
VMEM_LIMIT = 56 * 1024 * 1024

COL_GA = 0
COL_GB = COL_GA + D_MODEL
COL_QA = COL_GB + D_MODEL
COL_QB = COL_QA + A_WIDTH
COL_KA = COL_QB + B_WIDTH
COL_VA = COL_KA + A_WIDTH
COL_KB = COL_VA + A_WIDTH
COL_VB = COL_KB + B_KV_WIDTH
IN_COLS = COL_VB + B_KV_WIDTH
KV_COLS = IN_COLS - COL_KA
PROJ_TN = 768
ATT_BLOCK = A_PREV_CHUNKS * CHUNK

NT_DIMS = (((1,), (1,)), ((), ()))


def _compiler_params(semantics):
    return pltpu.CompilerParams(dimension_semantics=semantics, vmem_limit_bytes=VMEM_LIMIT)


def _const_spec(shape):
    zeros = (0,) * len(shape)
    return pl.BlockSpec(shape, lambda *_: zeros, pipeline_mode=pl.Buffered(1))


def _in_proj_kernel(x_ref, w_ref, p_ref, kv_ref, xb_ref):
    j = pl.program_id(1)

    @pl.when(j == 0)
    def _():
        xb_ref[...] = x_ref[...].astype(BF16)

    acc = jnp.dot(xb_ref[...], w_ref[...], preferred_element_type=F32)
    p_ref[...] = acc.astype(BF16)

    @pl.when(j >= COL_KA // PROJ_TN)
    def _():
        kv_ref[...] = acc


def _in_proj(x, w_bf16, tm):
    n = x.shape[0]
    first_kv = COL_KA // PROJ_TN
    return pl.pallas_call(
        _in_proj_kernel,
        grid=(n // tm, IN_COLS // PROJ_TN),
        in_specs=[
            pl.BlockSpec((tm, D_MODEL), lambda i, j: (i, 0)),
            pl.BlockSpec((D_MODEL, PROJ_TN), lambda i, j: (0, j)),
        ],
        out_specs=[
            pl.BlockSpec((tm, PROJ_TN), lambda i, j: (i, j)),
            pl.BlockSpec((tm, PROJ_TN), lambda i, j: (i, jnp.maximum(j - first_kv, 0))),
        ],
        out_shape=[
            jax.ShapeDtypeStruct((n, IN_COLS), BF16),
            jax.ShapeDtypeStruct((n, KV_COLS), F32),
        ],
        scratch_shapes=[pltpu.VMEM((tm, D_MODEL), BF16)],
        compiler_params=_compiler_params(("parallel", "arbitrary")),
    )(x, w_bf16)


def _softmax_pv(s, v2, sink):
    m = jnp.max(s, axis=-1, keepdims=True)
    if sink is not None:
        m = jnp.maximum(m, sink)
    p = jnp.exp(s - m)
    den = jnp.sum(p, axis=-1, keepdims=True)
    if sink is not None:
        den = den + jnp.exp(sink - m)
    o = jnp.dot(p.astype(BF16), v2, preferred_element_type=F32)
    return o / den


def _scores(qq, k2, bias, key_thresh):
    s = lax.dot_general(qq, k2, NT_DIMS, preferred_element_type=F32) * (HEAD_DIM ** -0.5) + bias
    if key_thresh is not None:
        col = lax.broadcasted_iota(jnp.int32, s.shape, 1)
        s = jnp.where(col >= key_thresh, s, -jnp.inf)
    return s


def _mixer_a(q_of, k_of, v_of, bias_of, key_thresh, store):
    for j in range(A_HEADS // 2):
        q2 = q_of(j)
        tq = q2.shape[0]
        lane_hi = lax.broadcasted_iota(jnp.int32, q2.shape, 1) >= HEAD_DIM
        zero = jnp.zeros_like(q2)
        qq = jnp.concatenate([jnp.where(lane_hi, zero, q2), jnp.where(lane_hi, q2, zero)], axis=0)
        o = _softmax_pv(_scores(qq, k_of(j), bias_of(j), key_thresh), v_of(j), None)
        store(j, jnp.where(lane_hi, o[tq:2 * tq], o[0:tq]))


def _alibi_rows(absdist):
    return jnp.concatenate([(-(2.0 ** (-8.0 * (h + 1) / B_Q_HEADS))) * absdist for h in range(B_Q_HEADS)], axis=0)


def _sink_rows(sink_ref, tq):
    return jnp.concatenate([jnp.full((tq, 1), sink_ref[h], F32) for h in range(B_Q_HEADS)], axis=0)


def _mixer_b(q_of, k2, v2, bias_rows, sink_rows, key_thresh, store):
    group = B_Q_HEADS // B_KV_HEADS
    pieces = []
    for j in range(B_Q_HEADS // 2):
        q2 = q_of(j)
        tq = q2.shape[0]
        lane_hi = lax.broadcasted_iota(jnp.int32, q2.shape, 1) >= HEAD_DIM
        zero = jnp.zeros_like(q2)
        q2_swapped = pltpu.roll(q2.astype(F32), HEAD_DIM, 1).astype(BF16)
        for hh in range(2):
            kv = (2 * j + hh) // group
            src = q2 if hh == kv else q2_swapped
            pieces.append(jnp.where(lane_hi, src, zero) if kv else jnp.where(lane_hi, zero, src))
    qq = jnp.concatenate(pieces, axis=0)
    o = _softmax_pv(_scores(qq, k2, bias_rows, key_thresh), v2, sink_rows)
    lane_hi = lax.broadcasted_iota(jnp.int32, (tq, HEAD_PAIR), 1) >= HEAD_DIM
    for j in range(B_Q_HEADS // 2):
        halves = []
        for hh in range(2):
            h = 2 * j + hh
            o_h = o[h * tq:(h + 1) * tq]
            halves.append(o_h if hh == h // group else pltpu.roll(o_h, HEAD_DIM, 1))
        store(j, jnp.where(lane_hi, halves[1], halves[0]))


def _attn_prompt_kernel(sink_ref, qa_ref, kap_ref, kac_ref, vap_ref, vac_ref,
                        qb_ref, kbp_ref, kbc_ref, vbp_ref, vbc_ref, bias_ref,
                        ya_ref, yb_ref, ka_s, va_s, kb_s, vb_s, bias_b_s, sink_s):
    i = pl.program_id(1)
    blk = ATT_BLOCK
    ka_s[0:blk, :] = kap_ref[...]
    ka_s[blk:2 * blk, :] = kac_ref[...]
    va_s[0:blk, :] = vap_ref[...]
    va_s[blk:2 * blk, :] = vac_ref[...]
    kb_s[0:blk, :] = kbp_ref[...]
    kb_s[blk:2 * blk, :] = kbc_ref[...]
    vb_s[0:blk, :] = vbp_ref[...]
    vb_s[blk:2 * blk, :] = vbc_ref[...]

    tk_a = (A_PREV_CHUNKS + 1) * CHUNK
    tk_b = (B_PREV_CHUNKS + 1) * CHUNK
    b_off = blk - B_PREV_CHUNKS * CHUNK
    tq = lax.broadcasted_iota(jnp.int32, (CHUNK, tk_b), 0)
    ts = lax.broadcasted_iota(jnp.int32, (CHUNK, tk_b), 1)
    bias_b_s[...] = _alibi_rows(jnp.abs(tq + B_PREV_CHUNKS * CHUNK - ts).astype(F32))
    sink_s[...] = _sink_rows(sink_ref, CHUNK)

    def chunk_body(c, carry):
        row0 = pl.multiple_of(c * CHUNK, CHUNK)
        thresh_a = jnp.where(i > 0, 0, blk - c * CHUNK)
        thresh_b = jnp.where(i > 0, 0, B_PREV_CHUNKS * CHUNK - c * CHUNK)

        def store_a(j, val):
            ya_ref[pl.ds(row0, CHUNK), j * HEAD_PAIR:(j + 1) * HEAD_PAIR] = val.astype(ya_ref.dtype)

        def store_b(j, val):
            yb_ref[pl.ds(row0, CHUNK), j * HEAD_PAIR:(j + 1) * HEAD_PAIR] = val.astype(yb_ref.dtype)

        _mixer_a(
            lambda j: qa_ref[pl.ds(row0, CHUNK), j * HEAD_PAIR:(j + 1) * HEAD_PAIR],
            lambda j: ka_s[pl.ds(row0, tk_a), j * HEAD_PAIR:(j + 1) * HEAD_PAIR],
            lambda j: va_s[pl.ds(row0, tk_a), j * HEAD_PAIR:(j + 1) * HEAD_PAIR],
            lambda j: bias_ref[j], thresh_a, store_a)

        rowb = pl.multiple_of(b_off + c * CHUNK, CHUNK)
        _mixer_b(
            lambda j: qb_ref[pl.ds(row0, CHUNK), j * HEAD_PAIR:(j + 1) * HEAD_PAIR],
            kb_s[pl.ds(rowb, tk_b), :], vb_s[pl.ds(rowb, tk_b), :],
            bias_b_s[...], sink_s[...], thresh_b, store_b)
        return carry

    lax.fori_loop(0, blk // CHUNK, chunk_body, 0)


def _attn_prompt(p, bias_a, sinks, batch, seq):
    blk = ATT_BLOCK
    nb = seq // blk
    n = batch * seq

    def cur(col_block):
        return lambda b, i: (b * nb + i, col_block)

    def prev(col_block):
        return lambda b, i: (b * nb + jnp.maximum(i - 1, 0), col_block)

    wa, wb = A_WIDTH, B_KV_WIDTH
    in_specs = [
        pl.BlockSpec(memory_space=pltpu.SMEM),
        pl.BlockSpec((blk, wa), cur(COL_QA // wa)),
        pl.BlockSpec((blk, wa), prev(COL_KA // wa)),
        pl.BlockSpec((blk, wa), cur(COL_KA // wa)),
        pl.BlockSpec((blk, wa), prev(COL_VA // wa)),
        pl.BlockSpec((blk, wa), cur(COL_VA // wa)),
        pl.BlockSpec((blk, wa), cur(COL_QB // wa)),
        pl.BlockSpec((blk, wb), prev(COL_KB // wb)),
        pl.BlockSpec((blk, wb), cur(COL_KB // wb)),
        pl.BlockSpec((blk, wb), prev(COL_VB // wb)),
        pl.BlockSpec((blk, wb), cur(COL_VB // wb)),
        _const_spec(bias_a.shape),
    ]
    out_spec = pl.BlockSpec((blk, wa), lambda b, i: (b * nb + i, 0))
    rows_b = B_Q_HEADS * CHUNK
    return pl.pallas_call(
        _attn_prompt_kernel,
        grid=(batch, nb),
        in_specs=in_specs,
        out_specs=[out_spec, out_spec],
        out_shape=[jax.ShapeDtypeStruct((n, A_WIDTH), BF16), jax.ShapeDtypeStruct((n, B_WIDTH), BF16)],
        scratch_shapes=[
            pltpu.VMEM((2 * blk, wa), BF16), pltpu.VMEM((2 * blk, wa), BF16),
            pltpu.VMEM((2 * blk, wb), BF16), pltpu.VMEM((2 * blk, wb), BF16),
            pltpu.VMEM((rows_b, (B_PREV_CHUNKS + 1) * CHUNK), F32), pltpu.VMEM((rows_b, 1), F32),
        ],
        compiler_params=_compiler_params(("parallel", "arbitrary")),
    )(sinks, p, p, p, p, p, p, p, p, p, p, bias_a)


def _attn_sample_kernel(sink_ref, qa_ref, ka_ref, va_ref, qb_ref, kb_ref, vb_ref, bias_ref, ya_ref, yb_ref):
    t, tk_b = qb_ref.shape[0], kb_ref.shape[1]

    def store_a(j, val):
        ya_ref[:, j * HEAD_PAIR:(j + 1) * HEAD_PAIR] = val.astype(ya_ref.dtype)

    def store_b(j, val):
        yb_ref[:, j * HEAD_PAIR:(j + 1) * HEAD_PAIR] = val.astype(yb_ref.dtype)

    _mixer_a(
        lambda j: qa_ref[:, j * HEAD_PAIR:(j + 1) * HEAD_PAIR],
        lambda j: ka_ref[0, :, j * HEAD_PAIR:(j + 1) * HEAD_PAIR],
        lambda j: va_ref[0, :, j * HEAD_PAIR:(j + 1) * HEAD_PAIR],
        lambda j: bias_ref[j], None, store_a)

    tq = lax.broadcasted_iota(jnp.int32, (t, tk_b), 0)
    ts = lax.broadcasted_iota(jnp.int32, (t, tk_b), 1)
    bias_rows = _alibi_rows(jnp.abs(tq + (tk_b - t) - ts).astype(F32))
    _mixer_b(
        lambda j: qb_ref[:, j * HEAD_PAIR:(j + 1) * HEAD_PAIR],
        kb_ref[0], vb_ref[0], bias_rows, _sink_rows(sink_ref, t), None, store_b)


def _attn_sample(p, kcat_a, vcat_a, kcat_b, vcat_b, bias_a, sinks, batch, t):
    n = batch * t
    wa, wb = A_WIDTH, B_KV_WIDTH
    tk_a, tk_b = kcat_a.shape[1], kcat_b.shape[1]
    in_specs = [
        pl.BlockSpec(memory_space=pltpu.SMEM),
        pl.BlockSpec((t, wa), lambda b: (b, COL_QA // wa)),
        pl.BlockSpec((1, tk_a, wa), lambda b: (b, 0, 0)),
        pl.BlockSpec((1, tk_a, wa), lambda b: (b, 0, 0)),
        pl.BlockSpec((t, wa), lambda b: (b, COL_QB // wa)),
        pl.BlockSpec((1, tk_b, wb), lambda b: (b, 0, 0)),
        pl.BlockSpec((1, tk_b, wb), lambda b: (b, 0, 0)),
        _const_spec(bias_a.shape),
    ]
    out_spec = pl.BlockSpec((t, wa), lambda b: (b, 0))
    return pl.pallas_call(
        _attn_sample_kernel,
        grid=(batch,),
        in_specs=in_specs,
        out_specs=[out_spec, out_spec],
        out_shape=[jax.ShapeDtypeStruct((n, A_WIDTH), BF16), jax.ShapeDtypeStruct((n, B_WIDTH), BF16)],
        compiler_params=_compiler_params(("parallel",)),
    )(sinks, p, kcat_a, vcat_a, p, kcat_b, vcat_b, bias_a)


def _layer_norm(x, g, b):
    mu = jnp.mean(x, axis=-1, keepdims=True)
    xc = x - mu
    var = jnp.mean(xc * xc, axis=-1, keepdims=True)
    return xc * lax.rsqrt(var + LN_EPS) * g + b


def _merge_kernel(ya_ref, yb_ref, ga_ref, gb_ref, x_ref, wpa_ref, wpb_ref, wout_ref, g_ref, b_ref, h_ref):
    za = jnp.dot(ya_ref[...], wpa_ref[...], preferred_element_type=F32)
    zb = jnp.dot(yb_ref[...], wpb_ref[...], preferred_element_type=F32)
    m = jax.nn.sigmoid(ga_ref[...].astype(F32)) * za + jax.nn.sigmoid(gb_ref[...].astype(F32)) * zb
    r = jnp.dot(m.astype(BF16), wout_ref[...], preferred_element_type=F32)
    h_ref[...] = _layer_norm(DEEPNORM_ALPHA * x_ref[...] + r, g_ref[...], b_ref[...])


def _merge(ya, yb, p, x, wpa, wpb, wout, ln_g, ln_b, tm):
    n = x.shape[0]
    row = lambda i: (i, 0)
    return pl.pallas_call(
        _merge_kernel,
        grid=(n // tm,),
        in_specs=[
            pl.BlockSpec((tm, A_WIDTH), row),
            pl.BlockSpec((tm, B_WIDTH), row),
            pl.BlockSpec((tm, D_MODEL), lambda i: (i, COL_GA // D_MODEL)),
            pl.BlockSpec((tm, D_MODEL), lambda i: (i, COL_GB // D_MODEL)),
            pl.BlockSpec((tm, D_MODEL), row),
            _const_spec(wpa.shape), _const_spec(wpb.shape), _const_spec(wout.shape),
            _const_spec(ln_g.shape), _const_spec(ln_b.shape),
        ],
        out_specs=pl.BlockSpec((tm, D_MODEL), row),
        out_shape=jax.ShapeDtypeStruct((n, D_MODEL), F32),
        compiler_params=_compiler_params(("parallel",)),
    )(ya, yb, p, p, x, wpa, wpb, wout, ln_g, ln_b)


ROUTE_TM = 128
ID_SENTINEL = 1 << 30


def _topk_rows(s, ids, k):
    vals, picked = [], []
    for _ in range(k):
        m = jnp.max(s, axis=0, keepdims=True)
        ix = jnp.min(jnp.where(s == m, ids, ID_SENTINEL), axis=0, keepdims=True)
        s = jnp.where(ids == ix, -jnp.inf, s)
        vals.append(m)
        picked.append(ix)
    return jnp.concatenate(vals, axis=0), jnp.concatenate(picked, axis=0)


def _candidate_blocks(v1, v2):
    t = v1.shape[1]
    sub = lax.broadcasted_iota(jnp.int32, (SUBLANES, t), 0)
    neg = jnp.full((SUBLANES, t), -jnp.inf, F32)
    k = PEER_TOPK
    blocks = []
    blocks.append((v1[0:1] + v2[0:8], sub))
    blocks.append((v1[0:1] + v2[8:16], sub + 8))
    blocks.append((v1[1:2] + v2[0:8], sub + k))
    blocks.append((jnp.where(sub < 5, v1[2:3] + v2[0:8], neg), sub + 2 * k))
    blocks.append((jnp.where(sub < 4, v1[3:4] + v2[0:8], neg), sub + 3 * k))
    blocks.append((v1[8:16] + v2[0:1], (sub + 8) * k))
    a47 = jnp.concatenate([v1[4:8], v1[4:8]], axis=0)
    blocks.append((a47 + jnp.where(sub < 4, v2[0:1], v2[1:2]),
                   (4 + lax.bitwise_and(sub, 3)) * k + lax.shift_right_logical(sub, 2)))
    blocks.append((jnp.where(sub == 0, v1[4:5] + v2[2:3], neg), sub + 4 * k + 2))
    vals = jnp.concatenate([b[0] for b in blocks], axis=0)
    ids = jnp.concatenate([b[1] for b in blocks], axis=0)
    return vals, ids


def _route_kernel(h_ref, wq_ref, sk1_ref, sk2_ref, idx_ref, gate_ref):
    q = jnp.dot(h_ref[...].astype(BF16), wq_ref[...], preferred_element_type=F32).astype(BF16)
    half = PEER_QDIM // 2
    key_ids = lax.broadcasted_iota(jnp.int32, (PEER_KEYS, h_ref.shape[0]), 0)
    experts, gates = [], []
    for hd in range(PEER_HEADS):
        q1 = q[:, hd * PEER_QDIM:hd * PEER_QDIM + half]
        q2 = q[:, hd * PEER_QDIM + half:(hd + 1) * PEER_QDIM]
        s1 = lax.dot_general(sk1_ref[...], q1, NT_DIMS, preferred_element_type=F32)
        s2 = lax.dot_general(sk2_ref[...], q2, NT_DIMS, preferred_element_type=F32)
        v1, i1 = _topk_rows(s1, key_ids, PEER_TOPK)
        v2, i2 = _topk_rows(s2, key_ids, PEER_TOPK)
        sc, ci = _topk_rows(*_candidate_blocks(v1, v2), PEER_TOPK)
        ca = lax.shift_right_logical(ci, int(math.log2(PEER_TOPK)))
        cb = lax.bitwise_and(ci, PEER_TOPK - 1)
        e1 = jnp.zeros_like(ci)
        e2 = jnp.zeros_like(ci)
        for a in range(PEER_TOPK):
            e1 = jnp.where(ca == a, i1[a:a + 1], e1)
            e2 = jnp.where(cb == a, i2[a:a + 1], e2)
        experts.append(e1 * PEER_KEYS + e2)
        e = jnp.exp(sc - sc[0:1])
        gates.append(e / jnp.sum(e, axis=0, keepdims=True))
    idx_ref[...] = jnp.concatenate(experts, axis=0).astype(F32).T.astype(jnp.int32)
    gate_ref[...] = jnp.concatenate(gates, axis=0).T


def _route(h, wq, sk1, sk2):
    n = h.shape[0]
    tm = ROUTE_TM
    row = lambda i: (i, 0)
    return pl.pallas_call(
        _route_kernel,
        grid=(n // tm,),
        in_specs=[pl.BlockSpec((tm, D_MODEL), row), _const_spec(wq.shape),
                  _const_spec(sk1.shape), _const_spec(sk2.shape)],
        out_specs=[pl.BlockSpec((tm, PEER_PICKS), row), pl.BlockSpec((tm, PEER_PICKS), row)],
        out_shape=[jax.ShapeDtypeStruct((n, PEER_PICKS), jnp.int32),
                   jax.ShapeDtypeStruct((n, PEER_PICKS), F32)],
        compiler_params=_compiler_params(("parallel",)),
    )(h, wq, sk1, sk2)


PEER_TB = 32
PEER_SLOTS = 8
PEER_LOOKAHEAD = PEER_SLOTS - 1
ROW_TILES = D_MODEL // LANES

def _gelu_exact(x):
    return 0.5 * x * (1.0 + lax.erf(x * (2.0 ** -0.5)))


def _peer_kernel(idx_ref, gate_ref, h_ref, uv_ref, g_ref, b_ref, y_ref, *scratch):
    bufs, (sem, ffn) = scratch[:PEER_SLOTS], scratch[PEER_SLOTS:]
    i = pl.program_id(0)
    last = pl.num_programs(0) - 1
    tb = h_ref.shape[0]

    def issue(t, slot):
        for k in range(PEER_PICKS):
            dst = bufs[slot].at[:, k // SUBLANES, pl.ds(k % SUBLANES, 1), :]
            pltpu.make_async_copy(uv_ref.at[idx_ref[0, t, k]], dst, sem.at[slot]).start()

    def wait_slot(slot):
        pltpu.make_async_copy(bufs[slot], bufs[slot], sem.at[slot]).wait()

    def picks_tile(slot, c):
        return bufs[slot][c].reshape(PEER_PICKS, LANES)

    @pl.when(i == 0)
    def _():
        for t in range(PEER_LOOKAHEAD):
            issue(t, t)

    eye = (lax.broadcasted_iota(jnp.int32, (PEER_PICKS, PEER_PICKS), 0)
           == lax.broadcasted_iota(jnp.int32, (PEER_PICKS, PEER_PICKS), 1))

    def apply(t, slot):
        h_row = h_ref[pl.ds(t, 1), :]
        acc = None
        for c in range(ROW_TILES):
            u = lax.bitcast_convert_type(lax.bitwise_and(picks_tile(slot, c), jnp.int32(-(1 << 16))), F32)
            term = u * h_row[:, c * LANES:(c + 1) * LANES]
            acc = term if acc is None else acc + term
        a_col = jnp.sum(acc, axis=1, keepdims=True)
        a_row = jnp.sum(jnp.where(eye, a_col, 0.0), axis=0, keepdims=True)
        w_row = gate_ref[pl.ds(t, 1), :] * _gelu_exact(a_row)
        w_col = jnp.sum(jnp.where(eye, w_row, 0.0), axis=1, keepdims=True)
        out = []
        for c in range(ROW_TILES):
            v = lax.bitcast_convert_type(lax.shift_left(picks_tile(slot, c), jnp.int32(16)), F32)
            out.append(jnp.sum(w_col * v, axis=0, keepdims=True))
        ffn[pl.ds(t, 1), :] = jnp.concatenate(out, axis=1)

    def slot_round(r, carry):
        for slot in range(PEER_SLOTS):
            t = r * PEER_SLOTS + slot
            wait_slot(slot)
            issue(t + PEER_LOOKAHEAD, (slot + PEER_LOOKAHEAD) % PEER_SLOTS)
            apply(t, slot)
        return carry

    lax.fori_loop(0, tb // PEER_SLOTS, slot_round, 0)

    @pl.when(i == last)
    def _():
        for t in range(PEER_LOOKAHEAD):
            wait_slot(t)

    y_ref[...] = _layer_norm(DEEPNORM_ALPHA * h_ref[...] + ffn[...], g_ref[...], b_ref[...])


def _peer(h, idx, gate, uv, ln_g, ln_b):
    n = h.shape[0]
    tb = min(PEER_TB, n)
    assert n % tb == 0 and tb % PEER_SLOTS == 0
    nb = n // tb
    row = lambda i: (i, 0)
    idx_pad = jnp.concatenate([idx, jnp.zeros((tb, PEER_PICKS), idx.dtype)], axis=0)
    idx_next = idx_pad[tb:].reshape(nb, tb, PEER_PICKS)[:, :PEER_LOOKAHEAD]
    idx_ext = jnp.concatenate([idx.reshape(nb, tb, PEER_PICKS), idx_next], axis=1)
    return pl.pallas_call(
        _peer_kernel,
        grid=(nb,),
        in_specs=[
            pl.BlockSpec((1, tb + PEER_LOOKAHEAD, PEER_PICKS), lambda i: (i, 0, 0), memory_space=pltpu.SMEM),
            pl.BlockSpec((tb, PEER_PICKS), row),
            pl.BlockSpec((tb, D_MODEL), row),
            pl.BlockSpec(memory_space=pl.ANY),
            _const_spec(ln_g.shape), _const_spec(ln_b.shape),
        ],
        out_specs=pl.BlockSpec((tb, D_MODEL), row),
        out_shape=jax.ShapeDtypeStruct((n, D_MODEL), F32),
        scratch_shapes=(
            [pltpu.VMEM((ROW_TILES, PEER_PICKS // SUBLANES, SUBLANES, LANES), jnp.int32) for _ in range(PEER_SLOTS)]
            + [pltpu.SemaphoreType.DMA((PEER_SLOTS,)), pltpu.VMEM((tb, D_MODEL), F32)]
        ),
        compiler_params=_compiler_params(("arbitrary",)),
    )(idx_ext, gate, h, uv, ln_g, ln_b)


def _rel_bias_pairs(rel_table, t, tk):
    dist = (tk - 1) - jnp.arange(tk + t - 1)
    line = jnp.take(rel_table, jnp.clip(dist, -A_REL_CLIP, A_REL_CLIP) + A_REL_CLIP, axis=1).astype(F32)
    bias = jnp.stack([line[:, t - 1 - q:t - 1 - q + tk] for q in range(t)], axis=1)
    return bias.reshape(A_HEADS // 2, 2 * t, tk)


def _pack_expert_tables(u, v):
    ub = lax.bitcast_convert_type(u.astype(BF16), jnp.uint16).astype(jnp.uint32)
    vb = lax.bitcast_convert_type(v.astype(BF16), jnp.uint16).astype(jnp.uint32)
    words = lax.bitcast_convert_type(lax.shift_left(ub, jnp.uint32(16)) | vb, jnp.int32)
    return words.reshape(u.shape[0], ROW_TILES, 1, LANES)


def _permute_in_proj(w_in):
    offs = np.cumsum([0, A_WIDTH, A_WIDTH, A_WIDTH, B_WIDTH, B_KV_WIDTH, B_KV_WIDTH, D_MODEL, D_MODEL])
    qa, ka, va, qb, kb, vb, ga, gb = [w_in[:, offs[s]:offs[s + 1]] for s in range(8)]
    return jnp.concatenate([ga, gb, qa, qb, ka, va, kb, vb], axis=1).astype(BF16)


def _split_kv(kv, batch, t):
    ka = kv[:, 0:A_WIDTH].reshape(batch, t, A_HEADS, HEAD_DIM)
    va = kv[:, A_WIDTH:2 * A_WIDTH].reshape(batch, t, A_HEADS, HEAD_DIM)
    kb = kv[:, 2 * A_WIDTH:2 * A_WIDTH + B_KV_WIDTH].reshape(batch, t, B_KV_HEADS, HEAD_DIM)
    vb = kv[:, 2 * A_WIDTH + B_KV_WIDTH:].reshape(batch, t, B_KV_HEADS, HEAD_DIM)
    return ka, va, kb, vb


def kernel(x_prompt, x_sample, cache_a_k, cache_a_v, cache_b_k, cache_b_v, w_in, rel_bias_table, b_sinks,
           w_proj_a, w_proj_b, w_out, ln1_g, ln1_b, peer_w_query, peer_sub_keys_1, peer_sub_keys_2,
           peer_u, peer_v, ln2_g, ln2_b):
    batch, seq, _ = x_prompt.shape
    dbatch, dseq, _ = x_sample.shape
    assert seq % ATT_BLOCK == 0 and seq >= ATT_BLOCK

    w_in_p = _permute_in_proj(w_in)
    wpa, wpb, wout = w_proj_a.astype(BF16), w_proj_b.astype(BF16), w_out.astype(BF16)
    wq = peer_w_query.astype(BF16)
    sk1, sk2 = peer_sub_keys_1.astype(BF16), peer_sub_keys_2.astype(BF16)
    uv = _pack_expert_tables(peer_u, peer_v)
    ln1 = (ln1_g.reshape(1, D_MODEL), ln1_b.reshape(1, D_MODEL))
    ln2 = (ln2_g.reshape(1, D_MODEL), ln2_b.reshape(1, D_MODEL))
    sinks = b_sinks.astype(F32)

    def tail(xf, ya, yb, p, tm):
        h = _merge(ya, yb, p, xf, wpa, wpb, wout, *ln1, tm=tm)
        idx, gate = _route(h, wq, sk1, sk2)
        return _peer(h, idx, gate, uv, *ln2)

    ns = dbatch * dseq
    xs = x_sample.reshape(ns, D_MODEL)
    p_s, kv_s = _in_proj(xs, w_in_p, tm=ns)
    ka_s, va_s, kb_s, vb_s = _split_kv(kv_s, dbatch, dseq)

    def with_cache(cache, new_cols, width):
        new = p_s[:, new_cols:new_cols + width].reshape(dbatch, dseq, width)
        return jnp.concatenate([cache.reshape(dbatch, -1, width).astype(BF16), new], axis=1)

    kcat_a = with_cache(cache_a_k, COL_KA, A_WIDTH)
    vcat_a = with_cache(cache_a_v, COL_VA, A_WIDTH)
    kcat_b = with_cache(cache_b_k, COL_KB, B_KV_WIDTH)
    vcat_b = with_cache(cache_b_v, COL_VB, B_KV_WIDTH)
    bias_s = _rel_bias_pairs(rel_bias_table, dseq, kcat_a.shape[1])
    ya_s, yb_s = _attn_sample(p_s, kcat_a, vcat_a, kcat_b, vcat_b, bias_s, sinks, dbatch, dseq)
    y_sample = tail(xs, ya_s, yb_s, p_s, ns).reshape(dbatch, dseq, D_MODEL)

    xp = x_prompt.reshape(batch * seq, D_MODEL)
    p, kv = _in_proj(xp, w_in_p, tm=512)
    bias_p = _rel_bias_pairs(rel_bias_table, CHUNK, (A_PREV_CHUNKS + 1) * CHUNK)
    ya, yb = _attn_prompt(p, bias_p, sinks, batch, seq)
    y_prompt = tail(xp, ya, yb, p, 256).reshape(batch, seq, D_MODEL)
    ka, va, kb, vb = _split_kv(kv, batch, seq)
    la = min(A_PREV_CHUNKS * CHUNK, seq)
    lb = min(B_PREV_CHUNKS * CHUNK, seq)

    return (y_prompt, y_sample, ka[:, -la:], va[:, -la:], kb[:, -lb:], vb[:, -lb:], ka_s, va_s, kb_s, vb_s)
```

```python
import math

import jax
import jax.numpy as jnp
import numpy as np
from jax import lax
from jax.experimental import pallas as pl
from jax.experimental.pallas import tpu as pltpu

F32 = jnp.float32
BF16 = jnp.bfloat16

D_MODEL = 2048
CHUNK = 64
HEAD_DIM = 64
A_HEADS = 16
A_PREV_CHUNKS = 8
A_REL_CLIP = 128
B_Q_HEADS = 16
B_KV_HEADS = 2
B_PREV_CHUNKS = 2
A_WIDTH = A_HEADS * HEAD_DIM
B_WIDTH = B_Q_HEADS * HEAD_DIM
B_KV_WIDTH = B_KV_HEADS * HEAD_DIM
PEER_HEADS = 8
PEER_KEYS = 128
PEER_QDIM = 256
PEER_TOPK = 16
PEER_PICKS = PEER_HEADS * PEER_TOPK
LN_EPS = 1e-5
DEEPNORM_ALPHA = 2.0 ** 0.25

LANES = 128
SUBLANES = 8
HEAD_PAIR = 2 * HEAD_DIM
VMEM_LIMIT = 56 * 1024 * 1024

COL_GA = 0
COL_GB = COL_GA + D_MODEL
COL_QA = COL_GB + D_MODEL
COL_QB = COL_QA + A_WIDTH
COL_KA = COL_QB + B_WIDTH
COL_VA = COL_KA + A_WIDTH
COL_KB = COL_VA + A_WIDTH
COL_VB = COL_KB + B_KV_WIDTH
IN_COLS = COL_VB + B_KV_WIDTH
KV_COLS = IN_COLS - COL_KA
PROJ_TN = 768
ATT_BLOCK = A_PREV_CHUNKS * CHUNK

NT_DIMS = (((1,), (1,)), ((), ()))


def _compiler_params(semantics):
    return pltpu.CompilerParams(dimension_semantics=semantics, vmem_limit_bytes=VMEM_LIMIT)


def _const_spec(shape):
    zeros = (0,) * len(shape)
    return pl.BlockSpec(shape, lambda *_: zeros, pipeline_mode=pl.Buffered(1))


def _in_proj_kernel(x_ref, w_ref, p_ref, kv_ref, xb_ref):
    j = pl.program_id(1)

    @pl.when(j == 0)
    def _():
        xb_ref[...] = x_ref[...].astype(BF16)

    acc = jnp.dot(xb_ref[...], w_ref[...], preferred_element_type=F32)
    p_ref[...] = acc.astype(BF16)

    @pl.when(j >= COL_KA // PROJ_TN)
    def _():
        kv_ref[...] = acc


def _in_proj(x, w_bf16, tm):
    n = x.shape[0]
    first_kv = COL_KA // PROJ_TN
    return pl.pallas_call(
        _in_proj_kernel,
        grid=(n // tm, IN_COLS // PROJ_TN),
        in_specs=[
            pl.BlockSpec((tm, D_MODEL), lambda i, j: (i, 0)),
            pl.BlockSpec((D_MODEL, PROJ_TN), lambda i, j: (0, j)),
        ],
        out_specs=[
            pl.BlockSpec((tm, PROJ_TN), lambda i, j: (i, j)),
            pl.BlockSpec((tm, PROJ_TN), lambda i, j: (i, jnp.maximum(j - first_kv, 0))),
        ],
        out_shape=[
            jax.ShapeDtypeStruct((n, IN_COLS), BF16),
            jax.ShapeDtypeStruct((n, KV_COLS), F32),
        ],
        scratch_shapes=[pltpu.VMEM((tm, D_MODEL), BF16)],
        compiler_params=_compiler_params(("parallel", "arbitrary")),
    )(x, w_bf16)


def _softmax_pv(s, v2, sink):
    m = jnp.max(s, axis=-1, keepdims=True)
    if sink is not None:
        m = jnp.maximum(m, sink)
    p = jnp.exp(s - m)
    den = jnp.sum(p, axis=-1, keepdims=True)
    if sink is not None:
        den = den + jnp.exp(sink - m)
    o = jnp.dot(p.astype(BF16), v2, preferred_element_type=F32)
    return o / den


def _scores(qq, k2, bias, key_thresh):
    s = lax.dot_general(qq, k2, NT_DIMS, preferred_element_type=F32) * (HEAD_DIM ** -0.5) + bias
    if key_thresh is not None:
        col = lax.broadcasted_iota(jnp.int32, s.shape, 1)
        s = jnp.where(col >= key_thresh, s, -jnp.inf)
    return s


def _mixer_a(q_of, k_of, v_of, bias_of, key_thresh, store):
    for j in range(A_HEADS // 2):
        q2 = q_of(j)
        tq = q2.shape[0]
        lane_hi = lax.broadcasted_iota(jnp.int32, q2.shape, 1) >= HEAD_DIM
        zero = jnp.zeros_like(q2)
        qq = jnp.concatenate([jnp.where(lane_hi, zero, q2), jnp.where(lane_hi, q2, zero)], axis=0)
        o = _softmax_pv(_scores(qq, k_of(j), bias_of(j), key_thresh), v_of(j), None)
        store(j, jnp.where(lane_hi, o[tq:2 * tq], o[0:tq]))


def _alibi_rows(absdist):
    return jnp.concatenate([(-(2.0 ** (-8.0 * (h + 1) / B_Q_HEADS))) * absdist for h in range(B_Q_HEADS)], axis=0)


def _sink_rows(sink_ref, tq):
    return jnp.concatenate([jnp.full((tq, 1), sink_ref[h], F32) for h in range(B_Q_HEADS)], axis=0)


def _mixer_b(q_of, k2, v2, bias_rows, sink_rows, key_thresh, store):
    group = B_Q_HEADS // B_KV_HEADS
    pieces = []
    for j in range(B_Q_HEADS // 2):
        q2 = q_of(j)
        tq = q2.shape[0]
        lane_hi = lax.broadcasted_iota(jnp.int32, q2.shape, 1) >= HEAD_DIM
        zero = jnp.zeros_like(q2)
        q2_swapped = pltpu.roll(q2.astype(F32), HEAD_DIM, 1).astype(BF16)
        for hh in range(2):
            kv = (2 * j + hh) // group
            src = q2 if hh == kv else q2_swapped
            pieces.append(jnp.where(lane_hi, src, zero) if kv else jnp.where(lane_hi, zero, src))
    qq = jnp.concatenate(pieces, axis=0)
    o = _softmax_pv(_scores(qq, k2, bias_rows, key_thresh), v2, sink_rows)
    lane_hi = lax.broadcasted_iota(jnp.int32, (tq, HEAD_PAIR), 1) >= HEAD_DIM
    for j in range(B_Q_HEADS // 2):
        halves = []
        for hh in range(2):
            h = 2 * j + hh
            o_h = o[h * tq:(h + 1) * tq]
            halves.append(o_h if hh == h // group else pltpu.roll(o_h, HEAD_DIM, 1))
        store(j, jnp.where(lane_hi, halves[1], halves[0]))


def _attn_prompt_kernel(sink_ref, qa_ref, kap_ref, kac_ref, vap_ref, vac_ref,
                        qb_ref, kbp_ref, kbc_ref, vbp_ref, vbc_ref, bias_ref,
                        ya_ref, yb_ref, ka_s, va_s, kb_s, vb_s, bias_b_s, sink_s):
    i = pl.program_id(1)
    blk = ATT_BLOCK
    ka_s[0:blk, :] = kap_ref[...]
    ka_s[blk:2 * blk, :] = kac_ref[...]
    va_s[0:blk, :] = vap_ref[...]
    va_s[blk:2 * blk, :] = vac_ref[...]
    kb_s[0:blk, :] = kbp_ref[...]
    kb_s[blk:2 * blk, :] = kbc_ref[...]
    vb_s[0:blk, :] = vbp_ref[...]
    vb_s[blk:2 * blk, :] = vbc_ref[...]

    tk_a = (A_PREV_CHUNKS + 1) * CHUNK
    tk_b = (B_PREV_CHUNKS + 1) * CHUNK
    b_off = blk - B_PREV_CHUNKS * CHUNK
    tq = lax.broadcasted_iota(jnp.int32, (CHUNK, tk_b), 0)
    ts = lax.broadcasted_iota(jnp.int32, (CHUNK, tk_b), 1)
    bias_b_s[...] = _alibi_rows(jnp.abs(tq + B_PREV_CHUNKS * CHUNK - ts).astype(F32))
    sink_s[...] = _sink_rows(sink_ref, CHUNK)

    def chunk_body(c, carry):
        row0 = pl.multiple_of(c * CHUNK, CHUNK)
        thresh_a = jnp.where(i > 0, 0, blk - c * CHUNK)
        thresh_b = jnp.where(i > 0, 0, B_PREV_CHUNKS * CHUNK - c * CHUNK)

        def store_a(j, val):
            ya_ref[pl.ds(row0, CHUNK), j * HEAD_PAIR:(j + 1) * HEAD_PAIR] = val.astype(ya_ref.dtype)

        def store_b(j, val):
            yb_ref[pl.ds(row0, CHUNK), j * HEAD_PAIR:(j + 1) * HEAD_PAIR] = val.astype(yb_ref.dtype)

        _mixer_a(
            lambda j: qa_ref[pl.ds(row0, CHUNK), j * HEAD_PAIR:(j + 1) * HEAD_PAIR],
            lambda j: ka_s[pl.ds(row0, tk_a), j * HEAD_PAIR:(j + 1) * HEAD_PAIR],
            lambda j: va_s[pl.ds(row0, tk_a), j * HEAD_PAIR:(j + 1) * HEAD_PAIR],
            lambda j: bias_ref[j], thresh_a, store_a)

        rowb = pl.multiple_of(b_off + c * CHUNK, CHUNK)
        _mixer_b(
            lambda j: qb_ref[pl.ds(row0, CHUNK), j * HEAD_PAIR:(j + 1) * HEAD_PAIR],
            kb_s[pl.ds(rowb, tk_b), :], vb_s[pl.ds(rowb, tk_b), :],
            bias_b_s[...], sink_s[...], thresh_b, store_b)
        return carry

    lax.fori_loop(0, blk // CHUNK, chunk_body, 0)


def _attn_prompt(p, bias_a, sinks, batch, seq):
    blk = ATT_BLOCK
    nb = seq // blk
    n = batch * seq

    def cur(col_block):
        return lambda b, i: (b * nb + i, col_block)

    def prev(col_block):
        return lambda b, i: (b * nb + jnp.maximum(i - 1, 0), col_block)

    wa, wb = A_WIDTH, B_KV_WIDTH
    in_specs = [
        pl.BlockSpec(memory_space=pltpu.SMEM),
        pl.BlockSpec((blk, wa), cur(COL_QA // wa)),
        pl.BlockSpec((blk, wa), prev(COL_KA // wa)),
        pl.BlockSpec((blk, wa), cur(COL_KA // wa)),
        pl.BlockSpec((blk, wa), prev(COL_VA // wa)),
        pl.BlockSpec((blk, wa), cur(COL_VA // wa)),
        pl.BlockSpec((blk, wa), cur(COL_QB // wa)),
        pl.BlockSpec((blk, wb), prev(COL_KB // wb)),
        pl.BlockSpec((blk, wb), cur(COL_KB // wb)),
        pl.BlockSpec((blk, wb), prev(COL_VB // wb)),
        pl.BlockSpec((blk, wb), cur(COL_VB // wb)),
        _const_spec(bias_a.shape),
    ]
    out_spec = pl.BlockSpec((blk, wa), lambda b, i: (b * nb + i, 0))
    rows_b = B_Q_HEADS * CHUNK
    return pl.pallas_call(
        _attn_prompt_kernel,
        grid=(batch, nb),
        in_specs=in_specs,
        out_specs=[out_spec, out_spec],
        out_shape=[jax.ShapeDtypeStruct((n, A_WIDTH), BF16), jax.ShapeDtypeStruct((n, B_WIDTH), BF16)],
        scratch_shapes=[
            pltpu.VMEM((2 * blk, wa), BF16), pltpu.VMEM((2 * blk, wa), BF16),
            pltpu.VMEM((2 * blk, wb), BF16), pltpu.VMEM((2 * blk, wb), BF16),
            pltpu.VMEM((rows_b, (B_PREV_CHUNKS + 1) * CHUNK), F32), pltpu.VMEM((rows_b, 1), F32),
        ],
        compiler_params=_compiler_params(("parallel", "arbitrary")),
    )(sinks, p, p, p, p, p, p, p, p, p, p, bias_a)


def _attn_sample_kernel(sink_ref, qa_ref, ka_ref, va_ref, qb_ref, kb_ref, vb_ref, bias_ref, ya_ref, yb_ref):
    t, tk_b = qb_ref.shape[0], kb_ref.shape[1]

    def store_a(j, val):
        ya_ref[:, j * HEAD_PAIR:(j + 1) * HEAD_PAIR] = val.astype(ya_ref.dtype)

    def store_b(j, val):
        yb_ref[:, j * HEAD_PAIR:(j + 1) * HEAD_PAIR] = val.astype(yb_ref.dtype)

    _mixer_a(
        lambda j: qa_ref[:, j * HEAD_PAIR:(j + 1) * HEAD_PAIR],
        lambda j: ka_ref[0, :, j * HEAD_PAIR:(j + 1) * HEAD_PAIR],
        lambda j: va_ref[0, :, j * HEAD_PAIR:(j + 1) * HEAD_PAIR],
        lambda j: bias_ref[j], None, store_a)

    tq = lax.broadcasted_iota(jnp.int32, (t, tk_b), 0)
    ts = lax.broadcasted_iota(jnp.int32, (t, tk_b), 1)
    bias_rows = _alibi_rows(jnp.abs(tq + (tk_b - t) - ts).astype(F32))
    _mixer_b(
        lambda j: qb_ref[:, j * HEAD_PAIR:(j + 1) * HEAD_PAIR],
        kb_ref[0], vb_ref[0], bias_rows, _sink_rows(sink_ref, t), None, store_b)


def _attn_sample(p, kcat_a, vcat_a, kcat_b, vcat_b, bias_a, sinks, batch, t):
    n = batch * t
    wa, wb = A_WIDTH, B_KV_WIDTH
    tk_a, tk_b = kcat_a.shape[1], kcat_b.shape[1]
    in_specs = [
        pl.BlockSpec(memory_space=pltpu.SMEM),
        pl.BlockSpec((t, wa), lambda b: (b, COL_QA // wa)),
        pl.BlockSpec((1, tk_a, wa), lambda b: (b, 0, 0)),
        pl.BlockSpec((1, tk_a, wa), lambda b: (b, 0, 0)),
        pl.BlockSpec((t, wa), lambda b: (b, COL_QB // wa)),
        pl.BlockSpec((1, tk_b, wb), lambda b: (b, 0, 0)),
        pl.BlockSpec((1, tk_b, wb), lambda b: (b, 0, 0)),
        _const_spec(bias_a.shape),
    ]
    out_spec = pl.BlockSpec((t, wa), lambda b: (b, 0))
    return pl.pallas_call(
        _attn_sample_kernel,
        grid=(batch,),
        in_specs=in_specs,
        out_specs=[out_spec, out_spec],
        out_shape=[jax.ShapeDtypeStruct((n, A_WIDTH), BF16), jax.ShapeDtypeStruct((n, B_WIDTH), BF16)],
        compiler_params=_compiler_params(("parallel",)),
    )(sinks, p, kcat_a, vcat_a, p, kcat_b, vcat_b, bias_a)


def _layer_norm(x, g, b):
    mu = jnp.mean(x, axis=-1, keepdims=True)
    xc = x - mu
    var = jnp.mean(xc * xc, axis=-1, keepdims=True)
    return xc * lax.rsqrt(var + LN_EPS) * g + b


def _merge_kernel(ya_ref, yb_ref, ga_ref, gb_ref, x_ref, wpa_ref, wpb_ref, wout_ref, g_ref, b_ref, h_ref):
    za = jnp.dot(ya_ref[...], wpa_ref[...], preferred_element_type=F32)
    zb = jnp.dot(yb_ref[...], wpb_ref[...], preferred_element_type=F32)
    m = jax.nn.sigmoid(ga_ref[...].astype(F32)) * za + jax.nn.sigmoid(gb_ref[...].astype(F32)) * zb
    r = jnp.dot(m.astype(BF16), wout_ref[...], preferred_element_type=F32)
    h_ref[...] = _layer_norm(DEEPNORM_ALPHA * x_ref[...] + r, g_ref[...], b_ref[...])


def _merge(ya, yb, p, x, wpa, wpb, wout, ln_g, ln_b, tm):
    n = x.shape[0]
    row = lambda i: (i, 0)
    return pl.pallas_call(
        _merge_kernel,
        grid=(n // tm,),
        in_specs=[
            pl.BlockSpec((tm, A_WIDTH), row),
            pl.BlockSpec((tm, B_WIDTH), row),
            pl.BlockSpec((tm, D_MODEL), lambda i: (i, COL_GA // D_MODEL)),
            pl.BlockSpec((tm, D_MODEL), lambda i: (i, COL_GB // D_MODEL)),
            pl.BlockSpec((tm, D_MODEL), row),
            _const_spec(wpa.shape), _const_spec(wpb.shape), _const_spec(wout.shape),
            _const_spec(ln_g.shape), _const_spec(ln_b.shape),
        ],
        out_specs=pl.BlockSpec((tm, D_MODEL), row),
        out_shape=jax.ShapeDtypeStruct((n, D_MODEL), F32),
        compiler_params=_compiler_params(("parallel",)),
    )(ya, yb, p, p, x, wpa, wpb, wout, ln_g, ln_b)


ROUTE_TM = 128
ID_SENTINEL = 1 << 30


def _topk_rows(s, ids, k):
    vals, picked = [], []
    for _ in range(k):
        m = jnp.max(s, axis=0, keepdims=True)
        ix = jnp.min(jnp.where(s == m, ids, ID_SENTINEL), axis=0, keepdims=True)
        s = jnp.where(ids == ix, -jnp.inf, s)
        vals.append(m)
        picked.append(ix)
    return jnp.concatenate(vals, axis=0), jnp.concatenate(picked, axis=0)


def _candidate_blocks(v1, v2):
    t = v1.shape[1]
    sub = lax.broadcasted_iota(jnp.int32, (SUBLANES, t), 0)
    neg = jnp.full((SUBLANES, t), -jnp.inf, F32)
    k = PEER_TOPK
    blocks = []
    blocks.append((v1[0:1] + v2[0:8], sub))
    blocks.append((v1[0:1] + v2[8:16], sub + 8))
    blocks.append((v1[1:2] + v2[0:8], sub + k))
    blocks.append((jnp.where(sub < 5, v1[2:3] + v2[0:8], neg), sub + 2 * k))
    blocks.append((jnp.where(sub < 4, v1[3:4] + v2[0:8], neg), sub + 3 * k))
    blocks.append((v1[8:16] + v2[0:1], (sub + 8) * k))
    a47 = jnp.concatenate([v1[4:8], v1[4:8]], axis=0)
    blocks.append((a47 + jnp.where(sub < 4, v2[0:1], v2[1:2]),
                   (4 + lax.bitwise_and(sub, 3)) * k + lax.shift_right_logical(sub, 2)))
    blocks.append((jnp.where(sub == 0, v1[4:5] + v2[2:3], neg), sub + 4 * k + 2))
    vals = jnp.concatenate([b[0] for b in blocks], axis=0)
    ids = jnp.concatenate([b[1] for b in blocks], axis=0)
    return vals, ids


def _route_kernel(h_ref, wq_ref, sk1_ref, sk2_ref, idx_ref, gate_ref):
    q = jnp.dot(h_ref[...].astype(BF16), wq_ref[...], preferred_element_type=F32).astype(BF16)
    half = PEER_QDIM // 2
    key_ids = lax.broadcasted_iota(jnp.int32, (PEER_KEYS, h_ref.shape[0]), 0)
    experts, gates = [], []
    for hd in range(PEER_HEADS):
        q1 = q[:, hd * PEER_QDIM:hd * PEER_QDIM + half]
        q2 = q[:, hd * PEER_QDIM + half:(hd + 1) * PEER_QDIM]
        s1 = lax.dot_general(sk1_ref[...], q1, NT_DIMS, preferred_element_type=F32)
        s2 = lax.dot_general(sk2_ref[...], q2, NT_DIMS, preferred_element_type=F32)
        v1, i1 = _topk_rows(s1, key_ids, PEER_TOPK)
        v2, i2 = _topk_rows(s2, key_ids, PEER_TOPK)
        sc, ci = _topk_rows(*_candidate_blocks(v1, v2), PEER_TOPK)
        ca = lax.shift_right_logical(ci, int(math.log2(PEER_TOPK)))
        cb = lax.bitwise_and(ci, PEER_TOPK - 1)
        e1 = jnp.zeros_like(ci)
        e2 = jnp.zeros_like(ci)
        for a in range(PEER_TOPK):
            e1 = jnp.where(ca == a, i1[a:a + 1], e1)
            e2 = jnp.where(cb == a, i2[a:a + 1], e2)
        experts.append(e1 * PEER_KEYS + e2)
        e = jnp.exp(sc - sc[0:1])
        gates.append(e / jnp.sum(e, axis=0, keepdims=True))
    idx_ref[...] = jnp.concatenate(experts, axis=0).astype(F32).T.astype(jnp.int32)
    gate_ref[...] = jnp.concatenate(gates, axis=0).T


def _route(h, wq, sk1, sk2):
    n = h.shape[0]
    tm = ROUTE_TM
    row = lambda i: (i, 0)
    return pl.pallas_call(
        _route_kernel,
        grid=(n // tm,),
        in_specs=[pl.BlockSpec((tm, D_MODEL), row), _const_spec(wq.shape),
                  _const_spec(sk1.shape), _const_spec(sk2.shape)],
        out_specs=[pl.BlockSpec((tm, PEER_PICKS), row), pl.BlockSpec((tm, PEER_PICKS), row)],
        out_shape=[jax.ShapeDtypeStruct((n, PEER_PICKS), jnp.int32),
                   jax.ShapeDtypeStruct((n, PEER_PICKS), F32)],
        compiler_params=_compiler_params(("parallel",)),
    )(h, wq, sk1, sk2)


PEER_TB = 32
PEER_SLOTS = 8
PEER_LOOKAHEAD = PEER_SLOTS - 1
ROW_TILES = D_MODEL // LANES
HALF_TILES = ROW_TILES // 2

def _gelu_exact(x):
    return 0.5 * x * (1.0 + lax.erf(x * (2.0 ** -0.5)))


def _peer_kernel(idx_ref, gate_ref, h_ref, uv_ref, g_ref, b_ref, y_ref, *scratch):
    bufs, (sem, ffn) = scratch[:PEER_SLOTS], scratch[PEER_SLOTS:]
    i = pl.program_id(0)
    last = pl.num_programs(0) - 1
    tb = h_ref.shape[0]

    def issue(t, slot, part=None):
        per_part = PEER_PICKS // ROW_TILES
        picks = range(PEER_PICKS) if part is None else range(part * per_part, (part + 1) * per_part)
        for k in picks:
            dst = bufs[slot].at[:, k // SUBLANES, pl.ds(k % SUBLANES, 1), :]
            pltpu.make_async_copy(uv_ref.at[idx_ref[0, t, k]], dst, sem.at[slot]).start()

    def wait_slot(slot):
        pltpu.make_async_copy(bufs[slot], bufs[slot], sem.at[slot]).wait()

    def picks_tile(slot, c):
        return bufs[slot][c].reshape(PEER_PICKS, LANES)

    @pl.when(i == 0)
    def _():
        for t in range(PEER_LOOKAHEAD):
            issue(t, t)

    eye = (lax.broadcasted_iota(jnp.int32, (PEER_PICKS, PEER_PICKS), 0)
           == lax.broadcasted_iota(jnp.int32, (PEER_PICKS, PEER_PICKS), 1))

    def halves(words):
        lo = lax.bitcast_convert_type(lax.shift_left(words, jnp.int32(16)), F32)
        hi = lax.bitcast_convert_type(lax.bitwise_and(words, jnp.int32(-(1 << 16))), F32)
        return lo, hi

    def pick_weights(t, slot, between):
        h_row = h_ref[pl.ds(t, 1), :]
        acc = None
        for c in range(HALF_TILES):
            between(c)
            lo, hi = halves(picks_tile(slot, c))
            term = (lo * h_row[:, c * LANES:(c + 1) * LANES]
                    + hi * h_row[:, (HALF_TILES + c) * LANES:(HALF_TILES + c + 1) * LANES])
            acc = term if acc is None else acc + term
        a_col = jnp.sum(acc, axis=1, keepdims=True)
        a_row = jnp.sum(jnp.where(eye, a_col, 0.0), axis=0, keepdims=True)
        w_row = gate_ref[pl.ds(t, 1), :] * _gelu_exact(a_row)
        return jnp.sum(jnp.where(eye, w_row, 0.0), axis=1, keepdims=True)

    def weighted_sum(t, slot, w_col):
        out_lo, out_hi = [], []
        for c in range(HALF_TILES):
            lo, hi = halves(picks_tile(slot, HALF_TILES + c))
            out_lo.append(jnp.sum(w_col * lo, axis=0, keepdims=True))
            out_hi.append(jnp.sum(w_col * hi, axis=0, keepdims=True))
        ffn[pl.ds(t, 1), :] = jnp.concatenate(out_lo + out_hi, axis=1)

    def slot_round(r, carry):
        for slot in range(PEER_SLOTS):
            t = r * PEER_SLOTS + slot
            wait_slot(slot)

            def issue_ahead(part, t=t, slot=slot):
                issue(t + PEER_LOOKAHEAD, (slot + PEER_LOOKAHEAD) % PEER_SLOTS, part)

            w_col = pick_weights(t, slot, issue_ahead)
            for part in range(HALF_TILES, ROW_TILES):
                issue_ahead(part)
            weighted_sum(t, slot, w_col)
        return carry

    lax.fori_loop(0, tb // PEER_SLOTS, slot_round, 0)

    @pl.when(i == last)
    def _():
        for t in range(PEER_LOOKAHEAD):
            wait_slot(t)

    y_ref[...] = _layer_norm(DEEPNORM_ALPHA * h_ref[...] + ffn[...], g_ref[...], b_ref[...])


def _peer(h, idx, gate, uv, ln_g, ln_b):
    n = h.shape[0]
    tb = min(PEER_TB, n)
    assert n % tb == 0 and tb % PEER_SLOTS == 0
    nb = n // tb
    row = lambda i: (i, 0)
    idx_pad = jnp.concatenate([idx, jnp.zeros((tb, PEER_PICKS), idx.dtype)], axis=0)
    idx_next = idx_pad[tb:].reshape(nb, tb, PEER_PICKS)[:, :PEER_LOOKAHEAD]
    idx_ext = jnp.concatenate([idx.reshape(nb, tb, PEER_PICKS), idx_next], axis=1)
    return pl.pallas_call(
        _peer_kernel,
        grid=(nb,),
        in_specs=[
            pl.BlockSpec((1, tb + PEER_LOOKAHEAD, PEER_PICKS), lambda i: (i, 0, 0), memory_space=pltpu.SMEM),
            pl.BlockSpec((tb, PEER_PICKS), row),
            pl.BlockSpec((tb, D_MODEL), row),
            pl.BlockSpec(memory_space=pl.ANY),
            _const_spec(ln_g.shape), _const_spec(ln_b.shape),
        ],
        out_specs=pl.BlockSpec((tb, D_MODEL), row),
        out_shape=jax.ShapeDtypeStruct((n, D_MODEL), F32),
        scratch_shapes=(
            [pltpu.VMEM((ROW_TILES, PEER_PICKS // SUBLANES, SUBLANES, LANES), jnp.int32) for _ in range(PEER_SLOTS)]
            + [pltpu.SemaphoreType.DMA((PEER_SLOTS,)), pltpu.VMEM((tb, D_MODEL), F32)]
        ),
        compiler_params=_compiler_params(("arbitrary",)),
    )(idx_ext, gate, h, uv, ln_g, ln_b)


def _rel_bias_pairs(rel_table, t, tk):
    dist = (tk - 1) - jnp.arange(tk + t - 1)
    line = jnp.take(rel_table, jnp.clip(dist, -A_REL_CLIP, A_REL_CLIP) + A_REL_CLIP, axis=1).astype(F32)
    bias = jnp.stack([line[:, t - 1 - q:t - 1 - q + tk] for q in range(t)], axis=1)
    return bias.reshape(A_HEADS // 2, 2 * t, tk)


PACK_ROWS = 256


def _pack_kernel(u_ref, v_ref, out_ref, stage, sem):
    i = pl.program_id(0)
    buf = lax.rem(i, 2)
    rows = u_ref.shape[0]
    half = D_MODEL // 2

    def drain(b):
        pltpu.make_async_copy(stage.at[b], stage.at[b], sem.at[b]).wait()

    @pl.when(i >= 2)
    def _():
        drain(buf)

    def words(x_ref):
        def bf16_bits(x):
            return lax.bitcast_convert_type(x.astype(BF16).astype(F32), jnp.int32)
        lo = lax.shift_right_logical(bf16_bits(x_ref[:, 0:half]), jnp.int32(16))
        hi = lax.bitwise_and(bf16_bits(x_ref[:, half:D_MODEL]), jnp.int32(-(1 << 16)))
        return lax.bitwise_or(hi, lo)

    for part, x_ref in enumerate((u_ref, v_ref)):
        w = words(x_ref)
        for c in range(HALF_TILES):
            tile = w[:, c * LANES:(c + 1) * LANES].reshape(rows // SUBLANES, SUBLANES, LANES)
            stage[buf, part * HALF_TILES + c] = tile

    def row_copy(r, carry):
        for s in range(SUBLANES):
            src = stage.at[buf, :, r, pl.ds(s, 1), :]
            pltpu.make_async_copy(src, out_ref.at[i * rows + r * SUBLANES + s], sem.at[buf]).start()
        return carry

    lax.fori_loop(0, rows // SUBLANES, row_copy, 0)

    @pl.when(i == pl.num_programs(0) - 1)
    def _():
        drain(buf)

        @pl.when(i >= 1)
        def _():
            drain(1 - buf)


def _pack_expert_tables(u, v):
    n = u.shape[0]
    rows = PACK_ROWS
    assert n % rows == 0
    return pl.pallas_call(
        _pack_kernel,
        grid=(n // rows,),
        in_specs=[pl.BlockSpec((rows, D_MODEL), lambda i: (i, 0)), pl.BlockSpec((rows, D_MODEL), lambda i: (i, 0))],
        out_specs=pl.BlockSpec(memory_space=pl.ANY),
        out_shape=jax.ShapeDtypeStruct((n, ROW_TILES, 1, LANES), jnp.int32),
        scratch_shapes=[pltpu.VMEM((2, ROW_TILES, rows // SUBLANES, SUBLANES, LANES), jnp.int32),
                        pltpu.SemaphoreType.DMA((2,))],
        compiler_params=_compiler_params(("arbitrary",)),
    )(u, v)


def _permute_in_proj(w_in):
    offs = np.cumsum([0, A_WIDTH, A_WIDTH, A_WIDTH, B_WIDTH, B_KV_WIDTH, B_KV_WIDTH, D_MODEL, D_MODEL])
    qa, ka, va, qb, kb, vb, ga, gb = [w_in[:, offs[s]:offs[s + 1]] for s in range(8)]
    return jnp.concatenate([ga, gb, qa, qb, ka, va, kb, vb], axis=1).astype(BF16)


def _split_kv(kv, batch, t):
    ka = kv[:, 0:A_WIDTH].reshape(batch, t, A_HEADS, HEAD_DIM)
    va = kv[:, A_WIDTH:2 * A_WIDTH].reshape(batch, t, A_HEADS, HEAD_DIM)
    kb = kv[:, 2 * A_WIDTH:2 * A_WIDTH + B_KV_WIDTH].reshape(batch, t, B_KV_HEADS, HEAD_DIM)
    vb = kv[:, 2 * A_WIDTH + B_KV_WIDTH:].reshape(batch, t, B_KV_HEADS, HEAD_DIM)
    return ka, va, kb, vb


def kernel(x_prompt, x_sample, cache_a_k, cache_a_v, cache_b_k, cache_b_v, w_in, rel_bias_table, b_sinks,
           w_proj_a, w_proj_b, w_out, ln1_g, ln1_b, peer_w_query, peer_sub_keys_1, peer_sub_keys_2,
           peer_u, peer_v, ln2_g, ln2_b):
    batch, seq, _ = x_prompt.shape
    dbatch, dseq, _ = x_sample.shape
    assert seq % ATT_BLOCK == 0 and seq >= ATT_BLOCK

    w_in_p = _permute_in_proj(w_in)
    wpa, wpb, wout = w_proj_a.astype(BF16), w_proj_b.astype(BF16), w_out.astype(BF16)
    wq = peer_w_query.astype(BF16)
    sk1, sk2 = peer_sub_keys_1.astype(BF16), peer_sub_keys_2.astype(BF16)
    uv = _pack_expert_tables(peer_u, peer_v)
    ln1 = (ln1_g.reshape(1, D_MODEL), ln1_b.reshape(1, D_MODEL))
    ln2 = (ln2_g.reshape(1, D_MODEL), ln2_b.reshape(1, D_MODEL))
    sinks = b_sinks.astype(F32)

    def tail(xf, ya, yb, p, tm):
        h = _merge(ya, yb, p, xf, wpa, wpb, wout, *ln1, tm=tm)
        idx, gate = _route(h, wq, sk1, sk2)
        return _peer(h, idx, gate, uv, *ln2)

    ns = dbatch * dseq
    xs = x_sample.reshape(ns, D_MODEL)
    p_s, kv_s = _in_proj(xs, w_in_p, tm=ns)
    ka_s, va_s, kb_s, vb_s = _split_kv(kv_s, dbatch, dseq)

    def with_cache(cache, new_cols, width):
        new = p_s[:, new_cols:new_cols + width].reshape(dbatch, dseq, width)
        return jnp.concatenate([cache.reshape(dbatch, -1, width).astype(BF16), new], axis=1)

    kcat_a = with_cache(cache_a_k, COL_KA, A_WIDTH)
    vcat_a = with_cache(cache_a_v, COL_VA, A_WIDTH)
    kcat_b = with_cache(cache_b_k, COL_KB, B_KV_WIDTH)
    vcat_b = with_cache(cache_b_v, COL_VB, B_KV_WIDTH)
    bias_s = _rel_bias_pairs(rel_bias_table, dseq, kcat_a.shape[1])
    ya_s, yb_s = _attn_sample(p_s, kcat_a, vcat_a, kcat_b, vcat_b, bias_s, sinks, dbatch, dseq)
    y_sample = tail(xs, ya_s, yb_s, p_s, ns).reshape(dbatch, dseq, D_MODEL)

    xp = x_prompt.reshape(batch * seq, D_MODEL)
    p, kv = _in_proj(xp, w_in_p, tm=512)
    bias_p = _rel_bias_pairs(rel_bias_table, CHUNK, (A_PREV_CHUNKS + 1) * CHUNK)
    ya, yb = _attn_prompt(p, bias_p, sinks, batch, seq)
    y_prompt = tail(xp, ya, yb, p, 256).reshape(batch, seq, D_MODEL)
    ka, va, kb, vb = _split_kv(kv, batch, seq)
    la = min(A_PREV_CHUNKS * CHUNK, seq)
    lb = min(B_PREV_CHUNKS * CHUNK, seq)

    return (y_prompt, y_sample, ka[:, -la:], va[:, -la:], kb[:, -lb:], vb[:, -lb:], ka_s, va_s, kb_s, vb_s)
```

```python
import math

import jax
import jax.numpy as jnp
import numpy as np
from jax import lax
from jax.experimental import pallas as pl
from jax.experimental.pallas import tpu as pltpu

F32 = jnp.float32
BF16 = jnp.bfloat16

D_MODEL = 2048
CHUNK = 64
HEAD_DIM = 64
A_HEADS = 16
A_PREV_CHUNKS = 8
A_REL_CLIP = 128
B_Q_HEADS = 16
B_KV_HEADS = 2
B_PREV_CHUNKS = 2
A_WIDTH = A_HEADS * HEAD_DIM
B_WIDTH = B_Q_HEADS * HEAD_DIM
B_KV_WIDTH = B_KV_HEADS * HEAD_DIM
PEER_HEADS = 8
PEER_KEYS = 128
PEER_QDIM = 256
PEER_TOPK = 16
PEER_PICKS = PEER_HEADS * PEER_TOPK
LN_EPS = 1e-5
DEEPNORM_ALPHA = 2.0 ** 0.25

LANES = 128
SUBLANES = 8
HEAD_PAIR = 2 * HEAD_DIM
VMEM_LIMIT = 56 * 1024 * 1024

COL_GA = 0
COL_GB = COL_GA + D_MODEL
COL_QA = COL_GB + D_MODEL
COL_QB = COL_QA + A_WIDTH
COL_KA = COL_QB + B_WIDTH
COL_VA = COL_KA + A_WIDTH
COL_KB = COL_VA + A_WIDTH
COL_VB = COL_KB + B_KV_WIDTH
IN_COLS = COL_VB + B_KV_WIDTH
KV_COLS = IN_COLS - COL_KA
PROJ_TN = 768
ATT_BLOCK = A_PREV_CHUNKS * CHUNK

NT_DIMS = (((1,), (1,)), ((), ()))


def _compiler_params(semantics):
    return pltpu.CompilerParams(dimension_semantics=semantics, vmem_limit_bytes=VMEM_LIMIT)


def _const_spec(shape):
    zeros = (0,) * len(shape)
    return pl.BlockSpec(shape, lambda *_: zeros, pipeline_mode=pl.Buffered(1))


def _in_proj_kernel(x_ref, w_ref, p_ref, kv_ref, xb_ref):
    j = pl.program_id(1)

    @pl.when(j == 0)
    def _():
        xb_ref[...] = x_ref[...].astype(BF16)

    acc = jnp.dot(xb_ref[...], w_ref[...], preferred_element_type=F32)
    p_ref[...] = acc.astype(BF16)

    @pl.when(j >= COL_KA // PROJ_TN)
    def _():
        kv_ref[...] = acc


def _in_proj(x, w_bf16, tm):
    n = x.shape[0]
    first_kv = COL_KA // PROJ_TN
    return pl.pallas_call(
        _in_proj_kernel,
        grid=(n // tm, IN_COLS // PROJ_TN),
        in_specs=[
            pl.BlockSpec((tm, D_MODEL), lambda i, j: (i, 0)),
            pl.BlockSpec((D_MODEL, PROJ_TN), lambda i, j: (0, j)),
        ],
        out_specs=[
            pl.BlockSpec((tm, PROJ_TN), lambda i, j: (i, j)),
            pl.BlockSpec((tm, PROJ_TN), lambda i, j: (i, jnp.maximum(j - first_kv, 0))),
        ],
        out_shape=[
            jax.ShapeDtypeStruct((n, IN_COLS), BF16),
            jax.ShapeDtypeStruct((n, KV_COLS), F32),
        ],
        scratch_shapes=[pltpu.VMEM((tm, D_MODEL), BF16)],
        compiler_params=_compiler_params(("parallel", "arbitrary")),
    )(x, w_bf16)


def _softmax_pv(s, v2, sink):
    m = jnp.max(s, axis=-1, keepdims=True)
    if sink is not None:
        m = jnp.maximum(m, sink)
    p = jnp.exp(s - m)
    den = jnp.sum(p, axis=-1, keepdims=True)
    if sink is not None:
        den = den + jnp.exp(sink - m)
    o = jnp.dot(p.astype(BF16), v2, preferred_element_type=F32)
    return o / den


def _scores(qq, k2, bias, key_thresh):
    s = lax.dot_general(qq, k2, NT_DIMS, preferred_element_type=F32) * (HEAD_DIM ** -0.5) + bias
    if key_thresh is not None:
        col = lax.broadcasted_iota(jnp.int32, s.shape, 1)
        s = jnp.where(col >= key_thresh, s, -jnp.inf)
    return s


def _mixer_a(q_of, k_of, v_of, bias_of, key_thresh, store):
    for j in range(A_HEADS // 2):
        q2 = q_of(j)
        tq = q2.shape[0]
        lane_hi = lax.broadcasted_iota(jnp.int32, q2.shape, 1) >= HEAD_DIM
        zero = jnp.zeros_like(q2)
        qq = jnp.concatenate([jnp.where(lane_hi, zero, q2), jnp.where(lane_hi, q2, zero)], axis=0)
        o = _softmax_pv(_scores(qq, k_of(j), bias_of(j), key_thresh), v_of(j), None)
        store(j, jnp.where(lane_hi, o[tq:2 * tq], o[0:tq]))


def _alibi_rows(absdist):
    return jnp.concatenate([(-(2.0 ** (-8.0 * (h + 1) / B_Q_HEADS))) * absdist for h in range(B_Q_HEADS)], axis=0)


def _sink_rows(sink_ref, tq):
    return jnp.concatenate([jnp.full((tq, 1), sink_ref[h], F32) for h in range(B_Q_HEADS)], axis=0)


def _mixer_b(q_of, k2, v2, bias_rows, sink_rows, key_thresh, store):
    group = B_Q_HEADS // B_KV_HEADS
    pieces = []
    for j in range(B_Q_HEADS // 2):
        q2 = q_of(j)
        tq = q2.shape[0]
        lane_hi = lax.broadcasted_iota(jnp.int32, q2.shape, 1) >= HEAD_DIM
        zero = jnp.zeros_like(q2)
        q2_swapped = pltpu.roll(q2.astype(F32), HEAD_DIM, 1).astype(BF16)
        for hh in range(2):
            kv = (2 * j + hh) // group
            src = q2 if hh == kv else q2_swapped
            pieces.append(jnp.where(lane_hi, src, zero) if kv else jnp.where(lane_hi, zero, src))
    qq = jnp.concatenate(pieces, axis=0)
    o = _softmax_pv(_scores(qq, k2, bias_rows, key_thresh), v2, sink_rows)
    lane_hi = lax.broadcasted_iota(jnp.int32, (tq, HEAD_PAIR), 1) >= HEAD_DIM
    for j in range(B_Q_HEADS // 2):
        halves = []
        for hh in range(2):
            h = 2 * j + hh
            o_h = o[h * tq:(h + 1) * tq]
            halves.append(o_h if hh == h // group else pltpu.roll(o_h, HEAD_DIM, 1))
        store(j, jnp.where(lane_hi, halves[1], halves[0]))


def _attn_prompt_kernel(sink_ref, qa_ref, kap_ref, kac_ref, vap_ref, vac_ref,
                        qb_ref, kbp_ref, kbc_ref, vbp_ref, vbc_ref, bias_ref,
                        ya_ref, yb_ref, ka_s, va_s, kb_s, vb_s, bias_b_s, sink_s):
    i = pl.program_id(1)
    blk = ATT_BLOCK
    ka_s[0:blk, :] = kap_ref[...]
    ka_s[blk:2 * blk, :] = kac_ref[...]
    va_s[0:blk, :] = vap_ref[...]
    va_s[blk:2 * blk, :] = vac_ref[...]
    kb_s[0:blk, :] = kbp_ref[...]
    kb_s[blk:2 * blk, :] = kbc_ref[...]
    vb_s[0:blk, :] = vbp_ref[...]
    vb_s[blk:2 * blk, :] = vbc_ref[...]

    tk_a = (A_PREV_CHUNKS + 1) * CHUNK
    tk_b = (B_PREV_CHUNKS + 1) * CHUNK
    b_off = blk - B_PREV_CHUNKS * CHUNK
    tq = lax.broadcasted_iota(jnp.int32, (CHUNK, tk_b), 0)
    ts = lax.broadcasted_iota(jnp.int32, (CHUNK, tk_b), 1)
    bias_b_s[...] = _alibi_rows(jnp.abs(tq + B_PREV_CHUNKS * CHUNK - ts).astype(F32))
    sink_s[...] = _sink_rows(sink_ref, CHUNK)

    def chunk_body(c, carry):
        row0 = pl.multiple_of(c * CHUNK, CHUNK)
        thresh_a = jnp.where(i > 0, 0, blk - c * CHUNK)
        thresh_b = jnp.where(i > 0, 0, B_PREV_CHUNKS * CHUNK - c * CHUNK)

        def store_a(j, val):
            ya_ref[pl.ds(row0, CHUNK), j * HEAD_PAIR:(j + 1) * HEAD_PAIR] = val.astype(ya_ref.dtype)

        def store_b(j, val):
            yb_ref[pl.ds(row0, CHUNK), j * HEAD_PAIR:(j + 1) * HEAD_PAIR] = val.astype(yb_ref.dtype)

        _mixer_a(
            lambda j: qa_ref[pl.ds(row0, CHUNK), j * HEAD_PAIR:(j + 1) * HEAD_PAIR],
            lambda j: ka_s[pl.ds(row0, tk_a), j * HEAD_PAIR:(j + 1) * HEAD_PAIR],
            lambda j: va_s[pl.ds(row0, tk_a), j * HEAD_PAIR:(j + 1) * HEAD_PAIR],
            lambda j: bias_ref[j], thresh_a, store_a)

        rowb = pl.multiple_of(b_off + c * CHUNK, CHUNK)
        _mixer_b(
            lambda j: qb_ref[pl.ds(row0, CHUNK), j * HEAD_PAIR:(j + 1) * HEAD_PAIR],
            kb_s[pl.ds(rowb, tk_b), :], vb_s[pl.ds(rowb, tk_b), :],
            bias_b_s[...], sink_s[...], thresh_b, store_b)
        return carry

    lax.fori_loop(0, blk // CHUNK, chunk_body, 0)


def _attn_prompt(p, bias_a, sinks, batch, seq):
    blk = ATT_BLOCK
    nb = seq // blk
    n = batch * seq

    def cur(col_block):
        return lambda b, i: (b * nb + i, col_block)

    def prev(col_block):
        return lambda b, i: (b * nb + jnp.maximum(i - 1, 0), col_block)

    wa, wb = A_WIDTH, B_KV_WIDTH
    in_specs = [
        pl.BlockSpec(memory_space=pltpu.SMEM),
        pl.BlockSpec((blk, wa), cur(COL_QA // wa)),
        pl.BlockSpec((blk, wa), prev(COL_KA // wa)),
        pl.BlockSpec((blk, wa), cur(COL_KA // wa)),
        pl.BlockSpec((blk, wa), prev(COL_VA // wa)),
        pl.BlockSpec((blk, wa), cur(COL_VA // wa)),
        pl.BlockSpec((blk, wa), cur(COL_QB // wa)),
        pl.BlockSpec((blk, wb), prev(COL_KB // wb)),
        pl.BlockSpec((blk, wb), cur(COL_KB // wb)),
        pl.BlockSpec((blk, wb), prev(COL_VB // wb)),
        pl.BlockSpec((blk, wb), cur(COL_VB // wb)),
        _const_spec(bias_a.shape),
    ]
    out_spec = pl.BlockSpec((blk, wa), lambda b, i: (b * nb + i, 0))
    rows_b = B_Q_HEADS * CHUNK
    return pl.pallas_call(
        _attn_prompt_kernel,
        grid=(batch, nb),
        in_specs=in_specs,
        out_specs=[out_spec, out_spec],
        out_shape=[jax.ShapeDtypeStruct((n, A_WIDTH), BF16), jax.ShapeDtypeStruct((n, B_WIDTH), BF16)],
        scratch_shapes=[
            pltpu.VMEM((2 * blk, wa), BF16), pltpu.VMEM((2 * blk, wa), BF16),
            pltpu.VMEM((2 * blk, wb), BF16), pltpu.VMEM((2 * blk, wb), BF16),
            pltpu.VMEM((rows_b, (B_PREV_CHUNKS + 1) * CHUNK), F32), pltpu.VMEM((rows_b, 1), F32),
        ],
        compiler_params=_compiler_params(("parallel", "arbitrary")),
    )(sinks, p, p, p, p, p, p, p, p, p, p, bias_a)


def _attn_sample_kernel(sink_ref, qa_ref, ka_ref, va_ref, qb_ref, kb_ref, vb_ref, bias_ref, ya_ref, yb_ref):
    t, tk_b = qb_ref.shape[0], kb_ref.shape[1]

    def store_a(j, val):
        ya_ref[:, j * HEAD_PAIR:(j + 1) * HEAD_PAIR] = val.astype(ya_ref.dtype)

    def store_b(j, val):
        yb_ref[:, j * HEAD_PAIR:(j + 1) * HEAD_PAIR] = val.astype(yb_ref.dtype)

    _mixer_a(
        lambda j: qa_ref[:, j * HEAD_PAIR:(j + 1) * HEAD_PAIR],
        lambda j: ka_ref[0, :, j * HEAD_PAIR:(j + 1) * HEAD_PAIR],
        lambda j: va_ref[0, :, j * HEAD_PAIR:(j + 1) * HEAD_PAIR],
        lambda j: bias_ref[j], None, store_a)

    tq = lax.broadcasted_iota(jnp.int32, (t, tk_b), 0)
    ts = lax.broadcasted_iota(jnp.int32, (t, tk_b), 1)
    bias_rows = _alibi_rows(jnp.abs(tq + (tk_b - t) - ts).astype(F32))
    _mixer_b(
        lambda j: qb_ref[:, j * HEAD_PAIR:(j + 1) * HEAD_PAIR],
        kb_ref[0], vb_ref[0], bias_rows, _sink_rows(sink_ref, t), None, store_b)


def _attn_sample(p, kcat_a, vcat_a, kcat_b, vcat_b, bias_a, sinks, batch, t):
    n = batch * t
    wa, wb = A_WIDTH, B_KV_WIDTH
    tk_a, tk_b = kcat_a.shape[1], kcat_b.shape[1]
    in_specs = [
        pl.BlockSpec(memory_space=pltpu.SMEM),
        pl.BlockSpec((t, wa), lambda b: (b, COL_QA // wa)),
        pl.BlockSpec((1, tk_a, wa), lambda b: (b, 0, 0)),
        pl.BlockSpec((1, tk_a, wa), lambda b: (b, 0, 0)),
        pl.BlockSpec((t, wa), lambda b: (b, COL_QB // wa)),
        pl.BlockSpec((1, tk_b, wb), lambda b: (b, 0, 0)),
        pl.BlockSpec((1, tk_b, wb), lambda b: (b, 0, 0)),
        _const_spec(bias_a.shape),
    ]
    out_spec = pl.BlockSpec((t, wa), lambda b: (b, 0))
    return pl.pallas_call(
        _attn_sample_kernel,
        grid=(batch,),
        in_specs=in_specs,
        out_specs=[out_spec, out_spec],
        out_shape=[jax.ShapeDtypeStruct((n, A_WIDTH), BF16), jax.ShapeDtypeStruct((n, B_WIDTH), BF16)],
        compiler_params=_compiler_params(("parallel",)),
    )(sinks, p, kcat_a, vcat_a, p, kcat_b, vcat_b, bias_a)


def _layer_norm(x, g, b):
    mu = jnp.mean(x, axis=-1, keepdims=True)
    xc = x - mu
    var = jnp.mean(xc * xc, axis=-1, keepdims=True)
    return xc * lax.rsqrt(var + LN_EPS) * g + b


def _merge_kernel(ya_ref, yb_ref, ga_ref, gb_ref, x_ref, wpa_ref, wpb_ref, wout_ref, g_ref, b_ref, h_ref):
    za = jnp.dot(ya_ref[...], wpa_ref[...], preferred_element_type=F32)
    zb = jnp.dot(yb_ref[...], wpb_ref[...], preferred_element_type=F32)
    m = jax.nn.sigmoid(ga_ref[...].astype(F32)) * za + jax.nn.sigmoid(gb_ref[...].astype(F32)) * zb
    r = jnp.dot(m.astype(BF16), wout_ref[...], preferred_element_type=F32)
    h_ref[...] = _layer_norm(DEEPNORM_ALPHA * x_ref[...] + r, g_ref[...], b_ref[...])


def _merge(ya, yb, p, x, wpa, wpb, wout, ln_g, ln_b, tm):
    n = x.shape[0]
    row = lambda i: (i, 0)
    return pl.pallas_call(
        _merge_kernel,
        grid=(n // tm,),
        in_specs=[
            pl.BlockSpec((tm, A_WIDTH), row),
            pl.BlockSpec((tm, B_WIDTH), row),
            pl.BlockSpec((tm, D_MODEL), lambda i: (i, COL_GA // D_MODEL)),
            pl.BlockSpec((tm, D_MODEL), lambda i: (i, COL_GB // D_MODEL)),
            pl.BlockSpec((tm, D_MODEL), row),
            _const_spec(wpa.shape), _const_spec(wpb.shape), _const_spec(wout.shape),
            _const_spec(ln_g.shape), _const_spec(ln_b.shape),
        ],
        out_specs=pl.BlockSpec((tm, D_MODEL), row),
        out_shape=jax.ShapeDtypeStruct((n, D_MODEL), F32),
        compiler_params=_compiler_params(("parallel",)),
    )(ya, yb, p, p, x, wpa, wpb, wout, ln_g, ln_b)


ROUTE_TM = 128
ID_SENTINEL = 1 << 30


def _topk_rows(s, ids, k):
    vals, picked = [], []
    for _ in range(k):
        m = jnp.max(s, axis=0, keepdims=True)
        ix = jnp.min(jnp.where(s == m, ids, ID_SENTINEL), axis=0, keepdims=True)
        s = jnp.where(ids == ix, -jnp.inf, s)
        vals.append(m)
        picked.append(ix)
    return jnp.concatenate(vals, axis=0), jnp.concatenate(picked, axis=0)


def _candidate_blocks(v1, v2):
    t = v1.shape[1]
    sub = lax.broadcasted_iota(jnp.int32, (SUBLANES, t), 0)
    neg = jnp.full((SUBLANES, t), -jnp.inf, F32)
    k = PEER_TOPK
    blocks = []
    blocks.append((v1[0:1] + v2[0:8], sub))
    blocks.append((v1[0:1] + v2[8:16], sub + 8))
    blocks.append((v1[1:2] + v2[0:8], sub + k))
    blocks.append((jnp.where(sub < 5, v1[2:3] + v2[0:8], neg), sub + 2 * k))
    blocks.append((jnp.where(sub < 4, v1[3:4] + v2[0:8], neg), sub + 3 * k))
    blocks.append((v1[8:16] + v2[0:1], (sub + 8) * k))
    a47 = jnp.concatenate([v1[4:8], v1[4:8]], axis=0)
    blocks.append((a47 + jnp.where(sub < 4, v2[0:1], v2[1:2]),
                   (4 + lax.bitwise_and(sub, 3)) * k + lax.shift_right_logical(sub, 2)))
    blocks.append((jnp.where(sub == 0, v1[4:5] + v2[2:3], neg), sub + 4 * k + 2))
    vals = jnp.concatenate([b[0] for b in blocks], axis=0)
    ids = jnp.concatenate([b[1] for b in blocks], axis=0)
    return vals, ids


def _route_kernel(h_ref, wq_ref, sk1_ref, sk2_ref, idx_ref, gate_ref):
    q = jnp.dot(h_ref[...].astype(BF16), wq_ref[...], preferred_element_type=F32).astype(BF16)
    half = PEER_QDIM // 2
    key_ids = lax.broadcasted_iota(jnp.int32, (PEER_KEYS, h_ref.shape[0]), 0)
    experts, gates = [], []
    for hd in range(PEER_HEADS):
        q1 = q[:, hd * PEER_QDIM:hd * PEER_QDIM + half]
        q2 = q[:, hd * PEER_QDIM + half:(hd + 1) * PEER_QDIM]
        s1 = lax.dot_general(sk1_ref[...], q1, NT_DIMS, preferred_element_type=F32)
        s2 = lax.dot_general(sk2_ref[...], q2, NT_DIMS, preferred_element_type=F32)
        v1, i1 = _topk_rows(s1, key_ids, PEER_TOPK)
        v2, i2 = _topk_rows(s2, key_ids, PEER_TOPK)
        sc, ci = _topk_rows(*_candidate_blocks(v1, v2), PEER_TOPK)
        ca = lax.shift_right_logical(ci, int(math.log2(PEER_TOPK)))
        cb = lax.bitwise_and(ci, PEER_TOPK - 1)
        e1 = jnp.zeros_like(ci)
        e2 = jnp.zeros_like(ci)
        for a in range(PEER_TOPK):
            e1 = jnp.where(ca == a, i1[a:a + 1], e1)
            e2 = jnp.where(cb == a, i2[a:a + 1], e2)
        experts.append(e1 * PEER_KEYS + e2)
        e = jnp.exp(sc - sc[0:1])
        gates.append(e / jnp.sum(e, axis=0, keepdims=True))
    idx_ref[...] = jnp.concatenate(experts, axis=0).astype(F32).T.astype(jnp.int32)
    gate_ref[...] = jnp.concatenate(gates, axis=0).T


def _route(h, wq, sk1, sk2):
    n = h.shape[0]
    tm = ROUTE_TM
    row = lambda i: (i, 0)
    return pl.pallas_call(
        _route_kernel,
        grid=(n // tm,),
        in_specs=[pl.BlockSpec((tm, D_MODEL), row), _const_spec(wq.shape),
                  _const_spec(sk1.shape), _const_spec(sk2.shape)],
        out_specs=[pl.BlockSpec((tm, PEER_PICKS), row), pl.BlockSpec((tm, PEER_PICKS), row)],
        out_shape=[jax.ShapeDtypeStruct((n, PEER_PICKS), jnp.int32),
                   jax.ShapeDtypeStruct((n, PEER_PICKS), F32)],
        compiler_params=_compiler_params(("parallel",)),
    )(h, wq, sk1, sk2)


PEER_TB = 32
PEER_SLOTS = 8
PEER_LOOKAHEAD = PEER_SLOTS - 1
ROW_TILES = D_MODEL // LANES
HALF_TILES = ROW_TILES // 2

def _gelu_exact(x):
    return 0.5 * x * (1.0 + lax.erf(x * (2.0 ** -0.5)))


def _peer_kernel(idx_ref, gate_ref, h_ref, uv_ref, g_ref, b_ref, y_ref, *scratch):
    bufs, (sem, ffn) = scratch[:PEER_SLOTS], scratch[PEER_SLOTS:]
    i = pl.program_id(0)
    last = pl.num_programs(0) - 1
    tb = h_ref.shape[0]

    def issue(t, slot, part=None):
        per_part = PEER_PICKS // ROW_TILES
        picks = range(PEER_PICKS) if part is None else range(part * per_part, (part + 1) * per_part)
        for k in picks:
            dst = bufs[slot].at[:, k // SUBLANES, pl.ds(k % SUBLANES, 1), :]
            pltpu.make_async_copy(uv_ref.at[idx_ref[0, t, k]], dst, sem.at[slot]).start(priority=k % 2)

    def wait_slot(slot):
        pltpu.make_async_copy(bufs[slot], bufs[slot], sem.at[slot]).wait()

    def picks_tile(slot, c):
        return bufs[slot][c].reshape(PEER_PICKS, LANES)

    @pl.when(i == 0)
    def _():
        for t in range(PEER_LOOKAHEAD):
            issue(t, t)

    eye = (lax.broadcasted_iota(jnp.int32, (PEER_PICKS, PEER_PICKS), 0)
           == lax.broadcasted_iota(jnp.int32, (PEER_PICKS, PEER_PICKS), 1))

    def halves(words):
        lo = lax.bitcast_convert_type(lax.shift_left(words, jnp.int32(16)), F32)
        hi = lax.bitcast_convert_type(lax.bitwise_and(words, jnp.int32(-(1 << 16))), F32)
        return lo, hi

    def pick_weights(t, slot, between):
        h_row = h_ref[pl.ds(t, 1), :]
        acc = None
        for c in range(HALF_TILES):
            between(c)
            lo, hi = halves(picks_tile(slot, c))
            term = (lo * h_row[:, c * LANES:(c + 1) * LANES]
                    + hi * h_row[:, (HALF_TILES + c) * LANES:(HALF_TILES + c + 1) * LANES])
            acc = term if acc is None else acc + term
        a_col = jnp.sum(acc, axis=1, keepdims=True)
        a_row = jnp.sum(jnp.where(eye, a_col, 0.0), axis=0, keepdims=True)
        w_row = gate_ref[pl.ds(t, 1), :] * _gelu_exact(a_row)
        return jnp.sum(jnp.where(eye, w_row, 0.0), axis=1, keepdims=True)

    def weighted_sum(t, slot, w_col):
        out_lo, out_hi = [], []
        for c in range(HALF_TILES):
            lo, hi = halves(picks_tile(slot, HALF_TILES + c))
            out_lo.append(jnp.sum(w_col * lo, axis=0, keepdims=True))
            out_hi.append(jnp.sum(w_col * hi, axis=0, keepdims=True))
        ffn[pl.ds(t, 1), :] = jnp.concatenate(out_lo + out_hi, axis=1)

    def slot_round(r, carry):
        for slot in range(PEER_SLOTS):
            t = r * PEER_SLOTS + slot
            wait_slot(slot)

            def issue_ahead(part, t=t, slot=slot):
                issue(t + PEER_LOOKAHEAD, (slot + PEER_LOOKAHEAD) % PEER_SLOTS, part)

            w_col = pick_weights(t, slot, issue_ahead)
            for part in range(HALF_TILES, ROW_TILES):
                issue_ahead(part)
            weighted_sum(t, slot, w_col)
        return carry

    lax.fori_loop(0, tb // PEER_SLOTS, slot_round, 0)

    @pl.when(i == last)
    def _():
        for t in range(PEER_LOOKAHEAD):
            wait_slot(t)

    y_ref[...] = _layer_norm(DEEPNORM_ALPHA * h_ref[...] + ffn[...], g_ref[...], b_ref[...])


def _peer(h, idx, gate, uv, ln_g, ln_b):
    n = h.shape[0]
    tb = min(PEER_TB, n)
    assert n % tb == 0 and tb % PEER_SLOTS == 0
    nb = n // tb
    row = lambda i: (i, 0)
    idx_pad = jnp.concatenate([idx, jnp.zeros((tb, PEER_PICKS), idx.dtype)], axis=0)
    idx_next = idx_pad[tb:].reshape(nb, tb, PEER_PICKS)[:, :PEER_LOOKAHEAD]
    idx_ext = jnp.concatenate([idx.reshape(nb, tb, PEER_PICKS), idx_next], axis=1)
    return pl.pallas_call(
        _peer_kernel,
        grid=(nb,),
        in_specs=[
            pl.BlockSpec((1, tb + PEER_LOOKAHEAD, PEER_PICKS), lambda i: (i, 0, 0), memory_space=pltpu.SMEM),
            pl.BlockSpec((tb, PEER_PICKS), row),
            pl.BlockSpec((tb, D_MODEL), row),
            pl.BlockSpec(memory_space=pl.ANY),
            _const_spec(ln_g.shape), _const_spec(ln_b.shape),
        ],
        out_specs=pl.BlockSpec((tb, D_MODEL), row),
        out_shape=jax.ShapeDtypeStruct((n, D_MODEL), F32),
        scratch_shapes=(
            [pltpu.VMEM((ROW_TILES, PEER_PICKS // SUBLANES, SUBLANES, LANES), jnp.int32) for _ in range(PEER_SLOTS)]
            + [pltpu.SemaphoreType.DMA((PEER_SLOTS,)), pltpu.VMEM((tb, D_MODEL), F32)]
        ),
        compiler_params=_compiler_params(("arbitrary",)),
    )(idx_ext, gate, h, uv, ln_g, ln_b)


def _rel_bias_pairs(rel_table, t, tk):
    dist = (tk - 1) - jnp.arange(tk + t - 1)
    line = jnp.take(rel_table, jnp.clip(dist, -A_REL_CLIP, A_REL_CLIP) + A_REL_CLIP, axis=1).astype(F32)
    bias = jnp.stack([line[:, t - 1 - q:t - 1 - q + tk] for q in range(t)], axis=1)
    return bias.reshape(A_HEADS // 2, 2 * t, tk)


PACK_ROWS = 256


def _pack_kernel(u_ref, v_ref, out_ref, stage, sem):
    i = pl.program_id(0)
    buf = lax.rem(i, 2)
    rows = u_ref.shape[0]
    half = D_MODEL // 2

    def drain(b):
        pltpu.make_async_copy(stage.at[b], stage.at[b], sem.at[b]).wait()

    @pl.when(i >= 2)
    def _():
        drain(buf)

    def words(x_ref):
        def bf16_bits(x):
            return lax.bitcast_convert_type(x.astype(BF16).astype(F32), jnp.int32)
        lo = lax.shift_right_logical(bf16_bits(x_ref[:, 0:half]), jnp.int32(16))
        hi = lax.bitwise_and(bf16_bits(x_ref[:, half:D_MODEL]), jnp.int32(-(1 << 16)))
        return lax.bitwise_or(hi, lo)

    for part, x_ref in enumerate((u_ref, v_ref)):
        w = words(x_ref)
        for c in range(HALF_TILES):
            tile = w[:, c * LANES:(c + 1) * LANES].reshape(rows // SUBLANES, SUBLANES, LANES)
            stage[buf, part * HALF_TILES + c] = tile

    def row_copy(r, carry):
        for s in range(SUBLANES):
            src = stage.at[buf, :, r, pl.ds(s, 1), :]
            pltpu.make_async_copy(src, out_ref.at[i * rows + r * SUBLANES + s], sem.at[buf]).start()
        return carry

    lax.fori_loop(0, rows // SUBLANES, row_copy, 0)

    @pl.when(i == pl.num_programs(0) - 1)
    def _():
        drain(buf)

        @pl.when(i >= 1)
        def _():
            drain(1 - buf)


def _pack_expert_tables(u, v):
    n = u.shape[0]
    rows = PACK_ROWS
    assert n % rows == 0
    return pl.pallas_call(
        _pack_kernel,
        grid=(n // rows,),
        in_specs=[pl.BlockSpec((rows, D_MODEL), lambda i: (i, 0)), pl.BlockSpec((rows, D_MODEL), lambda i: (i, 0))],
        out_specs=pl.BlockSpec(memory_space=pl.ANY),
        out_shape=jax.ShapeDtypeStruct((n, ROW_TILES, 1, LANES), jnp.int32),
        scratch_shapes=[pltpu.VMEM((2, ROW_TILES, rows // SUBLANES, SUBLANES, LANES), jnp.int32),
                        pltpu.SemaphoreType.DMA((2,))],
        compiler_params=_compiler_params(("arbitrary",)),
    )(u, v)


def _permute_in_proj(w_in):
    offs = np.cumsum([0, A_WIDTH, A_WIDTH, A_WIDTH, B_WIDTH, B_KV_WIDTH, B_KV_WIDTH, D_MODEL, D_MODEL])
    qa, ka, va, qb, kb, vb, ga, gb = [w_in[:, offs[s]:offs[s + 1]] for s in range(8)]
    return jnp.concatenate([ga, gb, qa, qb, ka, va, kb, vb], axis=1).astype(BF16)


def _split_kv(kv, batch, t):
    ka = kv[:, 0:A_WIDTH].reshape(batch, t, A_HEADS, HEAD_DIM)
    va = kv[:, A_WIDTH:2 * A_WIDTH].reshape(batch, t, A_HEADS, HEAD_DIM)
    kb = kv[:, 2 * A_WIDTH:2 * A_WIDTH + B_KV_WIDTH].reshape(batch, t, B_KV_HEADS, HEAD_DIM)
    vb = kv[:, 2 * A_WIDTH + B_KV_WIDTH:].reshape(batch, t, B_KV_HEADS, HEAD_DIM)
    return ka, va, kb, vb


def kernel(x_prompt, x_sample, cache_a_k, cache_a_v, cache_b_k, cache_b_v, w_in, rel_bias_table, b_sinks,
           w_proj_a, w_proj_b, w_out, ln1_g, ln1_b, peer_w_query, peer_sub_keys_1, peer_sub_keys_2,
           peer_u, peer_v, ln2_g, ln2_b):
    batch, seq, _ = x_prompt.shape
    dbatch, dseq, _ = x_sample.shape
    assert seq % ATT_BLOCK == 0 and seq >= ATT_BLOCK

    w_in_p = _permute_in_proj(w_in)
    wpa, wpb, wout = w_proj_a.astype(BF16), w_proj_b.astype(BF16), w_out.astype(BF16)
    wq = peer_w_query.astype(BF16)
    sk1, sk2 = peer_sub_keys_1.astype(BF16), peer_sub_keys_2.astype(BF16)
    uv = _pack_expert_tables(peer_u, peer_v)
    ln1 = (ln1_g.reshape(1, D_MODEL), ln1_b.reshape(1, D_MODEL))
    ln2 = (ln2_g.reshape(1, D_MODEL), ln2_b.reshape(1, D_MODEL))
    sinks = b_sinks.astype(F32)

    def tail(xf, ya, yb, p, tm):
        h = _merge(ya, yb, p, xf, wpa, wpb, wout, *ln1, tm=tm)
        idx, gate = _route(h, wq, sk1, sk2)
        return _peer(h, idx, gate, uv, *ln2)

    ns = dbatch * dseq
    xs = x_sample.reshape(ns, D_MODEL)
    p_s, kv_s = _in_proj(xs, w_in_p, tm=ns)
    ka_s, va_s, kb_s, vb_s = _split_kv(kv_s, dbatch, dseq)

    def with_cache(cache, new_cols, width):
        new = p_s[:, new_cols:new_cols + width].reshape(dbatch, dseq, width)
        return jnp.concatenate([cache.reshape(dbatch, -1, width).astype(BF16), new], axis=1)

    kcat_a = with_cache(cache_a_k, COL_KA, A_WIDTH)
    vcat_a = with_cache(cache_a_v, COL_VA, A_WIDTH)
    kcat_b = with_cache(cache_b_k, COL_KB, B_KV_WIDTH)
    vcat_b = with_cache(cache_b_v, COL_VB, B_KV_WIDTH)
    bias_s = _rel_bias_pairs(rel_bias_table, dseq, kcat_a.shape[1])
    ya_s, yb_s = _attn_sample(p_s, kcat_a, vcat_a, kcat_b, vcat_b, bias_s, sinks, dbatch, dseq)
    y_sample = tail(xs, ya_s, yb_s, p_s, ns).reshape(dbatch, dseq, D_MODEL)

    xp = x_prompt.reshape(batch * seq, D_MODEL)
    p, kv = _in_proj(xp, w_in_p, tm=1024)
    bias_p = _rel_bias_pairs(rel_bias_table, CHUNK, (A_PREV_CHUNKS + 1) * CHUNK)
    ya, yb = _attn_prompt(p, bias_p, sinks, batch, seq)
    y_prompt = tail(xp, ya, yb, p, 256).reshape(batch, seq, D_MODEL)
    ka, va, kb, vb = _split_kv(kv, batch, seq)
    la = min(A_PREV_CHUNKS * CHUNK, seq)
    lb = min(B_PREV_CHUNKS * CHUNK, seq)

    return (y_prompt, y_sample, ka[:, -la:], va[:, -la:], kb[:, -lb:], vb[:, -lb:], ka_s, va_s, kb_s, vb_s)
```

```python
import math

import jax
import jax.numpy as jnp
import numpy as np
from jax import lax
from jax.experimental import pallas as pl
from jax.experimental.pallas import tpu as pltpu

F32 = jnp.float32
BF16 = jnp.bfloat16

D_MODEL = 2048
CHUNK = 64
HEAD_DIM = 64
A_HEADS = 16
A_PREV_CHUNKS = 8
A_REL_CLIP = 128
B_Q_HEADS = 16
B_KV_HEADS = 2
B_PREV_CHUNKS = 2
A_WIDTH = A_HEADS * HEAD_DIM
B_WIDTH = B_Q_HEADS * HEAD_DIM
B_KV_WIDTH = B_KV_HEADS * HEAD_DIM
PEER_HEADS = 8
PEER_KEYS = 128
PEER_QDIM = 256
PEER_TOPK = 16
PEER_PICKS = PEER_HEADS * PEER_TOPK
LN_EPS = 1e-5
DEEPNORM_ALPHA = 2.0 ** 0.25

LANES = 128
SUBLANES = 8
HEAD_PAIR = 2 * HEAD_DIM
VMEM_LIMIT = 56 * 1024 * 1024

COL_GA = 0
COL_GB = COL_GA + D_MODEL
COL_QA = COL_GB + D_MODEL
COL_QB = COL_QA + A_WIDTH
COL_KA = COL_QB + B_WIDTH
COL_VA = COL_KA + A_WIDTH
COL_KB = COL_VA + A_WIDTH
COL_VB = COL_KB + B_KV_WIDTH
IN_COLS = COL_VB + B_KV_WIDTH
KV_COLS = IN_COLS - COL_KA
PROJ_TN = 768
ATT_BLOCK = A_PREV_CHUNKS * CHUNK

NT_DIMS = (((1,), (1,)), ((), ()))


def _compiler_params(semantics):
    return pltpu.CompilerParams(dimension_semantics=semantics, vmem_limit_bytes=VMEM_LIMIT)


def _const_spec(shape):
    zeros = (0,) * len(shape)
    return pl.BlockSpec(shape, lambda *_: zeros, pipeline_mode=pl.Buffered(1))


def _in_proj_kernel(x_ref, w_ref, p_ref, kv_ref, xb_ref):
    j = pl.program_id(1)

    @pl.when(j == 0)
    def _():
        xb_ref[...] = x_ref[...].astype(BF16)

    acc = jnp.dot(xb_ref[...], w_ref[...], preferred_element_type=F32)
    p_ref[...] = acc.astype(BF16)

    @pl.when(j >= COL_KA // PROJ_TN)
    def _():
        kv_ref[...] = acc


def _in_proj(x, w_bf16, tm):
    n = x.shape[0]
    first_kv = COL_KA // PROJ_TN
    return pl.pallas_call(
        _in_proj_kernel,
        grid=(n // tm, IN_COLS // PROJ_TN),
        in_specs=[
            pl.BlockSpec((tm, D_MODEL), lambda i, j: (i, 0)),
            pl.BlockSpec((D_MODEL, PROJ_TN), lambda i, j: (0, j)),
        ],
        out_specs=[
            pl.BlockSpec((tm, PROJ_TN), lambda i, j: (i, j)),
            pl.BlockSpec((tm, PROJ_TN), lambda i, j: (i, jnp.maximum(j - first_kv, 0))),
        ],
        out_shape=[
            jax.ShapeDtypeStruct((n, IN_COLS), BF16),
            jax.ShapeDtypeStruct((n, KV_COLS), F32),
        ],
        scratch_shapes=[pltpu.VMEM((tm, D_MODEL), BF16)],
        compiler_params=_compiler_params(("parallel", "arbitrary")),
    )(x, w_bf16)


def _softmax_pv(s, v2, sink):
    m = jnp.max(s, axis=-1, keepdims=True)
    if sink is not None:
        m = jnp.maximum(m, sink)
    p = jnp.exp(s - m)
    den = jnp.sum(p, axis=-1, keepdims=True)
    if sink is not None:
        den = den + jnp.exp(sink - m)
    o = jnp.dot(p.astype(BF16), v2, preferred_element_type=F32)
    return o / den


def _scores(qq, k2, bias, key_thresh):
    s = lax.dot_general(qq, k2, NT_DIMS, preferred_element_type=F32) * (HEAD_DIM ** -0.5) + bias
    if key_thresh is not None:
        col = lax.broadcasted_iota(jnp.int32, s.shape, 1)
        s = jnp.where(col >= key_thresh, s, -jnp.inf)
    return s


def _mixer_a(q_of, k_of, v_of, bias_of, key_thresh, store):
    for j in range(A_HEADS // 2):
        q2 = q_of(j)
        tq = q2.shape[0]
        lane_hi = lax.broadcasted_iota(jnp.int32, q2.shape, 1) >= HEAD_DIM
        zero = jnp.zeros_like(q2)
        qq = jnp.concatenate([jnp.where(lane_hi, zero, q2), jnp.where(lane_hi, q2, zero)], axis=0)
        o = _softmax_pv(_scores(qq, k_of(j), bias_of(j), key_thresh), v_of(j), None)
        store(j, jnp.where(lane_hi, o[tq:2 * tq], o[0:tq]))


def _alibi_rows(absdist):
    return jnp.concatenate([(-(2.0 ** (-8.0 * (h + 1) / B_Q_HEADS))) * absdist for h in range(B_Q_HEADS)], axis=0)


def _sink_rows(sink_ref, tq):
    return jnp.concatenate([jnp.full((tq, 1), sink_ref[h], F32) for h in range(B_Q_HEADS)], axis=0)


def _mixer_b(q_of, k2, v2, bias_rows, sink_rows, key_thresh, store):
    group = B_Q_HEADS // B_KV_HEADS
    pieces = []
    for j in range(B_Q_HEADS // 2):
        q2 = q_of(j)
        tq = q2.shape[0]
        lane_hi = lax.broadcasted_iota(jnp.int32, q2.shape, 1) >= HEAD_DIM
        zero = jnp.zeros_like(q2)
        q2_swapped = pltpu.roll(q2.astype(F32), HEAD_DIM, 1).astype(BF16)
        for hh in range(2):
            kv = (2 * j + hh) // group
            src = q2 if hh == kv else q2_swapped
            pieces.append(jnp.where(lane_hi, src, zero) if kv else jnp.where(lane_hi, zero, src))
    qq = jnp.concatenate(pieces, axis=0)
    o = _softmax_pv(_scores(qq, k2, bias_rows, key_thresh), v2, sink_rows)
    lane_hi = lax.broadcasted_iota(jnp.int32, (tq, HEAD_PAIR), 1) >= HEAD_DIM
    for j in range(B_Q_HEADS // 2):
        halves = []
        for hh in range(2):
            h = 2 * j + hh
            o_h = o[h * tq:(h + 1) * tq]
            halves.append(o_h if hh == h // group else pltpu.roll(o_h, HEAD_DIM, 1))
        store(j, jnp.where(lane_hi, halves[1], halves[0]))


def _attn_prompt_kernel(sink_ref, qa_ref, kap_ref, kac_ref, vap_ref, vac_ref,
                        qb_ref, kbp_ref, kbc_ref, vbp_ref, vbc_ref, bias_ref,
                        ya_ref, yb_ref, ka_s, va_s, kb_s, vb_s, bias_b_s, sink_s):
    i = pl.program_id(1)
    blk = ATT_BLOCK
    ka_s[0:blk, :] = kap_ref[...]
    ka_s[blk:2 * blk, :] = kac_ref[...]
    va_s[0:blk, :] = vap_ref[...]
    va_s[blk:2 * blk, :] = vac_ref[...]
    kb_s[0:blk, :] = kbp_ref[...]
    kb_s[blk:2 * blk, :] = kbc_ref[...]
    vb_s[0:blk, :] = vbp_ref[...]
    vb_s[blk:2 * blk, :] = vbc_ref[...]

    tk_a = (A_PREV_CHUNKS + 1) * CHUNK
    tk_b = (B_PREV_CHUNKS + 1) * CHUNK
    b_off = blk - B_PREV_CHUNKS * CHUNK
    tq = lax.broadcasted_iota(jnp.int32, (CHUNK, tk_b), 0)
    ts = lax.broadcasted_iota(jnp.int32, (CHUNK, tk_b), 1)
    bias_b_s[...] = _alibi_rows(jnp.abs(tq + B_PREV_CHUNKS * CHUNK - ts).astype(F32))
    sink_s[...] = _sink_rows(sink_ref, CHUNK)

    def chunk_body(c, carry):
        row0 = pl.multiple_of(c * CHUNK, CHUNK)
        thresh_a = jnp.where(i > 0, 0, blk - c * CHUNK)
        thresh_b = jnp.where(i > 0, 0, B_PREV_CHUNKS * CHUNK - c * CHUNK)

        def store_a(j, val):
            ya_ref[pl.ds(row0, CHUNK), j * HEAD_PAIR:(j + 1) * HEAD_PAIR] = val.astype(ya_ref.dtype)

        def store_b(j, val):
            yb_ref[pl.ds(row0, CHUNK), j * HEAD_PAIR:(j + 1) * HEAD_PAIR] = val.astype(yb_ref.dtype)

        _mixer_a(
            lambda j: qa_ref[pl.ds(row0, CHUNK), j * HEAD_PAIR:(j + 1) * HEAD_PAIR],
            lambda j: ka_s[pl.ds(row0, tk_a), j * HEAD_PAIR:(j + 1) * HEAD_PAIR],
            lambda j: va_s[pl.ds(row0, tk_a), j * HEAD_PAIR:(j + 1) * HEAD_PAIR],
            lambda j: bias_ref[j], thresh_a, store_a)

        rowb = pl.multiple_of(b_off + c * CHUNK, CHUNK)
        _mixer_b(
            lambda j: qb_ref[pl.ds(row0, CHUNK), j * HEAD_PAIR:(j + 1) * HEAD_PAIR],
            kb_s[pl.ds(rowb, tk_b), :], vb_s[pl.ds(rowb, tk_b), :],
            bias_b_s[...], sink_s[...], thresh_b, store_b)
        return carry

    lax.fori_loop(0, blk // CHUNK, chunk_body, 0)


def _attn_prompt(p, bias_a, sinks, batch, seq):
    blk = ATT_BLOCK
    nb = seq // blk
    n = batch * seq

    def cur(col_block):
        return lambda b, i: (b * nb + i, col_block)

    def prev(col_block):
        return lambda b, i: (b * nb + jnp.maximum(i - 1, 0), col_block)

    wa, wb = A_WIDTH, B_KV_WIDTH
    in_specs = [
        pl.BlockSpec(memory_space=pltpu.SMEM),
        pl.BlockSpec((blk, wa), cur(COL_QA // wa)),
        pl.BlockSpec((blk, wa), prev(COL_KA // wa)),
        pl.BlockSpec((blk, wa), cur(COL_KA // wa)),
        pl.BlockSpec((blk, wa), prev(COL_VA // wa)),
        pl.BlockSpec((blk, wa), cur(COL_VA // wa)),
        pl.BlockSpec((blk, wa), cur(COL_QB // wa)),
        pl.BlockSpec((blk, wb), prev(COL_KB // wb)),
        pl.BlockSpec((blk, wb), cur(COL_KB // wb)),
        pl.BlockSpec((blk, wb), prev(COL_VB // wb)),
        pl.BlockSpec((blk, wb), cur(COL_VB // wb)),
        _const_spec(bias_a.shape),
    ]
    out_spec = pl.BlockSpec((blk, wa), lambda b, i: (b * nb + i, 0))
    rows_b = B_Q_HEADS * CHUNK
    return pl.pallas_call(
        _attn_prompt_kernel,
        grid=(batch, nb),
        in_specs=in_specs,
        out_specs=[out_spec, out_spec],
        out_shape=[jax.ShapeDtypeStruct((n, A_WIDTH), BF16), jax.ShapeDtypeStruct((n, B_WIDTH), BF16)],
        scratch_shapes=[
            pltpu.VMEM((2 * blk, wa), BF16), pltpu.VMEM((2 * blk, wa), BF16),
            pltpu.VMEM((2 * blk, wb), BF16), pltpu.VMEM((2 * blk, wb), BF16),
            pltpu.VMEM((rows_b, (B_PREV_CHUNKS + 1) * CHUNK), F32), pltpu.VMEM((rows_b, 1), F32),
        ],
        compiler_params=_compiler_params(("parallel", "arbitrary")),
    )(sinks, p, p, p, p, p, p, p, p, p, p, bias_a)


def _attn_sample_kernel(sink_ref, qa_ref, ka_ref, va_ref, qb_ref, kb_ref, vb_ref, bias_ref, ya_ref, yb_ref):
    t, tk_b = qb_ref.shape[0], kb_ref.shape[1]

    def store_a(j, val):
        ya_ref[:, j * HEAD_PAIR:(j + 1) * HEAD_PAIR] = val.astype(ya_ref.dtype)

    def store_b(j, val):
        yb_ref[:, j * HEAD_PAIR:(j + 1) * HEAD_PAIR] = val.astype(yb_ref.dtype)

    _mixer_a(
        lambda j: qa_ref[:, j * HEAD_PAIR:(j + 1) * HEAD_PAIR],
        lambda j: ka_ref[0, :, j * HEAD_PAIR:(j + 1) * HEAD_PAIR],
        lambda j: va_ref[0, :, j * HEAD_PAIR:(j + 1) * HEAD_PAIR],
        lambda j: bias_ref[j], None, store_a)

    tq = lax.broadcasted_iota(jnp.int32, (t, tk_b), 0)
    ts = lax.broadcasted_iota(jnp.int32, (t, tk_b), 1)
    bias_rows = _alibi_rows(jnp.abs(tq + (tk_b - t) - ts).astype(F32))
    _mixer_b(
        lambda j: qb_ref[:, j * HEAD_PAIR:(j + 1) * HEAD_PAIR],
        kb_ref[0], vb_ref[0], bias_rows, _sink_rows(sink_ref, t), None, store_b)


def _attn_sample(p, kcat_a, vcat_a, kcat_b, vcat_b, bias_a, sinks, batch, t):
    n = batch * t
    wa, wb = A_WIDTH, B_KV_WIDTH
    tk_a, tk_b = kcat_a.shape[1], kcat_b.shape[1]
    in_specs = [
        pl.BlockSpec(memory_space=pltpu.SMEM),
        pl.BlockSpec((t, wa), lambda b: (b, COL_QA // wa)),
        pl.BlockSpec((1, tk_a, wa), lambda b: (b, 0, 0)),
        pl.BlockSpec((1, tk_a, wa), lambda b: (b, 0, 0)),
        pl.BlockSpec((t, wa), lambda b: (b, COL_QB // wa)),
        pl.BlockSpec((1, tk_b, wb), lambda b: (b, 0, 0)),
        pl.BlockSpec((1, tk_b, wb), lambda b: (b, 0, 0)),
        _const_spec(bias_a.shape),
    ]
    out_spec = pl.BlockSpec((t, wa), lambda b: (b, 0))
    return pl.pallas_call(
        _attn_sample_kernel,
        grid=(batch,),
        in_specs=in_specs,
        out_specs=[out_spec, out_spec],
        out_shape=[jax.ShapeDtypeStruct((n, A_WIDTH), BF16), jax.ShapeDtypeStruct((n, B_WIDTH), BF16)],
        compiler_params=_compiler_params(("parallel",)),
    )(sinks, p, kcat_a, vcat_a, p, kcat_b, vcat_b, bias_a)


def _layer_norm(x, g, b):
    mu = jnp.mean(x, axis=-1, keepdims=True)
    xc = x - mu
    var = jnp.mean(xc * xc, axis=-1, keepdims=True)
    return xc * lax.rsqrt(var + LN_EPS) * g + b


def _merge_kernel(ya_ref, yb_ref, ga_ref, gb_ref, x_ref, wpa_ref, wpb_ref, wout_ref, g_ref, b_ref, h_ref):
    za = jnp.dot(ya_ref[...], wpa_ref[...], preferred_element_type=F32)
    zb = jnp.dot(yb_ref[...], wpb_ref[...], preferred_element_type=F32)
    m = jax.nn.sigmoid(ga_ref[...].astype(F32)) * za + jax.nn.sigmoid(gb_ref[...].astype(F32)) * zb
    r = jnp.dot(m.astype(BF16), wout_ref[...], preferred_element_type=F32)
    h_ref[...] = _layer_norm(DEEPNORM_ALPHA * x_ref[...] + r, g_ref[...], b_ref[...])


def _merge(ya, yb, p, x, wpa, wpb, wout, ln_g, ln_b, tm):
    n = x.shape[0]
    row = lambda i: (i, 0)
    return pl.pallas_call(
        _merge_kernel,
        grid=(n // tm,),
        in_specs=[
            pl.BlockSpec((tm, A_WIDTH), row),
            pl.BlockSpec((tm, B_WIDTH), row),
            pl.BlockSpec((tm, D_MODEL), lambda i: (i, COL_GA // D_MODEL)),
            pl.BlockSpec((tm, D_MODEL), lambda i: (i, COL_GB // D_MODEL)),
            pl.BlockSpec((tm, D_MODEL), row),
            _const_spec(wpa.shape), _const_spec(wpb.shape), _const_spec(wout.shape),
            _const_spec(ln_g.shape), _const_spec(ln_b.shape),
        ],
        out_specs=pl.BlockSpec((tm, D_MODEL), row),
        out_shape=jax.ShapeDtypeStruct((n, D_MODEL), F32),
        compiler_params=_compiler_params(("parallel",)),
    )(ya, yb, p, p, x, wpa, wpb, wout, ln_g, ln_b)


ROUTE_TM = 128
ID_SENTINEL = 1 << 30


def _topk_rows(s, ids, k):
    vals, picked = [], []
    for _ in range(k):
        m = jnp.max(s, axis=0, keepdims=True)
        ix = jnp.min(jnp.where(s == m, ids, ID_SENTINEL), axis=0, keepdims=True)
        s = jnp.where(ids == ix, -jnp.inf, s)
        vals.append(m)
        picked.append(ix)
    return jnp.concatenate(vals, axis=0), jnp.concatenate(picked, axis=0)


def _candidate_blocks(v1, v2):
    t = v1.shape[1]
    sub = lax.broadcasted_iota(jnp.int32, (SUBLANES, t), 0)
    neg = jnp.full((SUBLANES, t), -jnp.inf, F32)
    k = PEER_TOPK
    blocks = []
    blocks.append((v1[0:1] + v2[0:8], sub))
    blocks.append((v1[0:1] + v2[8:16], sub + 8))
    blocks.append((v1[1:2] + v2[0:8], sub + k))
    blocks.append((jnp.where(sub < 5, v1[2:3] + v2[0:8], neg), sub + 2 * k))
    blocks.append((jnp.where(sub < 4, v1[3:4] + v2[0:8], neg), sub + 3 * k))
    blocks.append((v1[8:16] + v2[0:1], (sub + 8) * k))
    a47 = jnp.concatenate([v1[4:8], v1[4:8]], axis=0)
    blocks.append((a47 + jnp.where(sub < 4, v2[0:1], v2[1:2]),
                   (4 + lax.bitwise_and(sub, 3)) * k + lax.shift_right_logical(sub, 2)))
    blocks.append((jnp.where(sub == 0, v1[4:5] + v2[2:3], neg), sub + 4 * k + 2))
    vals = jnp.concatenate([b[0] for b in blocks], axis=0)
    ids = jnp.concatenate([b[1] for b in blocks], axis=0)
    return vals, ids


def _route_kernel(h_ref, wq_ref, sk1_ref, sk2_ref, idx_ref, gate_ref):
    q = jnp.dot(h_ref[...].astype(BF16), wq_ref[...], preferred_element_type=F32).astype(BF16)
    half = PEER_QDIM // 2
    key_ids = lax.broadcasted_iota(jnp.int32, (PEER_KEYS, h_ref.shape[0]), 0)
    experts, gates = [], []
    for hd in range(PEER_HEADS):
        q1 = q[:, hd * PEER_QDIM:hd * PEER_QDIM + half]
        q2 = q[:, hd * PEER_QDIM + half:(hd + 1) * PEER_QDIM]
        s1 = lax.dot_general(sk1_ref[...], q1, NT_DIMS, preferred_element_type=F32)
        s2 = lax.dot_general(sk2_ref[...], q2, NT_DIMS, preferred_element_type=F32)
        v1, i1 = _topk_rows(s1, key_ids, PEER_TOPK)
        v2, i2 = _topk_rows(s2, key_ids, PEER_TOPK)
        sc, ci = _topk_rows(*_candidate_blocks(v1, v2), PEER_TOPK)
        ca = lax.shift_right_logical(ci, int(math.log2(PEER_TOPK)))
        cb = lax.bitwise_and(ci, PEER_TOPK - 1)
        e1 = jnp.zeros_like(ci)
        e2 = jnp.zeros_like(ci)
        for a in range(PEER_TOPK):
            e1 = jnp.where(ca == a, i1[a:a + 1], e1)
            e2 = jnp.where(cb == a, i2[a:a + 1], e2)
        experts.append(e1 * PEER_KEYS + e2)
        e = jnp.exp(sc - sc[0:1])
        gates.append(e / jnp.sum(e, axis=0, keepdims=True))
    idx_ref[...] = jnp.concatenate(experts, axis=0).astype(F32).T.astype(jnp.int32)
    gate_ref[...] = jnp.concatenate(gates, axis=0).T


def _route(h, wq, sk1, sk2):
    n = h.shape[0]
    tm = ROUTE_TM
    row = lambda i: (i, 0)
    return pl.pallas_call(
        _route_kernel,
        grid=(n // tm,),
        in_specs=[pl.BlockSpec((tm, D_MODEL), row), _const_spec(wq.shape),
                  _const_spec(sk1.shape), _const_spec(sk2.shape)],
        out_specs=[pl.BlockSpec((tm, PEER_PICKS), row), pl.BlockSpec((tm, PEER_PICKS), row)],
        out_shape=[jax.ShapeDtypeStruct((n, PEER_PICKS), jnp.int32),
                   jax.ShapeDtypeStruct((n, PEER_PICKS), F32)],
        compiler_params=_compiler_params(("parallel",)),
    )(h, wq, sk1, sk2)


PEER_TB = 32
PEER_ROUND = 8
PEER_SLOTS = 2 * PEER_ROUND
PEER_LOOKAHEAD = PEER_ROUND
ROW_TILES = D_MODEL // LANES
HALF_TILES = ROW_TILES // 2

def _gelu_exact(x):
    return 0.5 * x * (1.0 + lax.erf(x * (2.0 ** -0.5)))


def _peer_kernel(idx_ref, gate_ref, h_ref, uv_ref, g_ref, b_ref, y_ref, *scratch):
    bufs, (sem, ffn, pre, wgt) = scratch[:PEER_SLOTS], scratch[PEER_SLOTS:]
    i = pl.program_id(0)
    last = pl.num_programs(0) - 1
    tb = h_ref.shape[0]

    def issue(t, slot, part=None):
        per_part = PEER_PICKS // ROW_TILES
        picks = range(PEER_PICKS) if part is None else range(part * per_part, (part + 1) * per_part)
        for k in picks:
            dst = bufs[slot].at[:, k // SUBLANES, pl.ds(k % SUBLANES, 1), :]
            pltpu.make_async_copy(uv_ref.at[idx_ref[0, t, k]], dst, sem.at[slot]).start(priority=k % 2)

    def wait_slot(slot):
        pltpu.make_async_copy(bufs[slot], bufs[slot], sem.at[slot]).wait()

    def picks_tile(slot, c):
        return bufs[slot][c].reshape(PEER_PICKS, LANES)

    @pl.when(i == 0)
    def _():
        for t in range(PEER_LOOKAHEAD):
            issue(t, t)

    eye = (lax.broadcasted_iota(jnp.int32, (PEER_PICKS, PEER_PICKS), 0)
           == lax.broadcasted_iota(jnp.int32, (PEER_PICKS, PEER_PICKS), 1))

    def halves(words):
        lo = lax.bitcast_convert_type(lax.shift_left(words, jnp.int32(16)), F32)
        hi = lax.bitcast_convert_type(lax.bitwise_and(words, jnp.int32(-(1 << 16))), F32)
        return lo, hi

    def pre_activation(t, slot, between):
        h_row = h_ref[pl.ds(t, 1), :]
        acc = None
        for c in range(HALF_TILES):
            between(c)
            lo, hi = halves(picks_tile(slot, c))
            term = (lo * h_row[:, c * LANES:(c + 1) * LANES]
                    + hi * h_row[:, (HALF_TILES + c) * LANES:(HALF_TILES + c + 1) * LANES])
            acc = term if acc is None else acc + term
        a_col = jnp.sum(acc, axis=1, keepdims=True)
        return jnp.sum(jnp.where(eye, a_col, 0.0), axis=0, keepdims=True)

    def weighted_sum(t, slot, w_row, between):
        w_col = jnp.sum(jnp.where(eye, w_row, 0.0), axis=1, keepdims=True)
        out_lo, out_hi = [], []
        for c in range(HALF_TILES):
            between(HALF_TILES + c)
            lo, hi = halves(picks_tile(slot, HALF_TILES + c))
            out_lo.append(jnp.sum(w_col * lo, axis=0, keepdims=True))
            out_hi.append(jnp.sum(w_col * hi, axis=0, keepdims=True))
        ffn[pl.ds(t, 1), :] = jnp.concatenate(out_lo + out_hi, axis=1)

    def round_pair(rr, carry):
        for half in range(2):
            t0 = pl.multiple_of((2 * rr + half) * PEER_ROUND, PEER_ROUND)
            here, there = half * PEER_ROUND, (1 - half) * PEER_ROUND
            def issue_ahead(j):
                return lambda part: issue(t0 + PEER_ROUND + j, there + j, part)

            for j in range(PEER_ROUND):
                wait_slot(here + j)
                pre[j:j + 1, :] = pre_activation(t0 + j, here + j, issue_ahead(j))
            wgt[...] = gate_ref[pl.ds(t0, PEER_ROUND), :] * _gelu_exact(pre[...])
            for j in range(PEER_ROUND):
                weighted_sum(t0 + j, here + j, wgt[j:j + 1, :], issue_ahead(j))
        return carry

    lax.fori_loop(0, tb // PEER_SLOTS, round_pair, 0)

    @pl.when(i == last)
    def _():
        for t in range(PEER_LOOKAHEAD):
            wait_slot(t)

    y_ref[...] = _layer_norm(DEEPNORM_ALPHA * h_ref[...] + ffn[...], g_ref[...], b_ref[...])


def _peer(h, idx, gate, uv, ln_g, ln_b):
    n = h.shape[0]
    tb = min(PEER_TB, n)
    assert n % tb == 0 and tb % PEER_SLOTS == 0
    nb = n // tb
    row = lambda i: (i, 0)
    idx_pad = jnp.concatenate([idx, jnp.zeros((tb, PEER_PICKS), idx.dtype)], axis=0)
    idx_next = idx_pad[tb:].reshape(nb, tb, PEER_PICKS)[:, :PEER_LOOKAHEAD]
    idx_ext = jnp.concatenate([idx.reshape(nb, tb, PEER_PICKS), idx_next], axis=1)
    return pl.pallas_call(
        _peer_kernel,
        grid=(nb,),
        in_specs=[
            pl.BlockSpec((1, tb + PEER_LOOKAHEAD, PEER_PICKS), lambda i: (i, 0, 0), memory_space=pltpu.SMEM),
            pl.BlockSpec((tb, PEER_PICKS), row),
            pl.BlockSpec((tb, D_MODEL), row),
            pl.BlockSpec(memory_space=pl.ANY),
            _const_spec(ln_g.shape), _const_spec(ln_b.shape),
        ],
        out_specs=pl.BlockSpec((tb, D_MODEL), row),
        out_shape=jax.ShapeDtypeStruct((n, D_MODEL), F32),
        scratch_shapes=(
            [pltpu.VMEM((ROW_TILES, PEER_PICKS // SUBLANES, SUBLANES, LANES), jnp.int32) for _ in range(PEER_SLOTS)]
            + [pltpu.SemaphoreType.DMA((PEER_SLOTS,)), pltpu.VMEM((tb, D_MODEL), F32),
               pltpu.VMEM((PEER_ROUND, PEER_PICKS), F32), pltpu.VMEM((PEER_ROUND, PEER_PICKS), F32)]
        ),
        compiler_params=_compiler_params(("arbitrary",)),
    )(idx_ext, gate, h, uv, ln_g, ln_b)


def _rel_bias_pairs(rel_table, t, tk):
    dist = (tk - 1) - jnp.arange(tk + t - 1)
    line = jnp.take(rel_table, jnp.clip(dist, -A_REL_CLIP, A_REL_CLIP) + A_REL_CLIP, axis=1).astype(F32)
    bias = jnp.stack([line[:, t - 1 - q:t - 1 - q + tk] for q in range(t)], axis=1)
    return bias.reshape(A_HEADS // 2, 2 * t, tk)


PACK_ROWS = 256


def _pack_kernel(u_ref, v_ref, out_ref, stage, sem):
    i = pl.program_id(0)
    buf = lax.rem(i, 2)
    rows = u_ref.shape[0]
    half = D_MODEL // 2

    def drain(b):
        pltpu.make_async_copy(stage.at[b], stage.at[b], sem.at[b]).wait()

    @pl.when(i >= 2)
    def _():
        drain(buf)

    def words(x_ref):
        def bf16_bits(x):
            return lax.bitcast_convert_type(x.astype(BF16).astype(F32), jnp.int32)
        lo = lax.shift_right_logical(bf16_bits(x_ref[:, 0:half]), jnp.int32(16))
        hi = lax.bitwise_and(bf16_bits(x_ref[:, half:D_MODEL]), jnp.int32(-(1 << 16)))
        return lax.bitwise_or(hi, lo)

    for part, x_ref in enumerate((u_ref, v_ref)):
        w = words(x_ref)
        for c in range(HALF_TILES):
            tile = w[:, c * LANES:(c + 1) * LANES].reshape(rows // SUBLANES, SUBLANES, LANES)
            stage[buf, part * HALF_TILES + c] = tile

    def row_copy(r, carry):
        for s in range(SUBLANES):
            src = stage.at[buf, :, r, pl.ds(s, 1), :]
            pltpu.make_async_copy(src, out_ref.at[i * rows + r * SUBLANES + s], sem.at[buf]).start()
        return carry

    lax.fori_loop(0, rows // SUBLANES, row_copy, 0)

    @pl.when(i == pl.num_programs(0) - 1)
    def _():
        drain(buf)

        @pl.when(i >= 1)
        def _():
            drain(1 - buf)


def _pack_expert_tables(u, v):
    n = u.shape[0]
    rows = PACK_ROWS
    assert n % rows == 0
    return pl.pallas_call(
        _pack_kernel,
        grid=(n // rows,),
        in_specs=[pl.BlockSpec((rows, D_MODEL), lambda i: (i, 0)), pl.BlockSpec((rows, D_MODEL), lambda i: (i, 0))],
        out_specs=pl.BlockSpec(memory_space=pl.ANY),
        out_shape=jax.ShapeDtypeStruct((n, ROW_TILES, 1, LANES), jnp.int32),
        scratch_shapes=[pltpu.VMEM((2, ROW_TILES, rows // SUBLANES, SUBLANES, LANES), jnp.int32),
                        pltpu.SemaphoreType.DMA((2,))],
        compiler_params=_compiler_params(("arbitrary",)),
    )(u, v)


def _permute_in_proj(w_in):
    offs = np.cumsum([0, A_WIDTH, A_WIDTH, A_WIDTH, B_WIDTH, B_KV_WIDTH, B_KV_WIDTH, D_MODEL, D_MODEL])
    qa, ka, va, qb, kb, vb, ga, gb = [w_in[:, offs[s]:offs[s + 1]] for s in range(8)]
    return jnp.concatenate([ga, gb, qa, qb, ka, va, kb, vb], axis=1).astype(BF16)


def _split_kv(kv, batch, t):
    ka = kv[:, 0:A_WIDTH].reshape(batch, t, A_HEADS, HEAD_DIM)
    va = kv[:, A_WIDTH:2 * A_WIDTH].reshape(batch, t, A_HEADS, HEAD_DIM)
    kb = kv[:, 2 * A_WIDTH:2 * A_WIDTH + B_KV_WIDTH].reshape(batch, t, B_KV_HEADS, HEAD_DIM)
    vb = kv[:, 2 * A_WIDTH + B_KV_WIDTH:].reshape(batch, t, B_KV_HEADS, HEAD_DIM)
    return ka, va, kb, vb


def kernel(x_prompt, x_sample, cache_a_k, cache_a_v, cache_b_k, cache_b_v, w_in, rel_bias_table, b_sinks,
           w_proj_a, w_proj_b, w_out, ln1_g, ln1_b, peer_w_query, peer_sub_keys_1, peer_sub_keys_2,
           peer_u, peer_v, ln2_g, ln2_b):
    batch, seq, _ = x_prompt.shape
    dbatch, dseq, _ = x_sample.shape
    assert seq % ATT_BLOCK == 0 and seq >= ATT_BLOCK

    w_in_p = _permute_in_proj(w_in)
    wpa, wpb, wout = w_proj_a.astype(BF16), w_proj_b.astype(BF16), w_out.astype(BF16)
    wq = peer_w_query.astype(BF16)
    sk1, sk2 = peer_sub_keys_1.astype(BF16), peer_sub_keys_2.astype(BF16)
    uv = _pack_expert_tables(peer_u, peer_v)
    ln1 = (ln1_g.reshape(1, D_MODEL), ln1_b.reshape(1, D_MODEL))
    ln2 = (ln2_g.reshape(1, D_MODEL), ln2_b.reshape(1, D_MODEL))
    sinks = b_sinks.astype(F32)

    def tail(xf, ya, yb, p, tm):
        h = _merge(ya, yb, p, xf, wpa, wpb, wout, *ln1, tm=tm)
        idx, gate = _route(h, wq, sk1, sk2)
        return _peer(h, idx, gate, uv, *ln2)

    ns = dbatch * dseq
    xs = x_sample.reshape(ns, D_MODEL)
    p_s, kv_s = _in_proj(xs, w_in_p, tm=ns)
    ka_s, va_s, kb_s, vb_s = _split_kv(kv_s, dbatch, dseq)

    def with_cache(cache, new_cols, width):
        new = p_s[:, new_cols:new_cols + width].reshape(dbatch, dseq, width)
        return jnp.concatenate([cache.reshape(dbatch, -1, width).astype(BF16), new], axis=1)

    kcat_a = with_cache(cache_a_k, COL_KA, A_WIDTH)
    vcat_a = with_cache(cache_a_v, COL_VA, A_WIDTH)
    kcat_b = with_cache(cache_b_k, COL_KB, B_KV_WIDTH)
    vcat_b = with_cache(cache_b_v, COL_VB, B_KV_WIDTH)
    bias_s = _rel_bias_pairs(rel_bias_table, dseq, kcat_a.shape[1])
    ya_s, yb_s = _attn_sample(p_s, kcat_a, vcat_a, kcat_b, vcat_b, bias_s, sinks, dbatch, dseq)
    y_sample = tail(xs, ya_s, yb_s, p_s, ns).reshape(dbatch, dseq, D_MODEL)

    xp = x_prompt.reshape(batch * seq, D_MODEL)
    p, kv = _in_proj(xp, w_in_p, tm=1024)
    bias_p = _rel_bias_pairs(rel_bias_table, CHUNK, (A_PREV_CHUNKS + 1) * CHUNK)
    ya, yb = _attn_prompt(p, bias_p, sinks, batch, seq)
    y_prompt = tail(xp, ya, yb, p, 256).reshape(batch, seq, D_MODEL)
    ka, va, kb, vb = _split_kv(kv, batch, seq)
    la = min(A_PREV_CHUNKS * CHUNK, seq)
    lb = min(B_PREV_CHUNKS * CHUNK, seq)

    return (y_prompt, y_sample, ka[:, -la:], va[:, -la:], kb[:, -lb:], vb[:, -lb:], ka_s, va_s, kb_s, vb_s)
```

```python
import math

import jax
import jax.numpy as jnp
import numpy as np
from jax import lax
from jax.experimental import pallas as pl
from jax.experimental.pallas import tpu as pltpu

F32 = jnp.float32
BF16 = jnp.bfloat16

D_MODEL = 2048
CHUNK = 64
HEAD_DIM = 64
A_HEADS = 16
A_PREV_CHUNKS = 8
A_REL_CLIP = 128
B_Q_HEADS = 16
B_KV_HEADS = 2
B_PREV_CHUNKS = 2
A_WIDTH = A_HEADS * HEAD_DIM
B_WIDTH = B_Q_HEADS * HEAD_DIM
B_KV_WIDTH = B_KV_HEADS * HEAD_DIM
PEER_HEADS = 8
PEER_KEYS = 128
PEER_QDIM = 256
PEER_TOPK = 16
PEER_PICKS = PEER_HEADS * PEER_TOPK
LN_EPS = 1e-5
DEEPNORM_ALPHA = 2.0 ** 0.25

LANES = 128
SUBLANES = 8
HEAD_PAIR = 2 * HEAD_DIM
VMEM_LIMIT = 56 * 1024 * 1024

COL_GA = 0
COL_GB = COL_GA + D_MODEL
COL_QA = COL_GB + D_MODEL
COL_QB = COL_QA + A_WIDTH
COL_KA = COL_QB + B_WIDTH
COL_VA = COL_KA + A_WIDTH
COL_KB = COL_VA + A_WIDTH
COL_VB = COL_KB + B_KV_WIDTH
IN_COLS = COL_VB + B_KV_WIDTH
KV_COLS = IN_COLS - COL_KA
PROJ_TN = 768
ATT_BLOCK = A_PREV_CHUNKS * CHUNK

NT_DIMS = (((1,), (1,)), ((), ()))
TN_DIMS = (((0,), (0,)), ((), ()))


def _compiler_params(semantics):
    return pltpu.CompilerParams(dimension_semantics=semantics, vmem_limit_bytes=VMEM_LIMIT)


def _const_spec(shape):
    zeros = (0,) * len(shape)
    return pl.BlockSpec(shape, lambda *_: zeros, pipeline_mode=pl.Buffered(1))


def _in_proj_kernel(x_ref, w_ref, p_ref, kv_ref, xb_ref):
    j = pl.program_id(1)

    @pl.when(j == 0)
    def _():
        xb_ref[...] = x_ref[...].astype(BF16)

    acc = jnp.dot(xb_ref[...], w_ref[...], preferred_element_type=F32)
    p_ref[...] = acc.astype(BF16)

    @pl.when(j >= COL_KA // PROJ_TN)
    def _():
        kv_ref[...] = acc


def _in_proj(x, w_bf16, tm):
    n = x.shape[0]
    first_kv = COL_KA // PROJ_TN
    return pl.pallas_call(
        _in_proj_kernel,
        grid=(n // tm, IN_COLS // PROJ_TN),
        in_specs=[
            pl.BlockSpec((tm, D_MODEL), lambda i, j: (i, 0)),
            pl.BlockSpec((D_MODEL, PROJ_TN), lambda i, j: (0, j)),
        ],
        out_specs=[
            pl.BlockSpec((tm, PROJ_TN), lambda i, j: (i, j)),
            pl.BlockSpec((tm, PROJ_TN), lambda i, j: (i, jnp.maximum(j - first_kv, 0))),
        ],
        out_shape=[
            jax.ShapeDtypeStruct((n, IN_COLS), BF16),
            jax.ShapeDtypeStruct((n, KV_COLS), F32),
        ],
        scratch_shapes=[pltpu.VMEM((tm, D_MODEL), BF16)],
        compiler_params=_compiler_params(("parallel", "arbitrary")),
    )(x, w_bf16)


def _softmax_pv(s, v2, sink):
    m = jnp.max(s, axis=0, keepdims=True)
    if sink is not None:
        m = jnp.maximum(m, sink)
    p = jnp.exp(s - m)
    den = jnp.sum(p, axis=0, keepdims=True)
    if sink is not None:
        den = den + jnp.exp(sink - m)
    p = (p * (1.0 / den)).astype(BF16)
    return lax.dot_general(p, v2, TN_DIMS, preferred_element_type=F32)


def _scores(qq, k2, bias, key_thresh):
    s = lax.dot_general(k2, qq, NT_DIMS, preferred_element_type=F32) * (HEAD_DIM ** -0.5) + bias
    if key_thresh is not None:
        key = lax.broadcasted_iota(jnp.int32, s.shape, 0)
        s = jnp.where(key >= key_thresh, s, -jnp.inf)
    return s


def _mixer_a(q_of, k_of, v_of, bias_of, key_thresh, store):
    for j in range(A_HEADS // 2):
        q2 = q_of(j)
        tq = q2.shape[0]
        lane_hi = lax.broadcasted_iota(jnp.int32, q2.shape, 1) >= HEAD_DIM
        zero = jnp.zeros_like(q2)
        qq = jnp.concatenate([jnp.where(lane_hi, zero, q2), jnp.where(lane_hi, q2, zero)], axis=0)
        o = _softmax_pv(_scores(qq, k_of(j), bias_of(j), key_thresh), v_of(j), None)
        store(j, jnp.where(lane_hi, o[tq:2 * tq], o[0:tq]))


def _alibi_bias(tq, tk):
    shape = (tk, B_Q_HEADS * tq)
    key = lax.broadcasted_iota(jnp.int32, shape, 0)
    lane = lax.broadcasted_iota(jnp.int32, shape, 1)
    head = lax.shift_right_logical(lane, int(math.log2(tq)))
    q = lax.bitwise_and(lane, tq - 1)
    slope = jnp.exp((head + 1).astype(F32) * (-8.0 / B_Q_HEADS * math.log(2.0)))
    return -slope * jnp.abs(q + (tk - tq) - key).astype(F32)


def _sink_row(sink_ref, tq):
    lane = lax.broadcasted_iota(jnp.int32, (1, B_Q_HEADS * tq), 1)
    head = lax.shift_right_logical(lane, int(math.log2(tq)))
    row = jnp.zeros(lane.shape, F32)
    for h in range(B_Q_HEADS):
        row = jnp.where(head == h, sink_ref[h], row)
    return row


def _mixer_b(q_of, k2, v2, bias, sink_row, key_thresh, store):
    group = B_Q_HEADS // B_KV_HEADS
    pieces = []
    for j in range(B_Q_HEADS // 2):
        q2 = q_of(j)
        tq = q2.shape[0]
        lane_hi = lax.broadcasted_iota(jnp.int32, q2.shape, 1) >= HEAD_DIM
        zero = jnp.zeros_like(q2)
        q2_swapped = pltpu.roll(q2.astype(F32), HEAD_DIM, 1).astype(BF16)
        for hh in range(2):
            kv = (2 * j + hh) // group
            src = q2 if hh == kv else q2_swapped
            pieces.append(jnp.where(lane_hi, src, zero) if kv else jnp.where(lane_hi, zero, src))
    qq = jnp.concatenate(pieces, axis=0)
    o = _softmax_pv(_scores(qq, k2, bias, key_thresh), v2, sink_row)
    lane_hi = lax.broadcasted_iota(jnp.int32, (tq, HEAD_PAIR), 1) >= HEAD_DIM
    for j in range(B_Q_HEADS // 2):
        halves = []
        for hh in range(2):
            h = 2 * j + hh
            o_h = o[h * tq:(h + 1) * tq]
            halves.append(o_h if hh == h // group else pltpu.roll(o_h, HEAD_DIM, 1))
        store(j, jnp.where(lane_hi, halves[1], halves[0]))


def _attn_prompt_kernel(sink_ref, qa_ref, kap_ref, kac_ref, vap_ref, vac_ref,
                        qb_ref, kbp_ref, kbc_ref, vbp_ref, vbc_ref, bias_ref,
                        ya_ref, yb_ref, ka_s, va_s, kb_s, vb_s, bias_b_s, sink_s):
    i = pl.program_id(1)
    blk = ATT_BLOCK
    ka_s[0:blk, :] = kap_ref[...]
    ka_s[blk:2 * blk, :] = kac_ref[...]
    va_s[0:blk, :] = vap_ref[...]
    va_s[blk:2 * blk, :] = vac_ref[...]
    kb_s[0:blk, :] = kbp_ref[...]
    kb_s[blk:2 * blk, :] = kbc_ref[...]
    vb_s[0:blk, :] = vbp_ref[...]
    vb_s[blk:2 * blk, :] = vbc_ref[...]

    tk_a = (A_PREV_CHUNKS + 1) * CHUNK
    tk_b = (B_PREV_CHUNKS + 1) * CHUNK
    b_off = blk - B_PREV_CHUNKS * CHUNK
    bias_b_s[...] = _alibi_bias(CHUNK, tk_b)
    sink_s[...] = _sink_row(sink_ref, CHUNK)

    def chunk_body(c, carry):
        row0 = pl.multiple_of(c * CHUNK, CHUNK)
        thresh_a = jnp.where(i > 0, 0, blk - c * CHUNK)
        thresh_b = jnp.where(i > 0, 0, B_PREV_CHUNKS * CHUNK - c * CHUNK)

        def store_a(j, val):
            ya_ref[pl.ds(row0, CHUNK), j * HEAD_PAIR:(j + 1) * HEAD_PAIR] = val.astype(ya_ref.dtype)

        def store_b(j, val):
            yb_ref[pl.ds(row0, CHUNK), j * HEAD_PAIR:(j + 1) * HEAD_PAIR] = val.astype(yb_ref.dtype)

        _mixer_a(
            lambda j: qa_ref[pl.ds(row0, CHUNK), j * HEAD_PAIR:(j + 1) * HEAD_PAIR],
            lambda j: ka_s[pl.ds(row0, tk_a), j * HEAD_PAIR:(j + 1) * HEAD_PAIR],
            lambda j: va_s[pl.ds(row0, tk_a), j * HEAD_PAIR:(j + 1) * HEAD_PAIR],
            lambda j: bias_ref[j], thresh_a, store_a)

        rowb = pl.multiple_of(b_off + c * CHUNK, CHUNK)
        _mixer_b(
            lambda j: qb_ref[pl.ds(row0, CHUNK), j * HEAD_PAIR:(j + 1) * HEAD_PAIR],
            kb_s[pl.ds(rowb, tk_b), :], vb_s[pl.ds(rowb, tk_b), :],
            bias_b_s[...], sink_s[...], thresh_b, store_b)
        return carry

    lax.fori_loop(0, blk // CHUNK, chunk_body, 0)


def _attn_prompt(p, bias_a, sinks, batch, seq):
    blk = ATT_BLOCK
    nb = seq // blk
    n = batch * seq

    def cur(col_block):
        return lambda b, i: (b * nb + i, col_block)

    def prev(col_block):
        return lambda b, i: (b * nb + jnp.maximum(i - 1, 0), col_block)

    wa, wb = A_WIDTH, B_KV_WIDTH
    in_specs = [
        pl.BlockSpec(memory_space=pltpu.SMEM),
        pl.BlockSpec((blk, wa), cur(COL_QA // wa)),
        pl.BlockSpec((blk, wa), prev(COL_KA // wa)),
        pl.BlockSpec((blk, wa), cur(COL_KA // wa)),
        pl.BlockSpec((blk, wa), prev(COL_VA // wa)),
        pl.BlockSpec((blk, wa), cur(COL_VA // wa)),
        pl.BlockSpec((blk, wa), cur(COL_QB // wa)),
        pl.BlockSpec((blk, wb), prev(COL_KB // wb)),
        pl.BlockSpec((blk, wb), cur(COL_KB // wb)),
        pl.BlockSpec((blk, wb), prev(COL_VB // wb)),
        pl.BlockSpec((blk, wb), cur(COL_VB // wb)),
        _const_spec(bias_a.shape),
    ]
    out_spec = pl.BlockSpec((blk, wa), lambda b, i: (b * nb + i, 0))
    rows_b = B_Q_HEADS * CHUNK
    return pl.pallas_call(
        _attn_prompt_kernel,
        grid=(batch, nb),
        in_specs=in_specs,
        out_specs=[out_spec, out_spec],
        out_shape=[jax.ShapeDtypeStruct((n, A_WIDTH), BF16), jax.ShapeDtypeStruct((n, B_WIDTH), BF16)],
        scratch_shapes=[
            pltpu.VMEM((2 * blk, wa), BF16), pltpu.VMEM((2 * blk, wa), BF16),
            pltpu.VMEM((2 * blk, wb), BF16), pltpu.VMEM((2 * blk, wb), BF16),
            pltpu.VMEM(((B_PREV_CHUNKS + 1) * CHUNK, rows_b), F32), pltpu.VMEM((1, rows_b), F32),
        ],
        compiler_params=_compiler_params(("parallel", "arbitrary")),
    )(sinks, p, p, p, p, p, p, p, p, p, p, bias_a)


def _attn_sample_kernel(sink_ref, qa_ref, ka_ref, va_ref, qb_ref, kb_ref, vb_ref, bias_ref, ya_ref, yb_ref):
    t, tk_b = qb_ref.shape[0], kb_ref.shape[1]

    def store_a(j, val):
        ya_ref[:, j * HEAD_PAIR:(j + 1) * HEAD_PAIR] = val.astype(ya_ref.dtype)

    def store_b(j, val):
        yb_ref[:, j * HEAD_PAIR:(j + 1) * HEAD_PAIR] = val.astype(yb_ref.dtype)

    _mixer_a(
        lambda j: qa_ref[:, j * HEAD_PAIR:(j + 1) * HEAD_PAIR],
        lambda j: ka_ref[0, :, j * HEAD_PAIR:(j + 1) * HEAD_PAIR],
        lambda j: va_ref[0, :, j * HEAD_PAIR:(j + 1) * HEAD_PAIR],
        lambda j: bias_ref[j], None, store_a)

    _mixer_b(
        lambda j: qb_ref[:, j * HEAD_PAIR:(j + 1) * HEAD_PAIR],
        kb_ref[0], vb_ref[0], _alibi_bias(t, tk_b), _sink_row(sink_ref, t), None, store_b)


def _attn_sample(p, kcat_a, vcat_a, kcat_b, vcat_b, bias_a, sinks, batch, t):
    n = batch * t
    wa, wb = A_WIDTH, B_KV_WIDTH
    tk_a, tk_b = kcat_a.shape[1], kcat_b.shape[1]
    in_specs = [
        pl.BlockSpec(memory_space=pltpu.SMEM),
        pl.BlockSpec((t, wa), lambda b: (b, COL_QA // wa)),
        pl.BlockSpec((1, tk_a, wa), lambda b: (b, 0, 0)),
        pl.BlockSpec((1, tk_a, wa), lambda b: (b, 0, 0)),
        pl.BlockSpec((t, wa), lambda b: (b, COL_QB // wa)),
        pl.BlockSpec((1, tk_b, wb), lambda b: (b, 0, 0)),
        pl.BlockSpec((1, tk_b, wb), lambda b: (b, 0, 0)),
        _const_spec(bias_a.shape),
    ]
    out_spec = pl.BlockSpec((t, wa), lambda b: (b, 0))
    return pl.pallas_call(
        _attn_sample_kernel,
        grid=(batch,),
        in_specs=in_specs,
        out_specs=[out_spec, out_spec],
        out_shape=[jax.ShapeDtypeStruct((n, A_WIDTH), BF16), jax.ShapeDtypeStruct((n, B_WIDTH), BF16)],
        compiler_params=_compiler_params(("parallel",)),
    )(sinks, p, kcat_a, vcat_a, p, kcat_b, vcat_b, bias_a)


def _layer_norm(x, g, b):
    mu = jnp.mean(x, axis=-1, keepdims=True)
    xc = x - mu
    var = jnp.mean(xc * xc, axis=-1, keepdims=True)
    return xc * lax.rsqrt(var + LN_EPS) * g + b


def _merge_kernel(ya_ref, yb_ref, ga_ref, gb_ref, x_ref, wpa_ref, wpb_ref, wout_ref, g_ref, b_ref, h_ref):
    za = jnp.dot(ya_ref[...], wpa_ref[...], preferred_element_type=F32)
    zb = jnp.dot(yb_ref[...], wpb_ref[...], preferred_element_type=F32)
    m = jax.nn.sigmoid(ga_ref[...].astype(F32)) * za + jax.nn.sigmoid(gb_ref[...].astype(F32)) * zb
    r = jnp.dot(m.astype(BF16), wout_ref[...], preferred_element_type=F32)
    h_ref[...] = _layer_norm(DEEPNORM_ALPHA * x_ref[...] + r, g_ref[...], b_ref[...])


def _merge(ya, yb, p, x, wpa, wpb, wout, ln_g, ln_b, tm):
    n = x.shape[0]
    row = lambda i: (i, 0)
    return pl.pallas_call(
        _merge_kernel,
        grid=(n // tm,),
        in_specs=[
            pl.BlockSpec((tm, A_WIDTH), row),
            pl.BlockSpec((tm, B_WIDTH), row),
            pl.BlockSpec((tm, D_MODEL), lambda i: (i, COL_GA // D_MODEL)),
            pl.BlockSpec((tm, D_MODEL), lambda i: (i, COL_GB // D_MODEL)),
            pl.BlockSpec((tm, D_MODEL), row),
            _const_spec(wpa.shape), _const_spec(wpb.shape), _const_spec(wout.shape),
            _const_spec(ln_g.shape), _const_spec(ln_b.shape),
        ],
        out_specs=pl.BlockSpec((tm, D_MODEL), row),
        out_shape=jax.ShapeDtypeStruct((n, D_MODEL), F32),
        compiler_params=_compiler_params(("parallel",)),
    )(ya, yb, p, p, x, wpa, wpb, wout, ln_g, ln_b)


ROUTE_TM = 128
ID_SENTINEL = 1 << 30


def _topk_rows(s, ids, k):
    vals, picked = [], []
    for _ in range(k):
        m = jnp.max(s, axis=0, keepdims=True)
        ix = jnp.min(jnp.where(s == m, ids, ID_SENTINEL), axis=0, keepdims=True)
        s = jnp.where(ids == ix, -jnp.inf, s)
        vals.append(m)
        picked.append(ix)
    return jnp.concatenate(vals, axis=0), jnp.concatenate(picked, axis=0)


def _candidate_blocks(v1, v2):
    t = v1.shape[1]
    sub = lax.broadcasted_iota(jnp.int32, (SUBLANES, t), 0)
    neg = jnp.full((SUBLANES, t), -jnp.inf, F32)
    k = PEER_TOPK
    blocks = []
    blocks.append((v1[0:1] + v2[0:8], sub))
    blocks.append((v1[0:1] + v2[8:16], sub + 8))
    blocks.append((v1[1:2] + v2[0:8], sub + k))
    blocks.append((jnp.where(sub < 5, v1[2:3] + v2[0:8], neg), sub + 2 * k))
    blocks.append((jnp.where(sub < 4, v1[3:4] + v2[0:8], neg), sub + 3 * k))
    blocks.append((v1[8:16] + v2[0:1], (sub + 8) * k))
    a47 = jnp.concatenate([v1[4:8], v1[4:8]], axis=0)
    blocks.append((a47 + jnp.where(sub < 4, v2[0:1], v2[1:2]),
                   (4 + lax.bitwise_and(sub, 3)) * k + lax.shift_right_logical(sub, 2)))
    blocks.append((jnp.where(sub == 0, v1[4:5] + v2[2:3], neg), sub + 4 * k + 2))
    vals = jnp.concatenate([b[0] for b in blocks], axis=0)
    ids = jnp.concatenate([b[1] for b in blocks], axis=0)
    return vals, ids


def _route_kernel(h_ref, wq_ref, sk1_ref, sk2_ref, idx_ref, gate_ref):
    q = jnp.dot(h_ref[...].astype(BF16), wq_ref[...], preferred_element_type=F32).astype(BF16)
    half = PEER_QDIM // 2
    key_ids = lax.broadcasted_iota(jnp.int32, (PEER_KEYS, h_ref.shape[0]), 0)
    experts, gates = [], []
    for hd in range(PEER_HEADS):
        q1 = q[:, hd * PEER_QDIM:hd * PEER_QDIM + half]
        q2 = q[:, hd * PEER_QDIM + half:(hd + 1) * PEER_QDIM]
        s1 = lax.dot_general(sk1_ref[...], q1, NT_DIMS, preferred_element_type=F32)
        s2 = lax.dot_general(sk2_ref[...], q2, NT_DIMS, preferred_element_type=F32)
        v1, i1 = _topk_rows(s1, key_ids, PEER_TOPK)
        v2, i2 = _topk_rows(s2, key_ids, PEER_TOPK)
        sc, ci = _topk_rows(*_candidate_blocks(v1, v2), PEER_TOPK)
        ca = lax.shift_right_logical(ci, int(math.log2(PEER_TOPK)))
        cb = lax.bitwise_and(ci, PEER_TOPK - 1)
        e1 = jnp.zeros_like(ci)
        e2 = jnp.zeros_like(ci)
        for a in range(PEER_TOPK):
            e1 = jnp.where(ca == a, i1[a:a + 1], e1)
            e2 = jnp.where(cb == a, i2[a:a + 1], e2)
        experts.append(e1 * PEER_KEYS + e2)
        e = jnp.exp(sc - sc[0:1])
        gates.append(e / jnp.sum(e, axis=0, keepdims=True))
    idx_ref[...] = jnp.concatenate(experts, axis=0).astype(F32).T.astype(jnp.int32)
    gate_ref[...] = jnp.concatenate(gates, axis=0).T


def _route(h, wq, sk1, sk2):
    n = h.shape[0]
    tm = ROUTE_TM
    row = lambda i: (i, 0)
    return pl.pallas_call(
        _route_kernel,
        grid=(n // tm,),
        in_specs=[pl.BlockSpec((tm, D_MODEL), row), _const_spec(wq.shape),
                  _const_spec(sk1.shape), _const_spec(sk2.shape)],
        out_specs=[pl.BlockSpec((tm, PEER_PICKS), row), pl.BlockSpec((tm, PEER_PICKS), row)],
        out_shape=[jax.ShapeDtypeStruct((n, PEER_PICKS), jnp.int32),
                   jax.ShapeDtypeStruct((n, PEER_PICKS), F32)],
        compiler_params=_compiler_params(("parallel",)),
    )(h, wq, sk1, sk2)


PEER_TB = 32
PEER_ROUND = 8
PEER_SLOTS = 2 * PEER_ROUND
PEER_LOOKAHEAD = PEER_ROUND
ROW_TILES = D_MODEL // LANES
HALF_TILES = ROW_TILES // 2

def _gelu_exact(x):
    return 0.5 * x * (1.0 + lax.erf(x * (2.0 ** -0.5)))


def _peer_kernel(idx_ref, gate_ref, h_ref, uv_ref, g_ref, b_ref, y_ref, *scratch):
    bufs, (sem, ffn, pre, wgt) = scratch[:PEER_SLOTS], scratch[PEER_SLOTS:]
    i = pl.program_id(0)
    last = pl.num_programs(0) - 1
    tb = h_ref.shape[0]

    def issue(t, slot, part=None):
        per_part = PEER_PICKS // ROW_TILES
        picks = range(PEER_PICKS) if part is None else range(part * per_part, (part + 1) * per_part)
        for k in picks:
            dst = bufs[slot].at[:, k // SUBLANES, pl.ds(k % SUBLANES, 1), :]
            pltpu.make_async_copy(uv_ref.at[idx_ref[0, t, k]], dst, sem.at[slot]).start(priority=k % 2)

    def wait_slot(slot):
        pltpu.make_async_copy(bufs[slot], bufs[slot], sem.at[slot]).wait()

    def picks_tile(slot, c):
        return bufs[slot][c].reshape(PEER_PICKS, LANES)

    @pl.when(i == 0)
    def _():
        for t in range(PEER_LOOKAHEAD):
            issue(t, t)

    eye = (lax.broadcasted_iota(jnp.int32, (PEER_PICKS, PEER_PICKS), 0)
           == lax.broadcasted_iota(jnp.int32, (PEER_PICKS, PEER_PICKS), 1))

    def halves(words):
        lo = lax.bitcast_convert_type(lax.shift_left(words, jnp.int32(16)), F32)
        hi = lax.bitcast_convert_type(lax.bitwise_and(words, jnp.int32(-(1 << 16))), F32)
        return lo, hi

    def pre_activation(t, slot, between):
        h_row = h_ref[pl.ds(t, 1), :]
        acc = None
        for c in range(HALF_TILES):
            between(c)
            lo, hi = halves(picks_tile(slot, c))
            term = (lo * h_row[:, c * LANES:(c + 1) * LANES]
                    + hi * h_row[:, (HALF_TILES + c) * LANES:(HALF_TILES + c + 1) * LANES])
            acc = term if acc is None else acc + term
        a_col = jnp.sum(acc, axis=1, keepdims=True)
        return jnp.sum(jnp.where(eye, a_col, 0.0), axis=0, keepdims=True)

    def weighted_sum(t, slot, w_row, between):
        w_col = jnp.sum(jnp.where(eye, w_row, 0.0), axis=1, keepdims=True)
        out_lo, out_hi = [], []
        for c in range(HALF_TILES):
            between(HALF_TILES + c)
            lo, hi = halves(picks_tile(slot, HALF_TILES + c))
            out_lo.append(jnp.sum(w_col * lo, axis=0, keepdims=True))
            out_hi.append(jnp.sum(w_col * hi, axis=0, keepdims=True))
        ffn[pl.ds(t, 1), :] = jnp.concatenate(out_lo + out_hi, axis=1)

    def round_pair(rr, carry):
        for half in range(2):
            t0 = pl.multiple_of((2 * rr + half) * PEER_ROUND, PEER_ROUND)
            here, there = half * PEER_ROUND, (1 - half) * PEER_ROUND
            def issue_ahead(j):
                return lambda part: issue(t0 + PEER_ROUND + j, there + j, part)

            for j in range(PEER_ROUND):
                wait_slot(here + j)
                pre[j:j + 1, :] = pre_activation(t0 + j, here + j, issue_ahead(j))
            wgt[...] = gate_ref[pl.ds(t0, PEER_ROUND), :] * _gelu_exact(pre[...])
            for j in range(PEER_ROUND):
                weighted_sum(t0 + j, here + j, wgt[j:j + 1, :], issue_ahead(j))
        return carry

    lax.fori_loop(0, tb // PEER_SLOTS, round_pair, 0)

    @pl.when(i == last)
    def _():
        for t in range(PEER_LOOKAHEAD):
            wait_slot(t)

    y_ref[...] = _layer_norm(DEEPNORM_ALPHA * h_ref[...] + ffn[...], g_ref[...], b_ref[...])


def _peer(h, idx, gate, uv, ln_g, ln_b):
    n = h.shape[0]
    tb = min(PEER_TB, n)
    assert n % tb == 0 and tb % PEER_SLOTS == 0
    nb = n // tb
    row = lambda i: (i, 0)
    idx_pad = jnp.concatenate([idx, jnp.zeros((tb, PEER_PICKS), idx.dtype)], axis=0)
    idx_next = idx_pad[tb:].reshape(nb, tb, PEER_PICKS)[:, :PEER_LOOKAHEAD]
    idx_ext = jnp.concatenate([idx.reshape(nb, tb, PEER_PICKS), idx_next], axis=1)
    return pl.pallas_call(
        _peer_kernel,
        grid=(nb,),
        in_specs=[
            pl.BlockSpec((1, tb + PEER_LOOKAHEAD, PEER_PICKS), lambda i: (i, 0, 0), memory_space=pltpu.SMEM),
            pl.BlockSpec((tb, PEER_PICKS), row),
            pl.BlockSpec((tb, D_MODEL), row),
            pl.BlockSpec(memory_space=pl.ANY),
            _const_spec(ln_g.shape), _const_spec(ln_b.shape),
        ],
        out_specs=pl.BlockSpec((tb, D_MODEL), row),
        out_shape=jax.ShapeDtypeStruct((n, D_MODEL), F32),
        scratch_shapes=(
            [pltpu.VMEM((ROW_TILES, PEER_PICKS // SUBLANES, SUBLANES, LANES), jnp.int32) for _ in range(PEER_SLOTS)]
            + [pltpu.SemaphoreType.DMA((PEER_SLOTS,)), pltpu.VMEM((tb, D_MODEL), F32),
               pltpu.VMEM((PEER_ROUND, PEER_PICKS), F32), pltpu.VMEM((PEER_ROUND, PEER_PICKS), F32)]
        ),
        compiler_params=_compiler_params(("arbitrary",)),
    )(idx_ext, gate, h, uv, ln_g, ln_b)


def _rel_bias_pairs(rel_table, t, tk):
    dist = (tk - 1) - jnp.arange(tk + t - 1)
    line = jnp.take(rel_table, jnp.clip(dist, -A_REL_CLIP, A_REL_CLIP) + A_REL_CLIP, axis=1).astype(F32)
    bias = jnp.stack([line[:, t - 1 - q:t - 1 - q + tk] for q in range(t)], axis=1)
    return jnp.swapaxes(bias.reshape(A_HEADS // 2, 2 * t, tk), 1, 2)


PACK_ROWS = 256


def _pack_kernel(u_ref, v_ref, out_ref, stage, sem):
    i = pl.program_id(0)
    buf = lax.rem(i, 2)
    rows = u_ref.shape[0]
    half = D_MODEL // 2

    def drain(b):
        pltpu.make_async_copy(stage.at[b], stage.at[b], sem.at[b]).wait()

    @pl.when(i >= 2)
    def _():
        drain(buf)

    def words(x_ref):
        def bf16_bits(x):
            return lax.bitcast_convert_type(x.astype(BF16).astype(F32), jnp.int32)
        lo = lax.shift_right_logical(bf16_bits(x_ref[:, 0:half]), jnp.int32(16))
        hi = lax.bitwise_and(bf16_bits(x_ref[:, half:D_MODEL]), jnp.int32(-(1 << 16)))
        return lax.bitwise_or(hi, lo)

    for part, x_ref in enumerate((u_ref, v_ref)):
        w = words(x_ref)
        for c in range(HALF_TILES):
            tile = w[:, c * LANES:(c + 1) * LANES].reshape(rows // SUBLANES, SUBLANES, LANES)
            stage[buf, part * HALF_TILES + c] = tile

    def row_copy(r, carry):
        for s in range(SUBLANES):
            src = stage.at[buf, :, r, pl.ds(s, 1), :]
            pltpu.make_async_copy(src, out_ref.at[i * rows + r * SUBLANES + s], sem.at[buf]).start()
        return carry

    lax.fori_loop(0, rows // SUBLANES, row_copy, 0)

    @pl.when(i == pl.num_programs(0) - 1)
    def _():
        drain(buf)

        @pl.when(i >= 1)
        def _():
            drain(1 - buf)


def _pack_expert_tables(u, v):
    n = u.shape[0]
    rows = PACK_ROWS
    assert n % rows == 0
    return pl.pallas_call(
        _pack_kernel,
        grid=(n // rows,),
        in_specs=[pl.BlockSpec((rows, D_MODEL), lambda i: (i, 0)), pl.BlockSpec((rows, D_MODEL), lambda i: (i, 0))],
        out_specs=pl.BlockSpec(memory_space=pl.ANY),
        out_shape=jax.ShapeDtypeStruct((n, ROW_TILES, 1, LANES), jnp.int32),
        scratch_shapes=[pltpu.VMEM((2, ROW_TILES, rows // SUBLANES, SUBLANES, LANES), jnp.int32),
                        pltpu.SemaphoreType.DMA((2,))],
        compiler_params=_compiler_params(("arbitrary",)),
    )(u, v)


def _permute_in_proj(w_in):
    offs = np.cumsum([0, A_WIDTH, A_WIDTH, A_WIDTH, B_WIDTH, B_KV_WIDTH, B_KV_WIDTH, D_MODEL, D_MODEL])
    qa, ka, va, qb, kb, vb, ga, gb = [w_in[:, offs[s]:offs[s + 1]] for s in range(8)]
    return jnp.concatenate([ga, gb, qa, qb, ka, va, kb, vb], axis=1).astype(BF16)


def _split_kv(kv, batch, t):
    ka = kv[:, 0:A_WIDTH].reshape(batch, t, A_HEADS, HEAD_DIM)
    va = kv[:, A_WIDTH:2 * A_WIDTH].reshape(batch, t, A_HEADS, HEAD_DIM)
    kb = kv[:, 2 * A_WIDTH:2 * A_WIDTH + B_KV_WIDTH].reshape(batch, t, B_KV_HEADS, HEAD_DIM)
    vb = kv[:, 2 * A_WIDTH + B_KV_WIDTH:].reshape(batch, t, B_KV_HEADS, HEAD_DIM)
    return ka, va, kb, vb


def kernel(x_prompt, x_sample, cache_a_k, cache_a_v, cache_b_k, cache_b_v, w_in, rel_bias_table, b_sinks,
           w_proj_a, w_proj_b, w_out, ln1_g, ln1_b, peer_w_query, peer_sub_keys_1, peer_sub_keys_2,
           peer_u, peer_v, ln2_g, ln2_b):
    batch, seq, _ = x_prompt.shape
    dbatch, dseq, _ = x_sample.shape
    assert seq % ATT_BLOCK == 0 and seq >= ATT_BLOCK

    w_in_p = _permute_in_proj(w_in)
    wpa, wpb, wout = w_proj_a.astype(BF16), w_proj_b.astype(BF16), w_out.astype(BF16)
    wq = peer_w_query.astype(BF16)
    sk1, sk2 = peer_sub_keys_1.astype(BF16), peer_sub_keys_2.astype(BF16)
    uv = _pack_expert_tables(peer_u, peer_v)
    ln1 = (ln1_g.reshape(1, D_MODEL), ln1_b.reshape(1, D_MODEL))
    ln2 = (ln2_g.reshape(1, D_MODEL), ln2_b.reshape(1, D_MODEL))
    sinks = b_sinks.astype(F32)

    def tail(xf, ya, yb, p, tm):
        h = _merge(ya, yb, p, xf, wpa, wpb, wout, *ln1, tm=tm)
        idx, gate = _route(h, wq, sk1, sk2)
        return _peer(h, idx, gate, uv, *ln2)

    ns = dbatch * dseq
    xs = x_sample.reshape(ns, D_MODEL)
    p_s, kv_s = _in_proj(xs, w_in_p, tm=ns)
    ka_s, va_s, kb_s, vb_s = _split_kv(kv_s, dbatch, dseq)

    def with_cache(cache, new_cols, width):
        new = p_s[:, new_cols:new_cols + width].reshape(dbatch, dseq, width)
        return jnp.concatenate([cache.reshape(dbatch, -1, width).astype(BF16), new], axis=1)

    kcat_a = with_cache(cache_a_k, COL_KA, A_WIDTH)
    vcat_a = with_cache(cache_a_v, COL_VA, A_WIDTH)
    kcat_b = with_cache(cache_b_k, COL_KB, B_KV_WIDTH)
    vcat_b = with_cache(cache_b_v, COL_VB, B_KV_WIDTH)
    bias_s = _rel_bias_pairs(rel_bias_table, dseq, kcat_a.shape[1])
    ya_s, yb_s = _attn_sample(p_s, kcat_a, vcat_a, kcat_b, vcat_b, bias_s, sinks, dbatch, dseq)
    y_sample = tail(xs, ya_s, yb_s, p_s, ns).reshape(dbatch, dseq, D_MODEL)

    xp = x_prompt.reshape(batch * seq, D_MODEL)
    p, kv = _in_proj(xp, w_in_p, tm=1024)
    bias_p = _rel_bias_pairs(rel_bias_table, CHUNK, (A_PREV_CHUNKS + 1) * CHUNK)
    ya, yb = _attn_prompt(p, bias_p, sinks, batch, seq)
    y_prompt = tail(xp, ya, yb, p, 256).reshape(batch, seq, D_MODEL)
    ka, va, kb, vb = _split_kv(kv, batch, seq)
    la = min(A_PREV_CHUNKS * CHUNK, seq)
    lb = min(B_PREV_CHUNKS * CHUNK, seq)

    return (y_prompt, y_sample, ka[:, -la:], va[:, -la:], kb[:, -lb:], vb[:, -lb:], ka_s, va_s, kb_s, vb_s)
```

```python
import math

import jax
import jax.numpy as jnp
import numpy as np
from jax import lax
from jax.experimental import pallas as pl
from jax.experimental.pallas import tpu as pltpu

F32 = jnp.float32
BF16 = jnp.bfloat16

D_MODEL = 2048
CHUNK = 64
HEAD_DIM = 64
A_HEADS = 16
A_PREV_CHUNKS = 8
A_REL_CLIP = 128
B_Q_HEADS = 16
B_KV_HEADS = 2
B_PREV_CHUNKS = 2
A_WIDTH = A_HEADS * HEAD_DIM
B_WIDTH = B_Q_HEADS * HEAD_DIM
B_KV_WIDTH = B_KV_HEADS * HEAD_DIM
PEER_HEADS = 8
PEER_KEYS = 128
PEER_QDIM = 256
PEER_TOPK = 16
PEER_PICKS = PEER_HEADS * PEER_TOPK
LN_EPS = 1e-5
DEEPNORM_ALPHA = 2.0 ** 0.25

LANES = 128
SUBLANES = 8
HEAD_PAIR = 2 * HEAD_DIM
VMEM_LIMIT = 56 * 1024 * 1024

COL_GA = 0
COL_GB = COL_GA + D_MODEL
COL_QA = COL_GB + D_MODEL
COL_QB = COL_QA + A_WIDTH
COL_KA = COL_QB + B_WIDTH
COL_VA = COL_KA + A_WIDTH
COL_KB = COL_VA + A_WIDTH
COL_VB = COL_KB + B_KV_WIDTH
IN_COLS = COL_VB + B_KV_WIDTH
KV_COLS = IN_COLS - COL_KA
PROJ_TN = 768
ATT_BLOCK = A_PREV_CHUNKS * CHUNK

NT_DIMS = (((1,), (1,)), ((), ()))
TN_DIMS = (((0,), (0,)), ((), ()))


def _compiler_params(semantics):
    return pltpu.CompilerParams(dimension_semantics=semantics, vmem_limit_bytes=VMEM_LIMIT)


def _const_spec(shape):
    zeros = (0,) * len(shape)
    return pl.BlockSpec(shape, lambda *_: zeros, pipeline_mode=pl.Buffered(1))


def _in_proj_kernel(x_ref, w_ref, p_ref, kv_ref, xb_ref):
    j = pl.program_id(1)

    @pl.when(j == 0)
    def _():
        xb_ref[...] = x_ref[...].astype(BF16)

    acc = jnp.dot(xb_ref[...], w_ref[...], preferred_element_type=F32)
    p_ref[...] = acc.astype(BF16)

    @pl.when(j >= COL_KA // PROJ_TN)
    def _():
        kv_ref[...] = acc


def _in_proj(x, w_bf16, tm):
    n = x.shape[0]
    first_kv = COL_KA // PROJ_TN
    return pl.pallas_call(
        _in_proj_kernel,
        grid=(n // tm, IN_COLS // PROJ_TN),
        in_specs=[
            pl.BlockSpec((tm, D_MODEL), lambda i, j: (i, 0)),
            pl.BlockSpec((D_MODEL, PROJ_TN), lambda i, j: (0, j)),
        ],
        out_specs=[
            pl.BlockSpec((tm, PROJ_TN), lambda i, j: (i, j)),
            pl.BlockSpec((tm, PROJ_TN), lambda i, j: (i, jnp.maximum(j - first_kv, 0))),
        ],
        out_shape=[
            jax.ShapeDtypeStruct((n, IN_COLS), BF16),
            jax.ShapeDtypeStruct((n, KV_COLS), F32),
        ],
        scratch_shapes=[pltpu.VMEM((tm, D_MODEL), BF16)],
        compiler_params=_compiler_params(("parallel", "arbitrary")),
    )(x, w_bf16)


def _softmax_pv(s, v2, sink):
    m = jnp.max(s, axis=0, keepdims=True)
    if sink is not None:
        m = jnp.maximum(m, sink)
    p = jnp.exp(s - m)
    den = jnp.sum(p, axis=0, keepdims=True)
    if sink is not None:
        den = den + jnp.exp(sink - m)
    p = (p * (1.0 / den)).astype(BF16)
    return lax.dot_general(p, v2, TN_DIMS, preferred_element_type=F32)


def _scores(qq, k2, bias, key_thresh):
    s = lax.dot_general(k2, qq, NT_DIMS, preferred_element_type=F32) * (HEAD_DIM ** -0.5) + bias
    if key_thresh is not None:
        key = lax.broadcasted_iota(jnp.int32, s.shape, 0)
        s = jnp.where(key >= key_thresh, s, -jnp.inf)
    return s


def _mixer_a(q_of, k_of, v_of, bias_of, key_thresh, store):
    for j in range(A_HEADS // 2):
        q2 = q_of(j)
        tq = q2.shape[0]
        lane_hi = lax.broadcasted_iota(jnp.int32, q2.shape, 1) >= HEAD_DIM
        zero = jnp.zeros_like(q2)
        qq = jnp.concatenate([jnp.where(lane_hi, zero, q2), jnp.where(lane_hi, q2, zero)], axis=0)
        o = _softmax_pv(_scores(qq, k_of(j), bias_of(j), key_thresh), v_of(j), None)
        store(j, jnp.where(lane_hi, o[tq:2 * tq], o[0:tq]))


def _alibi_bias(tq, tk):
    shape = (tk, B_Q_HEADS * tq)
    key = lax.broadcasted_iota(jnp.int32, shape, 0)
    lane = lax.broadcasted_iota(jnp.int32, shape, 1)
    head = lax.shift_right_logical(lane, int(math.log2(tq)))
    q = lax.bitwise_and(lane, tq - 1)
    slope = jnp.exp((head + 1).astype(F32) * (-8.0 / B_Q_HEADS * math.log(2.0)))
    return -slope * jnp.abs(q + (tk - tq) - key).astype(F32)


def _sink_row(sink_ref, tq):
    lane = lax.broadcasted_iota(jnp.int32, (1, B_Q_HEADS * tq), 1)
    head = lax.shift_right_logical(lane, int(math.log2(tq)))
    row = jnp.zeros(lane.shape, F32)
    for h in range(B_Q_HEADS):
        row = jnp.where(head == h, sink_ref[h], row)
    return row


def _mixer_b(q_of, k2, v2, bias, sink_row, key_thresh, store):
    group = B_Q_HEADS // B_KV_HEADS
    pieces = []
    for j in range(B_Q_HEADS // 2):
        q2 = q_of(j)
        tq = q2.shape[0]
        lane_hi = lax.broadcasted_iota(jnp.int32, q2.shape, 1) >= HEAD_DIM
        zero = jnp.zeros_like(q2)
        q2_swapped = pltpu.roll(q2.astype(F32), HEAD_DIM, 1).astype(BF16)
        for hh in range(2):
            kv = (2 * j + hh) // group
            src = q2 if hh == kv else q2_swapped
            pieces.append(jnp.where(lane_hi, src, zero) if kv else jnp.where(lane_hi, zero, src))
    qq = jnp.concatenate(pieces, axis=0)
    o = _softmax_pv(_scores(qq, k2, bias, key_thresh), v2, sink_row)
    lane_hi = lax.broadcasted_iota(jnp.int32, (tq, HEAD_PAIR), 1) >= HEAD_DIM
    for j in range(B_Q_HEADS // 2):
        halves = []
        for hh in range(2):
            h = 2 * j + hh
            o_h = o[h * tq:(h + 1) * tq]
            halves.append(o_h if hh == h // group else pltpu.roll(o_h, HEAD_DIM, 1))
        store(j, jnp.where(lane_hi, halves[1], halves[0]))


def _attn_prompt_kernel(sink_ref, qa_ref, kap_ref, kac_ref, vap_ref, vac_ref,
                        qb_ref, kbp_ref, kbc_ref, vbp_ref, vbc_ref, bias_ref,
                        ya_ref, yb_ref, ka_s, va_s, kb_s, vb_s, bias_b_s, sink_s):
    i = pl.program_id(1)
    blk = ATT_BLOCK
    ka_s[0:blk, :] = kap_ref[...]
    ka_s[blk:2 * blk, :] = kac_ref[...]
    va_s[0:blk, :] = vap_ref[...]
    va_s[blk:2 * blk, :] = vac_ref[...]
    kb_s[0:blk, :] = kbp_ref[...]
    kb_s[blk:2 * blk, :] = kbc_ref[...]
    vb_s[0:blk, :] = vbp_ref[...]
    vb_s[blk:2 * blk, :] = vbc_ref[...]

    tk_a = (A_PREV_CHUNKS + 1) * CHUNK
    tk_b = (B_PREV_CHUNKS + 1) * CHUNK
    b_off = blk - B_PREV_CHUNKS * CHUNK
    bias_b_s[...] = _alibi_bias(CHUNK, tk_b)
    sink_s[...] = _sink_row(sink_ref, CHUNK)

    def chunk_body(c, carry):
        row0 = pl.multiple_of(c * CHUNK, CHUNK)
        thresh_a = jnp.where(i > 0, 0, blk - c * CHUNK)
        thresh_b = jnp.where(i > 0, 0, B_PREV_CHUNKS * CHUNK - c * CHUNK)

        def store_a(j, val):
            ya_ref[pl.ds(row0, CHUNK), j * HEAD_PAIR:(j + 1) * HEAD_PAIR] = val.astype(ya_ref.dtype)

        def store_b(j, val):
            yb_ref[pl.ds(row0, CHUNK), j * HEAD_PAIR:(j + 1) * HEAD_PAIR] = val.astype(yb_ref.dtype)

        _mixer_a(
            lambda j: qa_ref[pl.ds(row0, CHUNK), j * HEAD_PAIR:(j + 1) * HEAD_PAIR],
            lambda j: ka_s[pl.ds(row0, tk_a), j * HEAD_PAIR:(j + 1) * HEAD_PAIR],
            lambda j: va_s[pl.ds(row0, tk_a), j * HEAD_PAIR:(j + 1) * HEAD_PAIR],
            lambda j: bias_ref[j], thresh_a, store_a)

        rowb = pl.multiple_of(b_off + c * CHUNK, CHUNK)
        _mixer_b(
            lambda j: qb_ref[pl.ds(row0, CHUNK), j * HEAD_PAIR:(j + 1) * HEAD_PAIR],
            kb_s[pl.ds(rowb, tk_b), :], vb_s[pl.ds(rowb, tk_b), :],
            bias_b_s[...], sink_s[...], thresh_b, store_b)
        return carry

    lax.fori_loop(0, blk // CHUNK, chunk_body, 0)


def _attn_prompt(p, bias_a, sinks, batch, seq):
    blk = ATT_BLOCK
    nb = seq // blk
    n = batch * seq

    def cur(col_block):
        return lambda b, i: (b * nb + i, col_block)

    def prev(col_block):
        return lambda b, i: (b * nb + jnp.maximum(i - 1, 0), col_block)

    wa, wb = A_WIDTH, B_KV_WIDTH
    in_specs = [
        pl.BlockSpec(memory_space=pltpu.SMEM),
        pl.BlockSpec((blk, wa), cur(COL_QA // wa)),
        pl.BlockSpec((blk, wa), prev(COL_KA // wa)),
        pl.BlockSpec((blk, wa), cur(COL_KA // wa)),
        pl.BlockSpec((blk, wa), prev(COL_VA // wa)),
        pl.BlockSpec((blk, wa), cur(COL_VA // wa)),
        pl.BlockSpec((blk, wa), cur(COL_QB // wa)),
        pl.BlockSpec((blk, wb), prev(COL_KB // wb)),
        pl.BlockSpec((blk, wb), cur(COL_KB // wb)),
        pl.BlockSpec((blk, wb), prev(COL_VB // wb)),
        pl.BlockSpec((blk, wb), cur(COL_VB // wb)),
        _const_spec(bias_a.shape),
    ]
    out_spec = pl.BlockSpec((blk, wa), lambda b, i: (b * nb + i, 0))
    rows_b = B_Q_HEADS * CHUNK
    return pl.pallas_call(
        _attn_prompt_kernel,
        grid=(batch, nb),
        in_specs=in_specs,
        out_specs=[out_spec, out_spec],
        out_shape=[jax.ShapeDtypeStruct((n, A_WIDTH), BF16), jax.ShapeDtypeStruct((n, B_WIDTH), BF16)],
        scratch_shapes=[
            pltpu.VMEM((2 * blk, wa), BF16), pltpu.VMEM((2 * blk, wa), BF16),
            pltpu.VMEM((2 * blk, wb), BF16), pltpu.VMEM((2 * blk, wb), BF16),
            pltpu.VMEM(((B_PREV_CHUNKS + 1) * CHUNK, rows_b), F32), pltpu.VMEM((1, rows_b), F32),
        ],
        compiler_params=_compiler_params(("parallel", "arbitrary")),
    )(sinks, p, p, p, p, p, p, p, p, p, p, bias_a)


def _attn_sample_kernel(sink_ref, qa_ref, ka_ref, va_ref, qb_ref, kb_ref, vb_ref, bias_ref, ya_ref, yb_ref):
    t, tk_b = qb_ref.shape[0], kb_ref.shape[1]

    def store_a(j, val):
        ya_ref[:, j * HEAD_PAIR:(j + 1) * HEAD_PAIR] = val.astype(ya_ref.dtype)

    def store_b(j, val):
        yb_ref[:, j * HEAD_PAIR:(j + 1) * HEAD_PAIR] = val.astype(yb_ref.dtype)

    _mixer_a(
        lambda j: qa_ref[:, j * HEAD_PAIR:(j + 1) * HEAD_PAIR],
        lambda j: ka_ref[0, :, j * HEAD_PAIR:(j + 1) * HEAD_PAIR],
        lambda j: va_ref[0, :, j * HEAD_PAIR:(j + 1) * HEAD_PAIR],
        lambda j: bias_ref[j], None, store_a)

    _mixer_b(
        lambda j: qb_ref[:, j * HEAD_PAIR:(j + 1) * HEAD_PAIR],
        kb_ref[0], vb_ref[0], _alibi_bias(t, tk_b), _sink_row(sink_ref, t), None, store_b)


def _attn_sample(p, kcat_a, vcat_a, kcat_b, vcat_b, bias_a, sinks, batch, t):
    n = batch * t
    wa, wb = A_WIDTH, B_KV_WIDTH
    tk_a, tk_b = kcat_a.shape[1], kcat_b.shape[1]
    in_specs = [
        pl.BlockSpec(memory_space=pltpu.SMEM),
        pl.BlockSpec((t, wa), lambda b: (b, COL_QA // wa)),
        pl.BlockSpec((1, tk_a, wa), lambda b: (b, 0, 0)),
        pl.BlockSpec((1, tk_a, wa), lambda b: (b, 0, 0)),
        pl.BlockSpec((t, wa), lambda b: (b, COL_QB // wa)),
        pl.BlockSpec((1, tk_b, wb), lambda b: (b, 0, 0)),
        pl.BlockSpec((1, tk_b, wb), lambda b: (b, 0, 0)),
        _const_spec(bias_a.shape),
    ]
    out_spec = pl.BlockSpec((t, wa), lambda b: (b, 0))
    return pl.pallas_call(
        _attn_sample_kernel,
        grid=(batch,),
        in_specs=in_specs,
        out_specs=[out_spec, out_spec],
        out_shape=[jax.ShapeDtypeStruct((n, A_WIDTH), BF16), jax.ShapeDtypeStruct((n, B_WIDTH), BF16)],
        compiler_params=_compiler_params(("parallel",)),
    )(sinks, p, kcat_a, vcat_a, p, kcat_b, vcat_b, bias_a)


def _layer_norm(x, g, b):
    mu = jnp.mean(x, axis=-1, keepdims=True)
    xc = x - mu
    var = jnp.mean(xc * xc, axis=-1, keepdims=True)
    return xc * lax.rsqrt(var + LN_EPS) * g + b


def _merge_route_kernel(ya_ref, yb_ref, ga_ref, gb_ref, x_ref, wpa_ref, wpb_ref, wout_ref, g_ref, b_ref,
                        wq_ref, sk1_ref, sk2_ref, h_ref, idx_ref, gate_ref):
    za = jnp.dot(ya_ref[...], wpa_ref[...], preferred_element_type=F32)
    zb = jnp.dot(yb_ref[...], wpb_ref[...], preferred_element_type=F32)
    m = jax.nn.sigmoid(ga_ref[...].astype(F32)) * za + jax.nn.sigmoid(gb_ref[...].astype(F32)) * zb
    r = jnp.dot(m.astype(BF16), wout_ref[...], preferred_element_type=F32)
    h_ref[...] = _layer_norm(DEEPNORM_ALPHA * x_ref[...] + r, g_ref[...], b_ref[...])
    for t0 in range(0, h_ref.shape[0], ROUTE_TM):
        idx, gate = _route_tokens(h_ref[t0:t0 + ROUTE_TM, :].astype(BF16), wq_ref, sk1_ref, sk2_ref)
        idx_ref[t0:t0 + ROUTE_TM, :] = idx
        gate_ref[t0:t0 + ROUTE_TM, :] = gate


def _merge_route(ya, yb, p, x, wpa, wpb, wout, ln_g, ln_b, wq, sk1, sk2, tm):
    n = x.shape[0]
    assert n % tm == 0 and tm % ROUTE_TM == 0
    row = lambda i: (i, 0)
    return pl.pallas_call(
        _merge_route_kernel,
        grid=(n // tm,),
        in_specs=[
            pl.BlockSpec((tm, A_WIDTH), row),
            pl.BlockSpec((tm, B_WIDTH), row),
            pl.BlockSpec((tm, D_MODEL), lambda i: (i, COL_GA // D_MODEL)),
            pl.BlockSpec((tm, D_MODEL), lambda i: (i, COL_GB // D_MODEL)),
            pl.BlockSpec((tm, D_MODEL), row),
            _const_spec(wpa.shape), _const_spec(wpb.shape), _const_spec(wout.shape),
            _const_spec(ln_g.shape), _const_spec(ln_b.shape),
            _const_spec(wq.shape), _const_spec(sk1.shape), _const_spec(sk2.shape),
        ],
        out_specs=[pl.BlockSpec((tm, D_MODEL), row), pl.BlockSpec((tm, PEER_PICKS), row),
                   pl.BlockSpec((tm, PEER_PICKS), row)],
        out_shape=[jax.ShapeDtypeStruct((n, D_MODEL), F32), jax.ShapeDtypeStruct((n, PEER_PICKS), jnp.int32),
                   jax.ShapeDtypeStruct((n, PEER_PICKS), F32)],
        compiler_params=_compiler_params(("parallel",)),
    )(ya, yb, p, p, x, wpa, wpb, wout, ln_g, ln_b, wq, sk1, sk2)


ROUTE_TM = 128
ID_SENTINEL = 1 << 30


def _topk_rows(s, ids, k):
    vals, picked = [], []
    for _ in range(k):
        m = jnp.max(s, axis=0, keepdims=True)
        ix = jnp.min(jnp.where(s == m, ids, ID_SENTINEL), axis=0, keepdims=True)
        s = jnp.where(ids == ix, -jnp.inf, s)
        vals.append(m)
        picked.append(ix)
    return jnp.concatenate(vals, axis=0), jnp.concatenate(picked, axis=0)


def _candidate_blocks(v1, v2):
    t = v1.shape[1]
    sub = lax.broadcasted_iota(jnp.int32, (SUBLANES, t), 0)
    neg = jnp.full((SUBLANES, t), -jnp.inf, F32)
    k = PEER_TOPK
    blocks = []
    blocks.append((v1[0:1] + v2[0:8], sub))
    blocks.append((v1[0:1] + v2[8:16], sub + 8))
    blocks.append((v1[1:2] + v2[0:8], sub + k))
    blocks.append((jnp.where(sub < 5, v1[2:3] + v2[0:8], neg), sub + 2 * k))
    blocks.append((jnp.where(sub < 4, v1[3:4] + v2[0:8], neg), sub + 3 * k))
    blocks.append((v1[8:16] + v2[0:1], (sub + 8) * k))
    a47 = jnp.concatenate([v1[4:8], v1[4:8]], axis=0)
    blocks.append((a47 + jnp.where(sub < 4, v2[0:1], v2[1:2]),
                   (4 + lax.bitwise_and(sub, 3)) * k + lax.shift_right_logical(sub, 2)))
    blocks.append((jnp.where(sub == 0, v1[4:5] + v2[2:3], neg), sub + 4 * k + 2))
    vals = jnp.concatenate([b[0] for b in blocks], axis=0)
    ids = jnp.concatenate([b[1] for b in blocks], axis=0)
    return vals, ids


def _route_tokens(h, wq_ref, sk1_ref, sk2_ref):
    q = jnp.dot(h, wq_ref[...], preferred_element_type=F32).astype(BF16)
    half = PEER_QDIM // 2
    key_ids = lax.broadcasted_iota(jnp.int32, (PEER_KEYS, h.shape[0]), 0)
    experts, gates = [], []
    for hd in range(PEER_HEADS):
        q1 = q[:, hd * PEER_QDIM:hd * PEER_QDIM + half]
        q2 = q[:, hd * PEER_QDIM + half:(hd + 1) * PEER_QDIM]
        s1 = lax.dot_general(sk1_ref[...], q1, NT_DIMS, preferred_element_type=F32)
        s2 = lax.dot_general(sk2_ref[...], q2, NT_DIMS, preferred_element_type=F32)
        v1, i1 = _topk_rows(s1, key_ids, PEER_TOPK)
        v2, i2 = _topk_rows(s2, key_ids, PEER_TOPK)
        sc, ci = _topk_rows(*_candidate_blocks(v1, v2), PEER_TOPK)
        ca = lax.shift_right_logical(ci, int(math.log2(PEER_TOPK)))
        cb = lax.bitwise_and(ci, PEER_TOPK - 1)
        e1 = jnp.zeros_like(ci)
        e2 = jnp.zeros_like(ci)
        for a in range(PEER_TOPK):
            e1 = jnp.where(ca == a, i1[a:a + 1], e1)
            e2 = jnp.where(cb == a, i2[a:a + 1], e2)
        experts.append(e1 * PEER_KEYS + e2)
        e = jnp.exp(sc - sc[0:1])
        gates.append(e / jnp.sum(e, axis=0, keepdims=True))
    idx = jnp.concatenate(experts, axis=0).astype(F32).T.astype(jnp.int32)
    return idx, jnp.concatenate(gates, axis=0).T


PEER_TB = 32
PEER_ROUND = 8
PEER_SLOTS = 2 * PEER_ROUND
PEER_LOOKAHEAD = PEER_ROUND
ROW_TILES = D_MODEL // LANES
HALF_TILES = ROW_TILES // 2

def _gelu_exact(x):
    return 0.5 * x * (1.0 + lax.erf(x * (2.0 ** -0.5)))


def _peer_kernel(idx_ref, gate_ref, h_ref, uv_ref, g_ref, b_ref, y_ref, *scratch):
    bufs, (sem, ffn, pre, wgt) = scratch[:PEER_SLOTS], scratch[PEER_SLOTS:]
    i = pl.program_id(0)
    last = pl.num_programs(0) - 1
    tb = h_ref.shape[0]

    def issue(t, slot, part=None):
        per_part = PEER_PICKS // ROW_TILES
        picks = range(PEER_PICKS) if part is None else range(part * per_part, (part + 1) * per_part)
        for k in picks:
            dst = bufs[slot].at[:, k // SUBLANES, pl.ds(k % SUBLANES, 1), :]
            pltpu.make_async_copy(uv_ref.at[idx_ref[0, t, k]], dst, sem.at[slot]).start(priority=k % 2)

    def wait_slot(slot):
        pltpu.make_async_copy(bufs[slot], bufs[slot], sem.at[slot]).wait()

    def picks_tile(slot, c):
        return bufs[slot][c].reshape(PEER_PICKS, LANES)

    @pl.when(i == 0)
    def _():
        for t in range(PEER_LOOKAHEAD):
            issue(t, t)

    eye = (lax.broadcasted_iota(jnp.int32, (PEER_PICKS, PEER_PICKS), 0)
           == lax.broadcasted_iota(jnp.int32, (PEER_PICKS, PEER_PICKS), 1))

    def halves(words):
        lo = lax.bitcast_convert_type(lax.shift_left(words, jnp.int32(16)), F32)
        hi = lax.bitcast_convert_type(lax.bitwise_and(words, jnp.int32(-(1 << 16))), F32)
        return lo, hi

    def pre_activation(t, slot, between):
        h_row = h_ref[pl.ds(t, 1), :]
        acc = None
        for c in range(HALF_TILES):
            between(c)
            lo, hi = halves(picks_tile(slot, c))
            term = (lo * h_row[:, c * LANES:(c + 1) * LANES]
                    + hi * h_row[:, (HALF_TILES + c) * LANES:(HALF_TILES + c + 1) * LANES])
            acc = term if acc is None else acc + term
        a_col = jnp.sum(acc, axis=1, keepdims=True)
        return jnp.sum(jnp.where(eye, a_col, 0.0), axis=0, keepdims=True)

    def weighted_sum(t, slot, w_row, between):
        w_col = jnp.sum(jnp.where(eye, w_row, 0.0), axis=1, keepdims=True)
        out_lo, out_hi = [], []
        for c in range(HALF_TILES):
            between(HALF_TILES + c)
            lo, hi = halves(picks_tile(slot, HALF_TILES + c))
            out_lo.append(jnp.sum(w_col * lo, axis=0, keepdims=True))
            out_hi.append(jnp.sum(w_col * hi, axis=0, keepdims=True))
        ffn[pl.ds(t, 1), :] = jnp.concatenate(out_lo + out_hi, axis=1)

    def round_pair(rr, carry):
        for half in range(2):
            t0 = pl.multiple_of((2 * rr + half) * PEER_ROUND, PEER_ROUND)
            here, there = half * PEER_ROUND, (1 - half) * PEER_ROUND
            def issue_ahead(j):
                return lambda part: issue(t0 + PEER_ROUND + j, there + j, part)

            for j in range(PEER_ROUND):
                wait_slot(here + j)
                pre[j:j + 1, :] = pre_activation(t0 + j, here + j, issue_ahead(j))
            wgt[...] = gate_ref[pl.ds(t0, PEER_ROUND), :] * _gelu_exact(pre[...])
            for j in range(PEER_ROUND):
                weighted_sum(t0 + j, here + j, wgt[j:j + 1, :], issue_ahead(j))
        return carry

    lax.fori_loop(0, tb // PEER_SLOTS, round_pair, 0)

    @pl.when(i == last)
    def _():
        for t in range(PEER_LOOKAHEAD):
            wait_slot(t)

    y_ref[...] = _layer_norm(DEEPNORM_ALPHA * h_ref[...] + ffn[...], g_ref[...], b_ref[...])


def _peer(h, idx, gate, uv, ln_g, ln_b):
    n = h.shape[0]
    tb = min(PEER_TB, n)
    assert n % tb == 0 and tb % PEER_SLOTS == 0
    nb = n // tb
    row = lambda i: (i, 0)
    idx_pad = jnp.concatenate([idx, jnp.zeros((tb, PEER_PICKS), idx.dtype)], axis=0)
    idx_next = idx_pad[tb:].reshape(nb, tb, PEER_PICKS)[:, :PEER_LOOKAHEAD]
    idx_ext = jnp.concatenate([idx.reshape(nb, tb, PEER_PICKS), idx_next], axis=1)
    return pl.pallas_call(
        _peer_kernel,
        grid=(nb,),
        in_specs=[
            pl.BlockSpec((1, tb + PEER_LOOKAHEAD, PEER_PICKS), lambda i: (i, 0, 0), memory_space=pltpu.SMEM),
            pl.BlockSpec((tb, PEER_PICKS), row),
            pl.BlockSpec((tb, D_MODEL), row),
            pl.BlockSpec(memory_space=pl.ANY),
            _const_spec(ln_g.shape), _const_spec(ln_b.shape),
        ],
        out_specs=pl.BlockSpec((tb, D_MODEL), row),
        out_shape=jax.ShapeDtypeStruct((n, D_MODEL), F32),
        scratch_shapes=(
            [pltpu.VMEM((ROW_TILES, PEER_PICKS // SUBLANES, SUBLANES, LANES), jnp.int32) for _ in range(PEER_SLOTS)]
            + [pltpu.SemaphoreType.DMA((PEER_SLOTS,)), pltpu.VMEM((tb, D_MODEL), F32),
               pltpu.VMEM((PEER_ROUND, PEER_PICKS), F32), pltpu.VMEM((PEER_ROUND, PEER_PICKS), F32)]
        ),
        compiler_params=_compiler_params(("arbitrary",)),
    )(idx_ext, gate, h, uv, ln_g, ln_b)


def _rel_bias_pairs(rel_table, t, tk):
    dist = (tk - 1) - jnp.arange(tk + t - 1)
    line = jnp.take(rel_table, jnp.clip(dist, -A_REL_CLIP, A_REL_CLIP) + A_REL_CLIP, axis=1).astype(F32)
    bias = jnp.stack([line[:, t - 1 - q:t - 1 - q + tk] for q in range(t)], axis=1)
    return jnp.swapaxes(bias.reshape(A_HEADS // 2, 2 * t, tk), 1, 2)


PACK_ROWS = 256


def _pack_kernel(u_ref, v_ref, out_ref, stage, sem):
    i = pl.program_id(0)
    buf = lax.rem(i, 2)
    rows = u_ref.shape[0]
    half = D_MODEL // 2

    def drain(b):
        pltpu.make_async_copy(stage.at[b], stage.at[b], sem.at[b]).wait()

    @pl.when(i >= 2)
    def _():
        drain(buf)

    def words(x_ref):
        def bf16_bits(x):
            return lax.bitcast_convert_type(x.astype(BF16).astype(F32), jnp.int32)
        lo = lax.shift_right_logical(bf16_bits(x_ref[:, 0:half]), jnp.int32(16))
        hi = lax.bitwise_and(bf16_bits(x_ref[:, half:D_MODEL]), jnp.int32(-(1 << 16)))
        return lax.bitwise_or(hi, lo)

    for part, x_ref in enumerate((u_ref, v_ref)):
        w = words(x_ref)
        for c in range(HALF_TILES):
            tile = w[:, c * LANES:(c + 1) * LANES].reshape(rows // SUBLANES, SUBLANES, LANES)
            stage[buf, part * HALF_TILES + c] = tile

    def row_copy(r, carry):
        for s in range(SUBLANES):
            src = stage.at[buf, :, r, pl.ds(s, 1), :]
            pltpu.make_async_copy(src, out_ref.at[i * rows + r * SUBLANES + s], sem.at[buf]).start()
        return carry

    lax.fori_loop(0, rows // SUBLANES, row_copy, 0)

    @pl.when(i == pl.num_programs(0) - 1)
    def _():
        drain(buf)

        @pl.when(i >= 1)
        def _():
            drain(1 - buf)


def _pack_expert_tables(u, v):
    n = u.shape[0]
    rows = PACK_ROWS
    assert n % rows == 0
    return pl.pallas_call(
        _pack_kernel,
        grid=(n // rows,),
        in_specs=[pl.BlockSpec((rows, D_MODEL), lambda i: (i, 0)), pl.BlockSpec((rows, D_MODEL), lambda i: (i, 0))],
        out_specs=pl.BlockSpec(memory_space=pl.ANY),
        out_shape=jax.ShapeDtypeStruct((n, ROW_TILES, 1, LANES), jnp.int32),
        scratch_shapes=[pltpu.VMEM((2, ROW_TILES, rows // SUBLANES, SUBLANES, LANES), jnp.int32),
                        pltpu.SemaphoreType.DMA((2,))],
        compiler_params=_compiler_params(("arbitrary",)),
    )(u, v)


def _permute_in_proj(w_in):
    offs = np.cumsum([0, A_WIDTH, A_WIDTH, A_WIDTH, B_WIDTH, B_KV_WIDTH, B_KV_WIDTH, D_MODEL, D_MODEL])
    qa, ka, va, qb, kb, vb, ga, gb = [w_in[:, offs[s]:offs[s + 1]] for s in range(8)]
    return jnp.concatenate([ga, gb, qa, qb, ka, va, kb, vb], axis=1).astype(BF16)


def _split_kv(kv, batch, t):
    ka = kv[:, 0:A_WIDTH].reshape(batch, t, A_HEADS, HEAD_DIM)
    va = kv[:, A_WIDTH:2 * A_WIDTH].reshape(batch, t, A_HEADS, HEAD_DIM)
    kb = kv[:, 2 * A_WIDTH:2 * A_WIDTH + B_KV_WIDTH].reshape(batch, t, B_KV_HEADS, HEAD_DIM)
    vb = kv[:, 2 * A_WIDTH + B_KV_WIDTH:].reshape(batch, t, B_KV_HEADS, HEAD_DIM)
    return ka, va, kb, vb


def _cache_heads_kernel(k_ref, v_ref, ko_ref, vo_ref):
    for src, dst in ((k_ref, ko_ref), (v_ref, vo_ref)):
        for hd in range(A_HEADS):
            dst[0, :, hd, :] = src[:, hd * HEAD_DIM:(hd + 1) * HEAD_DIM]


def _cache_rows_a(kv, batch, seq, rows):
    assert seq % rows == 0
    per_seq = seq // rows
    out = jax.ShapeDtypeStruct((batch, rows, A_HEADS, HEAD_DIM), F32)
    out_spec = pl.BlockSpec((1, rows, A_HEADS, HEAD_DIM), lambda b: (b, 0, 0, 0))
    return pl.pallas_call(
        _cache_heads_kernel,
        grid=(batch,),
        in_specs=[pl.BlockSpec((rows, A_WIDTH), lambda b: (b * per_seq + per_seq - 1, 0)),
                  pl.BlockSpec((rows, A_WIDTH), lambda b: (b * per_seq + per_seq - 1, 1))],
        out_specs=[out_spec, out_spec],
        out_shape=[out, out],
        compiler_params=_compiler_params(("parallel",)),
    )(kv, kv)


def kernel(x_prompt, x_sample, cache_a_k, cache_a_v, cache_b_k, cache_b_v, w_in, rel_bias_table, b_sinks,
           w_proj_a, w_proj_b, w_out, ln1_g, ln1_b, peer_w_query, peer_sub_keys_1, peer_sub_keys_2,
           peer_u, peer_v, ln2_g, ln2_b):
    batch, seq, _ = x_prompt.shape
    dbatch, dseq, _ = x_sample.shape
    assert seq % ATT_BLOCK == 0 and seq >= ATT_BLOCK

    w_in_p = _permute_in_proj(w_in)
    wpa, wpb, wout = w_proj_a.astype(BF16), w_proj_b.astype(BF16), w_out.astype(BF16)
    wq = peer_w_query.astype(BF16)
    sk1, sk2 = peer_sub_keys_1.astype(BF16), peer_sub_keys_2.astype(BF16)
    uv = _pack_expert_tables(peer_u, peer_v)
    ln1 = (ln1_g.reshape(1, D_MODEL), ln1_b.reshape(1, D_MODEL))
    ln2 = (ln2_g.reshape(1, D_MODEL), ln2_b.reshape(1, D_MODEL))
    sinks = b_sinks.astype(F32)

    def tail(xf, ya, yb, p, tm):
        h, idx, gate = _merge_route(ya, yb, p, xf, wpa, wpb, wout, *ln1, wq, sk1, sk2, tm=tm)
        return _peer(h, idx, gate, uv, *ln2)

    ns = dbatch * dseq
    xs = x_sample.reshape(ns, D_MODEL)
    p_s, kv_s = _in_proj(xs, w_in_p, tm=ns)
    ka_s, va_s, kb_s, vb_s = _split_kv(kv_s, dbatch, dseq)

    def with_cache(cache, new_cols, width):
        new = p_s[:, new_cols:new_cols + width].reshape(dbatch, dseq, width)
        return jnp.concatenate([cache.reshape(dbatch, -1, width).astype(BF16), new], axis=1)

    kcat_a = with_cache(cache_a_k, COL_KA, A_WIDTH)
    vcat_a = with_cache(cache_a_v, COL_VA, A_WIDTH)
    kcat_b = with_cache(cache_b_k, COL_KB, B_KV_WIDTH)
    vcat_b = with_cache(cache_b_v, COL_VB, B_KV_WIDTH)
    bias_s = _rel_bias_pairs(rel_bias_table, dseq, kcat_a.shape[1])
    ya_s, yb_s = _attn_sample(p_s, kcat_a, vcat_a, kcat_b, vcat_b, bias_s, sinks, dbatch, dseq)
    y_sample = tail(xs, ya_s, yb_s, p_s, ns).reshape(dbatch, dseq, D_MODEL)

    xp = x_prompt.reshape(batch * seq, D_MODEL)
    p, kv = _in_proj(xp, w_in_p, tm=1024)
    bias_p = _rel_bias_pairs(rel_bias_table, CHUNK, (A_PREV_CHUNKS + 1) * CHUNK)
    ya, yb = _attn_prompt(p, bias_p, sinks, batch, seq)
    y_prompt = tail(xp, ya, yb, p, 256).reshape(batch, seq, D_MODEL)
    la = min(A_PREV_CHUNKS * CHUNK, seq)
    lb = min(B_PREV_CHUNKS * CHUNK, seq)
    ka_last, va_last = _cache_rows_a(kv, batch, seq, la)
    _, _, kb, vb = _split_kv(kv, batch, seq)

    return (y_prompt, y_sample, ka_last, va_last, kb[:, -lb:], vb[:, -lb:], ka_s, va_s, kb_s, vb_s)
```

```python
import math

import jax
import jax.numpy as jnp
import numpy as np
from jax import lax
from jax.experimental import pallas as pl
from jax.experimental.pallas import tpu as pltpu

F32 = jnp.float32
BF16 = jnp.bfloat16

D_MODEL = 2048
CHUNK = 64
HEAD_DIM = 64
A_HEADS = 16
A_PREV_CHUNKS = 8
A_REL_CLIP = 128
B_Q_HEADS = 16
B_KV_HEADS = 2
B_PREV_CHUNKS = 2
A_WIDTH = A_HEADS * HEAD_DIM
B_WIDTH = B_Q_HEADS * HEAD_DIM
B_KV_WIDTH = B_KV_HEADS * HEAD_DIM
PEER_HEADS = 8
PEER_KEYS = 128
PEER_QDIM = 256
PEER_TOPK = 16
PEER_PICKS = PEER_HEADS * PEER_TOPK
LN_EPS = 1e-5
DEEPNORM_ALPHA = 2.0 ** 0.25

LANES = 128
SUBLANES = 8
HEAD_PAIR = 2 * HEAD_DIM
VMEM_LIMIT = 56 * 1024 * 1024

COL_GA = 0
COL_GB = COL_GA + D_MODEL
COL_QA = COL_GB + D_MODEL
COL_QB = COL_QA + A_WIDTH
COL_KA = COL_QB + B_WIDTH
COL_VA = COL_KA + A_WIDTH
COL_KB = COL_VA + A_WIDTH
COL_VB = COL_KB + B_KV_WIDTH
IN_COLS = COL_VB + B_KV_WIDTH
KV_COLS = IN_COLS - COL_KA
PROJ_TN = 768
ATT_BLOCK = A_PREV_CHUNKS * CHUNK

NT_DIMS = (((1,), (1,)), ((), ()))
TN_DIMS = (((0,), (0,)), ((), ()))


def _compiler_params(semantics):
    return pltpu.CompilerParams(dimension_semantics=semantics, vmem_limit_bytes=VMEM_LIMIT)


def _const_spec(shape):
    zeros = (0,) * len(shape)
    return pl.BlockSpec(shape, lambda *_: zeros, pipeline_mode=pl.Buffered(1))


def _in_proj_kernel(x_ref, w_ref, p_ref, kv_ref, xb_ref):
    j = pl.program_id(1)

    @pl.when(j == 0)
    def _():
        xb_ref[...] = x_ref[...].astype(BF16)

    acc = jnp.dot(xb_ref[...], w_ref[...], preferred_element_type=F32)
    p_ref[...] = acc.astype(BF16)

    @pl.when(j >= COL_KA // PROJ_TN)
    def _():
        kv_ref[...] = acc


def _in_proj(x, w_bf16, tm):
    n = x.shape[0]
    first_kv = COL_KA // PROJ_TN
    return pl.pallas_call(
        _in_proj_kernel,
        grid=(n // tm, IN_COLS // PROJ_TN),
        in_specs=[
            pl.BlockSpec((tm, D_MODEL), lambda i, j: (i, 0)),
            pl.BlockSpec((D_MODEL, PROJ_TN), lambda i, j: (0, j)),
        ],
        out_specs=[
            pl.BlockSpec((tm, PROJ_TN), lambda i, j: (i, j)),
            pl.BlockSpec((tm, PROJ_TN), lambda i, j: (i, jnp.maximum(j - first_kv, 0))),
        ],
        out_shape=[
            jax.ShapeDtypeStruct((n, IN_COLS), BF16),
            jax.ShapeDtypeStruct((n, KV_COLS), F32),
        ],
        scratch_shapes=[pltpu.VMEM((tm, D_MODEL), BF16)],
        compiler_params=_compiler_params(("parallel", "arbitrary")),
    )(x, w_bf16)


def _softmax_pv(s, v2, sink):
    m = jnp.max(s, axis=0, keepdims=True)
    if sink is not None:
        m = jnp.maximum(m, sink)
    p = jnp.exp(s - m)
    den = jnp.sum(p, axis=0, keepdims=True)
    if sink is not None:
        den = den + jnp.exp(sink - m)
    p = (p * (1.0 / den)).astype(BF16)
    return lax.dot_general(p, v2, TN_DIMS, preferred_element_type=F32)


def _scores(qq, k2, bias, key_thresh):
    s = lax.dot_general(k2, qq, NT_DIMS, preferred_element_type=F32) * (HEAD_DIM ** -0.5) + bias
    if key_thresh is not None:
        key = lax.broadcasted_iota(jnp.int32, s.shape, 0)
        s = jnp.where(key >= key_thresh, s, -jnp.inf)
    return s


def _mixer_a(q_of, k_of, v_of, bias_of, key_thresh, store):
    for j in range(A_HEADS // 2):
        q2 = q_of(j)
        tq = q2.shape[0]
        lane_hi = lax.broadcasted_iota(jnp.int32, q2.shape, 1) >= HEAD_DIM
        zero = jnp.zeros_like(q2)
        qq = jnp.concatenate([jnp.where(lane_hi, zero, q2), jnp.where(lane_hi, q2, zero)], axis=0)
        o = _softmax_pv(_scores(qq, k_of(j), bias_of(j), key_thresh), v_of(j), None)
        store(j, jnp.where(lane_hi, o[tq:2 * tq], o[0:tq]))


def _alibi_bias(tq, tk):
    shape = (tk, B_Q_HEADS * tq)
    key = lax.broadcasted_iota(jnp.int32, shape, 0)
    lane = lax.broadcasted_iota(jnp.int32, shape, 1)
    head = lax.shift_right_logical(lane, int(math.log2(tq)))
    q = lax.bitwise_and(lane, tq - 1)
    slope = jnp.exp((head + 1).astype(F32) * (-8.0 / B_Q_HEADS * math.log(2.0)))
    return -slope * jnp.abs(q + (tk - tq) - key).astype(F32)


def _sink_row(sink_ref, tq):
    lane = lax.broadcasted_iota(jnp.int32, (1, B_Q_HEADS * tq), 1)
    head = lax.shift_right_logical(lane, int(math.log2(tq)))
    row = jnp.zeros(lane.shape, F32)
    for h in range(B_Q_HEADS):
        row = jnp.where(head == h, sink_ref[h], row)
    return row


def _mixer_b(q_of, k2, v2, bias, sink_row, key_thresh, store):
    group = B_Q_HEADS // B_KV_HEADS
    pieces = []
    for j in range(B_Q_HEADS // 2):
        q2 = q_of(j)
        tq = q2.shape[0]
        lane_hi = lax.broadcasted_iota(jnp.int32, q2.shape, 1) >= HEAD_DIM
        zero = jnp.zeros_like(q2)
        q2_swapped = pltpu.roll(q2.astype(F32), HEAD_DIM, 1).astype(BF16)
        for hh in range(2):
            kv = (2 * j + hh) // group
            src = q2 if hh == kv else q2_swapped
            pieces.append(jnp.where(lane_hi, src, zero) if kv else jnp.where(lane_hi, zero, src))
    qq = jnp.concatenate(pieces, axis=0)
    o = _softmax_pv(_scores(qq, k2, bias, key_thresh), v2, sink_row)
    lane_hi = lax.broadcasted_iota(jnp.int32, (tq, HEAD_PAIR), 1) >= HEAD_DIM
    for j in range(B_Q_HEADS // 2):
        halves = []
        for hh in range(2):
            h = 2 * j + hh
            o_h = o[h * tq:(h + 1) * tq]
            halves.append(o_h if hh == h // group else pltpu.roll(o_h, HEAD_DIM, 1))
        store(j, jnp.where(lane_hi, halves[1], halves[0]))


def _attn_prompt_kernel(sink_ref, qa_ref, kap_ref, kac_ref, vap_ref, vac_ref,
                        qb_ref, kbp_ref, kbc_ref, vbp_ref, vbc_ref, bias_ref,
                        ya_ref, yb_ref, ka_s, va_s, kb_s, vb_s, bias_b_s, sink_s):
    i = pl.program_id(1)
    blk = ATT_BLOCK
    ka_s[0:blk, :] = kap_ref[...]
    ka_s[blk:2 * blk, :] = kac_ref[...]
    va_s[0:blk, :] = vap_ref[...]
    va_s[blk:2 * blk, :] = vac_ref[...]
    kb_s[0:blk, :] = kbp_ref[...]
    kb_s[blk:2 * blk, :] = kbc_ref[...]
    vb_s[0:blk, :] = vbp_ref[...]
    vb_s[blk:2 * blk, :] = vbc_ref[...]

    tk_a = (A_PREV_CHUNKS + 1) * CHUNK
    tk_b = (B_PREV_CHUNKS + 1) * CHUNK
    b_off = blk - B_PREV_CHUNKS * CHUNK
    bias_b_s[...] = _alibi_bias(CHUNK, tk_b)
    sink_s[...] = _sink_row(sink_ref, CHUNK)

    def chunk_body(c, carry):
        row0 = pl.multiple_of(c * CHUNK, CHUNK)
        thresh_a = jnp.where(i > 0, 0, blk - c * CHUNK)
        thresh_b = jnp.where(i > 0, 0, B_PREV_CHUNKS * CHUNK - c * CHUNK)

        def store_a(j, val):
            ya_ref[pl.ds(row0, CHUNK), j * HEAD_PAIR:(j + 1) * HEAD_PAIR] = val.astype(ya_ref.dtype)

        def store_b(j, val):
            yb_ref[pl.ds(row0, CHUNK), j * HEAD_PAIR:(j + 1) * HEAD_PAIR] = val.astype(yb_ref.dtype)

        _mixer_a(
            lambda j: qa_ref[pl.ds(row0, CHUNK), j * HEAD_PAIR:(j + 1) * HEAD_PAIR],
            lambda j: ka_s[pl.ds(row0, tk_a), j * HEAD_PAIR:(j + 1) * HEAD_PAIR],
            lambda j: va_s[pl.ds(row0, tk_a), j * HEAD_PAIR:(j + 1) * HEAD_PAIR],
            lambda j: bias_ref[j], thresh_a, store_a)

        rowb = pl.multiple_of(b_off + c * CHUNK, CHUNK)
        _mixer_b(
            lambda j: qb_ref[pl.ds(row0, CHUNK), j * HEAD_PAIR:(j + 1) * HEAD_PAIR],
            kb_s[pl.ds(rowb, tk_b), :], vb_s[pl.ds(rowb, tk_b), :],
            bias_b_s[...], sink_s[...], thresh_b, store_b)
        return carry

    lax.fori_loop(0, blk // CHUNK, chunk_body, 0)


def _attn_prompt(p, bias_a, sinks, batch, seq):
    blk = ATT_BLOCK
    nb = seq // blk
    n = batch * seq

    def cur(col_block):
        return lambda b, i: (b * nb + i, col_block)

    def prev(col_block):
        return lambda b, i: (b * nb + jnp.maximum(i - 1, 0), col_block)

    wa, wb = A_WIDTH, B_KV_WIDTH
    in_specs = [
        pl.BlockSpec(memory_space=pltpu.SMEM),
        pl.BlockSpec((blk, wa), cur(COL_QA // wa)),
        pl.BlockSpec((blk, wa), prev(COL_KA // wa)),
        pl.BlockSpec((blk, wa), cur(COL_KA // wa)),
        pl.BlockSpec((blk, wa), prev(COL_VA // wa)),
        pl.BlockSpec((blk, wa), cur(COL_VA // wa)),
        pl.BlockSpec((blk, wa), cur(COL_QB // wa)),
        pl.BlockSpec((blk, wb), prev(COL_KB // wb)),
        pl.BlockSpec((blk, wb), cur(COL_KB // wb)),
        pl.BlockSpec((blk, wb), prev(COL_VB // wb)),
        pl.BlockSpec((blk, wb), cur(COL_VB // wb)),
        _const_spec(bias_a.shape),
    ]
    out_spec = pl.BlockSpec((blk, wa), lambda b, i: (b * nb + i, 0))
    rows_b = B_Q_HEADS * CHUNK
    return pl.pallas_call(
        _attn_prompt_kernel,
        grid=(batch, nb),
        in_specs=in_specs,
        out_specs=[out_spec, out_spec],
        out_shape=[jax.ShapeDtypeStruct((n, A_WIDTH), BF16), jax.ShapeDtypeStruct((n, B_WIDTH), BF16)],
        scratch_shapes=[
            pltpu.VMEM((2 * blk, wa), BF16), pltpu.VMEM((2 * blk, wa), BF16),
            pltpu.VMEM((2 * blk, wb), BF16), pltpu.VMEM((2 * blk, wb), BF16),
            pltpu.VMEM(((B_PREV_CHUNKS + 1) * CHUNK, rows_b), F32), pltpu.VMEM((1, rows_b), F32),
        ],
        compiler_params=_compiler_params(("parallel", "arbitrary")),
    )(sinks, p, p, p, p, p, p, p, p, p, p, bias_a)


def _attn_sample_kernel(sink_ref, qa_ref, ka_ref, va_ref, qb_ref, kb_ref, vb_ref, bias_ref, ya_ref, yb_ref):
    t, tk_b = qb_ref.shape[0], kb_ref.shape[1]

    def store_a(j, val):
        ya_ref[:, j * HEAD_PAIR:(j + 1) * HEAD_PAIR] = val.astype(ya_ref.dtype)

    def store_b(j, val):
        yb_ref[:, j * HEAD_PAIR:(j + 1) * HEAD_PAIR] = val.astype(yb_ref.dtype)

    _mixer_a(
        lambda j: qa_ref[:, j * HEAD_PAIR:(j + 1) * HEAD_PAIR],
        lambda j: ka_ref[0, :, j * HEAD_PAIR:(j + 1) * HEAD_PAIR],
        lambda j: va_ref[0, :, j * HEAD_PAIR:(j + 1) * HEAD_PAIR],
        lambda j: bias_ref[j], None, store_a)

    _mixer_b(
        lambda j: qb_ref[:, j * HEAD_PAIR:(j + 1) * HEAD_PAIR],
        kb_ref[0], vb_ref[0], _alibi_bias(t, tk_b), _sink_row(sink_ref, t), None, store_b)


def _attn_sample(p, kcat_a, vcat_a, kcat_b, vcat_b, bias_a, sinks, batch, t):
    n = batch * t
    wa, wb = A_WIDTH, B_KV_WIDTH
    tk_a, tk_b = kcat_a.shape[1], kcat_b.shape[1]
    in_specs = [
        pl.BlockSpec(memory_space=pltpu.SMEM),
        pl.BlockSpec((t, wa), lambda b: (b, COL_QA // wa)),
        pl.BlockSpec((1, tk_a, wa), lambda b: (b, 0, 0)),
        pl.BlockSpec((1, tk_a, wa), lambda b: (b, 0, 0)),
        pl.BlockSpec((t, wa), lambda b: (b, COL_QB // wa)),
        pl.BlockSpec((1, tk_b, wb), lambda b: (b, 0, 0)),
        pl.BlockSpec((1, tk_b, wb), lambda b: (b, 0, 0)),
        _const_spec(bias_a.shape),
    ]
    out_spec = pl.BlockSpec((t, wa), lambda b: (b, 0))
    return pl.pallas_call(
        _attn_sample_kernel,
        grid=(batch,),
        in_specs=in_specs,
        out_specs=[out_spec, out_spec],
        out_shape=[jax.ShapeDtypeStruct((n, A_WIDTH), BF16), jax.ShapeDtypeStruct((n, B_WIDTH), BF16)],
        compiler_params=_compiler_params(("parallel",)),
    )(sinks, p, kcat_a, vcat_a, p, kcat_b, vcat_b, bias_a)


def _layer_norm(x, g, b):
    mu = jnp.mean(x, axis=-1, keepdims=True)
    xc = x - mu
    var = jnp.mean(xc * xc, axis=-1, keepdims=True)
    return xc * lax.rsqrt(var + LN_EPS) * g + b


def _merge_route_kernel(ya_ref, yb_ref, ga_ref, gb_ref, x_ref, wpa_ref, wpb_ref, wout_ref, g_ref, b_ref,
                        wq_ref, sk1_ref, sk2_ref, h_ref, idx_ref, gate_ref):
    za = jnp.dot(ya_ref[...], wpa_ref[...], preferred_element_type=F32)
    zb = jnp.dot(yb_ref[...], wpb_ref[...], preferred_element_type=F32)
    m = jax.nn.sigmoid(ga_ref[...].astype(F32)) * za + jax.nn.sigmoid(gb_ref[...].astype(F32)) * zb
    r = jnp.dot(m.astype(BF16), wout_ref[...], preferred_element_type=F32)
    h_ref[...] = _layer_norm(DEEPNORM_ALPHA * x_ref[...] + r, g_ref[...], b_ref[...])
    for t0 in range(0, h_ref.shape[0], ROUTE_TM):
        q = _route_query(h_ref[t0:t0 + ROUTE_TM, :].astype(BF16), wq_ref)
        routed = [_route_head(q, hd, sk1_ref, sk2_ref) for hd in range(PEER_HEADS)]
        experts = jnp.concatenate([e for e, _ in routed], axis=0)
        gates = jnp.concatenate([g for _, g in routed], axis=0)
        idx_ref[t0:t0 + ROUTE_TM, :] = experts.astype(F32).T.astype(jnp.int32)
        gate_ref[t0:t0 + ROUTE_TM, :] = gates.T


def _merge_route(ya, yb, p, x, wpa, wpb, wout, ln_g, ln_b, wq, sk1, sk2, tm):
    n = x.shape[0]
    assert n % tm == 0 and tm % ROUTE_TM == 0
    row = lambda i: (i, 0)
    return pl.pallas_call(
        _merge_route_kernel,
        grid=(n // tm,),
        in_specs=[
            pl.BlockSpec((tm, A_WIDTH), row),
            pl.BlockSpec((tm, B_WIDTH), row),
            pl.BlockSpec((tm, D_MODEL), lambda i: (i, COL_GA // D_MODEL)),
            pl.BlockSpec((tm, D_MODEL), lambda i: (i, COL_GB // D_MODEL)),
            pl.BlockSpec((tm, D_MODEL), row),
            _const_spec(wpa.shape), _const_spec(wpb.shape), _const_spec(wout.shape),
            _const_spec(ln_g.shape), _const_spec(ln_b.shape),
            _const_spec(wq.shape), _const_spec(sk1.shape), _const_spec(sk2.shape),
        ],
        out_specs=[pl.BlockSpec((tm, D_MODEL), row), pl.BlockSpec((tm, PEER_PICKS), row),
                   pl.BlockSpec((tm, PEER_PICKS), row)],
        out_shape=[jax.ShapeDtypeStruct((n, D_MODEL), F32), jax.ShapeDtypeStruct((n, PEER_PICKS), jnp.int32),
                   jax.ShapeDtypeStruct((n, PEER_PICKS), F32)],
        compiler_params=_compiler_params(("parallel",)),
    )(ya, yb, p, p, x, wpa, wpb, wout, ln_g, ln_b, wq, sk1, sk2)


ROUTE_TM = 128
ID_SENTINEL = 1 << 30


def _topk_rows(s, ids, k):
    vals, picked = [], []
    for _ in range(k):
        m = jnp.max(s, axis=0, keepdims=True)
        ix = jnp.min(jnp.where(s == m, ids, ID_SENTINEL), axis=0, keepdims=True)
        s = jnp.where(ids == ix, -jnp.inf, s)
        vals.append(m)
        picked.append(ix)
    return jnp.concatenate(vals, axis=0), jnp.concatenate(picked, axis=0)


def _sorting_network(n):
    def merge(lo, hi, r):
        step = r * 2
        if step < hi - lo:
            yield from merge(lo, hi, step)
            yield from merge(lo + r, hi, step)
            yield from [(i, i + r) for i in range(lo + r, hi - r, step)]
        else:
            yield (lo, lo + r)

    def sort(lo, hi):
        if hi - lo >= 1:
            mid = lo + (hi - lo) // 2
            yield from sort(lo, mid)
            yield from sort(mid + 1, hi)
            yield from merge(lo, hi, 1)

    return list(sort(0, n - 1))


def _topk_keys(s, k):
    wires = s.shape[0] // SUBLANES
    assert wires == k
    sub = lax.broadcasted_iota(jnp.int32, (SUBLANES, s.shape[1]), 0)
    v = [s[g * SUBLANES:(g + 1) * SUBLANES] for g in range(wires)]
    ids = [sub + g * SUBLANES for g in range(wires)]
    for a, b in _sorting_network(wires):
        first = (v[a] > v[b]) | ((v[a] == v[b]) & (ids[a] < ids[b]))
        v[a], v[b] = jnp.where(first, v[a], v[b]), jnp.where(first, v[b], v[a])
        ids[a], ids[b] = jnp.where(first, ids[a], ids[b]), jnp.where(first, ids[b], ids[a])
    vals, picked = [], []
    for t in range(k):
        m = jnp.max(v[0], axis=0, keepdims=True)
        ix = jnp.min(jnp.where(v[0] == m, ids[0], ID_SENTINEL), axis=0, keepdims=True)
        vals.append(m)
        picked.append(ix)
        hit = ids[0] == ix
        for w in range(k - 1 - t):
            v[w] = jnp.where(hit, v[w + 1], v[w])
            ids[w] = jnp.where(hit, ids[w + 1], ids[w])
    return jnp.concatenate(vals, axis=0), jnp.concatenate(picked, axis=0)


def _candidate_blocks(v1, v2):
    t = v1.shape[1]
    sub = lax.broadcasted_iota(jnp.int32, (SUBLANES, t), 0)
    neg = jnp.full((SUBLANES, t), -jnp.inf, F32)
    k = PEER_TOPK
    blocks = []
    blocks.append((v1[0:1] + v2[0:8], sub))
    blocks.append((v1[0:1] + v2[8:16], sub + 8))
    blocks.append((v1[1:2] + v2[0:8], sub + k))
    blocks.append((jnp.where(sub < 5, v1[2:3] + v2[0:8], neg), sub + 2 * k))
    blocks.append((jnp.where(sub < 4, v1[3:4] + v2[0:8], neg), sub + 3 * k))
    blocks.append((v1[8:16] + v2[0:1], (sub + 8) * k))
    a47 = jnp.concatenate([v1[4:8], v1[4:8]], axis=0)
    blocks.append((a47 + jnp.where(sub < 4, v2[0:1], v2[1:2]),
                   (4 + lax.bitwise_and(sub, 3)) * k + lax.shift_right_logical(sub, 2)))
    blocks.append((jnp.where(sub == 0, v1[4:5] + v2[2:3], neg), sub + 4 * k + 2))
    vals = jnp.concatenate([b[0] for b in blocks], axis=0)
    ids = jnp.concatenate([b[1] for b in blocks], axis=0)
    return vals, ids


def _route_query(h, wq_ref):
    return jnp.dot(h, wq_ref[...], preferred_element_type=F32).astype(BF16)


def _route_head(q, hd, sk1_ref, sk2_ref):
    half = PEER_QDIM // 2
    q1 = q[:, hd * PEER_QDIM:hd * PEER_QDIM + half]
    q2 = q[:, hd * PEER_QDIM + half:(hd + 1) * PEER_QDIM]
    s1 = lax.dot_general(sk1_ref[...], q1, NT_DIMS, preferred_element_type=F32)
    s2 = lax.dot_general(sk2_ref[...], q2, NT_DIMS, preferred_element_type=F32)
    v1, i1 = _topk_keys(s1, PEER_TOPK)
    v2, i2 = _topk_keys(s2, PEER_TOPK)
    sc, ci = _topk_rows(*_candidate_blocks(v1, v2), PEER_TOPK)
    ca = lax.shift_right_logical(ci, int(math.log2(PEER_TOPK)))
    cb = lax.bitwise_and(ci, PEER_TOPK - 1)
    e1 = jnp.zeros_like(ci)
    e2 = jnp.zeros_like(ci)
    for a in range(PEER_TOPK):
        e1 = jnp.where(ca == a, i1[a:a + 1], e1)
        e2 = jnp.where(cb == a, i2[a:a + 1], e2)
    e = jnp.exp(sc - sc[0:1])
    return e1 * PEER_KEYS + e2, e / jnp.sum(e, axis=0, keepdims=True)


PEER_TB = 32
PEER_ROUND = 8
PEER_SLOTS = 2 * PEER_ROUND
PEER_LOOKAHEAD = PEER_ROUND
ROW_TILES = D_MODEL // LANES
HALF_TILES = ROW_TILES // 2

def _gelu_exact(x):
    return 0.5 * x * (1.0 + lax.erf(x * (2.0 ** -0.5)))


def _peer_kernel(idx_ref, gate_ref, h_ref, uv_ref, g_ref, b_ref, y_ref, *scratch):
    bufs, (sem, ffn, pre, wgt) = scratch[:PEER_SLOTS], scratch[PEER_SLOTS:]
    i = pl.program_id(0)
    last = pl.num_programs(0) - 1
    tb = h_ref.shape[0]

    def issue(t, slot, part=None):
        per_part = PEER_PICKS // ROW_TILES
        picks = range(PEER_PICKS) if part is None else range(part * per_part, (part + 1) * per_part)
        for k in picks:
            dst = bufs[slot].at[:, k // SUBLANES, pl.ds(k % SUBLANES, 1), :]
            pltpu.make_async_copy(uv_ref.at[idx_ref[0, t, k]], dst, sem.at[slot]).start(priority=k % 2)

    def wait_slot(slot):
        pltpu.make_async_copy(bufs[slot], bufs[slot], sem.at[slot]).wait()

    def picks_tile(slot, c):
        return bufs[slot][c].reshape(PEER_PICKS, LANES)

    @pl.when(i == 0)
    def _():
        for t in range(PEER_LOOKAHEAD):
            issue(t, t)

    eye = (lax.broadcasted_iota(jnp.int32, (PEER_PICKS, PEER_PICKS), 0)
           == lax.broadcasted_iota(jnp.int32, (PEER_PICKS, PEER_PICKS), 1))

    def halves(words):
        lo = lax.bitcast_convert_type(lax.shift_left(words, jnp.int32(16)), F32)
        hi = lax.bitcast_convert_type(lax.bitwise_and(words, jnp.int32(-(1 << 16))), F32)
        return lo, hi

    def pre_activation(t, slot, between):
        h_row = h_ref[pl.ds(t, 1), :]
        acc = None
        for c in range(HALF_TILES):
            between(c)
            lo, hi = halves(picks_tile(slot, c))
            term = (lo * h_row[:, c * LANES:(c + 1) * LANES]
                    + hi * h_row[:, (HALF_TILES + c) * LANES:(HALF_TILES + c + 1) * LANES])
            acc = term if acc is None else acc + term
        a_col = jnp.sum(acc, axis=1, keepdims=True)
        return jnp.sum(jnp.where(eye, a_col, 0.0), axis=0, keepdims=True)

    def weighted_sum(t, slot, w_row, between):
        w_col = jnp.sum(jnp.where(eye, w_row, 0.0), axis=1, keepdims=True)
        out_lo, out_hi = [], []
        for c in range(HALF_TILES):
            between(HALF_TILES + c)
            lo, hi = halves(picks_tile(slot, HALF_TILES + c))
            out_lo.append(jnp.sum(w_col * lo, axis=0, keepdims=True))
            out_hi.append(jnp.sum(w_col * hi, axis=0, keepdims=True))
        ffn[pl.ds(t, 1), :] = jnp.concatenate(out_lo + out_hi, axis=1)

    def round_pair(rr, carry):
        for half in range(2):
            t0 = pl.multiple_of((2 * rr + half) * PEER_ROUND, PEER_ROUND)
            here, there = half * PEER_ROUND, (1 - half) * PEER_ROUND
            def issue_ahead(j):
                return lambda part: issue(t0 + PEER_ROUND + j, there + j, part)

            for j in range(PEER_ROUND):
                wait_slot(here + j)
                pre[j:j + 1, :] = pre_activation(t0 + j, here + j, issue_ahead(j))
            wgt[...] = gate_ref[pl.ds(t0, PEER_ROUND), :] * _gelu_exact(pre[...])
            for j in range(PEER_ROUND):
                weighted_sum(t0 + j, here + j, wgt[j:j + 1, :], issue_ahead(j))
        return carry

    lax.fori_loop(0, tb // PEER_SLOTS, round_pair, 0)

    @pl.when(i == last)
    def _():
        for t in range(PEER_LOOKAHEAD):
            wait_slot(t)

    y_ref[...] = _layer_norm(DEEPNORM_ALPHA * h_ref[...] + ffn[...], g_ref[...], b_ref[...])


def _peer(h, idx, gate, uv, ln_g, ln_b):
    n = h.shape[0]
    tb = min(PEER_TB, n)
    assert n % tb == 0 and tb % PEER_SLOTS == 0
    nb = n // tb
    row = lambda i: (i, 0)
    idx_pad = jnp.concatenate([idx, jnp.zeros((tb, PEER_PICKS), idx.dtype)], axis=0)
    idx_next = idx_pad[tb:].reshape(nb, tb, PEER_PICKS)[:, :PEER_LOOKAHEAD]
    idx_ext = jnp.concatenate([idx.reshape(nb, tb, PEER_PICKS), idx_next], axis=1)
    return pl.pallas_call(
        _peer_kernel,
        grid=(nb,),
        in_specs=[
            pl.BlockSpec((1, tb + PEER_LOOKAHEAD, PEER_PICKS), lambda i: (i, 0, 0), memory_space=pltpu.SMEM),
            pl.BlockSpec((tb, PEER_PICKS), row),
            pl.BlockSpec((tb, D_MODEL), row),
            pl.BlockSpec(memory_space=pl.ANY),
            _const_spec(ln_g.shape), _const_spec(ln_b.shape),
        ],
        out_specs=pl.BlockSpec((tb, D_MODEL), row),
        out_shape=jax.ShapeDtypeStruct((n, D_MODEL), F32),
        scratch_shapes=(
            [pltpu.VMEM((ROW_TILES, PEER_PICKS // SUBLANES, SUBLANES, LANES), jnp.int32) for _ in range(PEER_SLOTS)]
            + [pltpu.SemaphoreType.DMA((PEER_SLOTS,)), pltpu.VMEM((tb, D_MODEL), F32),
               pltpu.VMEM((PEER_ROUND, PEER_PICKS), F32), pltpu.VMEM((PEER_ROUND, PEER_PICKS), F32)]
        ),
        compiler_params=_compiler_params(("arbitrary",)),
    )(idx_ext, gate, h, uv, ln_g, ln_b)


def _rel_bias_pairs(rel_table, t, tk):
    dist = (tk - 1) - jnp.arange(tk + t - 1)
    line = jnp.take(rel_table, jnp.clip(dist, -A_REL_CLIP, A_REL_CLIP) + A_REL_CLIP, axis=1).astype(F32)
    bias = jnp.stack([line[:, t - 1 - q:t - 1 - q + tk] for q in range(t)], axis=1)
    return jnp.swapaxes(bias.reshape(A_HEADS // 2, 2 * t, tk), 1, 2)


PACK_ROWS = 256


def _pack_kernel(u_ref, v_ref, out_ref, stage, sem):
    i = pl.program_id(0)
    buf = lax.rem(i, 2)
    rows = u_ref.shape[0]
    half = D_MODEL // 2

    def drain(b):
        pltpu.make_async_copy(stage.at[b], stage.at[b], sem.at[b]).wait()

    @pl.when(i >= 2)
    def _():
        drain(buf)

    def words(x_ref):
        def bf16_bits(x):
            return lax.bitcast_convert_type(x.astype(BF16).astype(F32), jnp.int32)
        lo = lax.shift_right_logical(bf16_bits(x_ref[:, 0:half]), jnp.int32(16))
        hi = lax.bitwise_and(bf16_bits(x_ref[:, half:D_MODEL]), jnp.int32(-(1 << 16)))
        return lax.bitwise_or(hi, lo)

    for part, x_ref in enumerate((u_ref, v_ref)):
        w = words(x_ref)
        for c in range(HALF_TILES):
            tile = w[:, c * LANES:(c + 1) * LANES].reshape(rows // SUBLANES, SUBLANES, LANES)
            stage[buf, part * HALF_TILES + c] = tile

    def row_copy(r, carry):
        for s in range(SUBLANES):
            src = stage.at[buf, :, r, pl.ds(s, 1), :]
            pltpu.make_async_copy(src, out_ref.at[i * rows + r * SUBLANES + s], sem.at[buf]).start()
        return carry

    lax.fori_loop(0, rows // SUBLANES, row_copy, 0)

    @pl.when(i == pl.num_programs(0) - 1)
    def _():
        drain(buf)

        @pl.when(i >= 1)
        def _():
            drain(1 - buf)


def _pack_expert_tables(u, v):
    n = u.shape[0]
    rows = PACK_ROWS
    assert n % rows == 0
    return pl.pallas_call(
        _pack_kernel,
        grid=(n // rows,),
        in_specs=[pl.BlockSpec((rows, D_MODEL), lambda i: (i, 0)), pl.BlockSpec((rows, D_MODEL), lambda i: (i, 0))],
        out_specs=pl.BlockSpec(memory_space=pl.ANY),
        out_shape=jax.ShapeDtypeStruct((n, ROW_TILES, 1, LANES), jnp.int32),
        scratch_shapes=[pltpu.VMEM((2, ROW_TILES, rows // SUBLANES, SUBLANES, LANES), jnp.int32),
                        pltpu.SemaphoreType.DMA((2,))],
        compiler_params=_compiler_params(("arbitrary",)),
    )(u, v)


def _permute_in_proj(w_in):
    offs = np.cumsum([0, A_WIDTH, A_WIDTH, A_WIDTH, B_WIDTH, B_KV_WIDTH, B_KV_WIDTH, D_MODEL, D_MODEL])
    qa, ka, va, qb, kb, vb, ga, gb = [w_in[:, offs[s]:offs[s + 1]] for s in range(8)]
    return jnp.concatenate([ga, gb, qa, qb, ka, va, kb, vb], axis=1).astype(BF16)


def _split_kv(kv, batch, t):
    ka = kv[:, 0:A_WIDTH].reshape(batch, t, A_HEADS, HEAD_DIM)
    va = kv[:, A_WIDTH:2 * A_WIDTH].reshape(batch, t, A_HEADS, HEAD_DIM)
    kb = kv[:, 2 * A_WIDTH:2 * A_WIDTH + B_KV_WIDTH].reshape(batch, t, B_KV_HEADS, HEAD_DIM)
    vb = kv[:, 2 * A_WIDTH + B_KV_WIDTH:].reshape(batch, t, B_KV_HEADS, HEAD_DIM)
    return ka, va, kb, vb


def _cache_heads_kernel(k_ref, v_ref, ko_ref, vo_ref):
    for src, dst in ((k_ref, ko_ref), (v_ref, vo_ref)):
        for hd in range(A_HEADS):
            dst[0, :, hd, :] = src[:, hd * HEAD_DIM:(hd + 1) * HEAD_DIM]


def _cache_rows_a(kv, batch, seq, rows):
    assert seq % rows == 0
    per_seq = seq // rows
    out = jax.ShapeDtypeStruct((batch, rows, A_HEADS, HEAD_DIM), F32)
    out_spec = pl.BlockSpec((1, rows, A_HEADS, HEAD_DIM), lambda b: (b, 0, 0, 0))
    return pl.pallas_call(
        _cache_heads_kernel,
        grid=(batch,),
        in_specs=[pl.BlockSpec((rows, A_WIDTH), lambda b: (b * per_seq + per_seq - 1, 0)),
                  pl.BlockSpec((rows, A_WIDTH), lambda b: (b * per_seq + per_seq - 1, 1))],
        out_specs=[out_spec, out_spec],
        out_shape=[out, out],
        compiler_params=_compiler_params(("parallel",)),
    )(kv, kv)


def kernel(x_prompt, x_sample, cache_a_k, cache_a_v, cache_b_k, cache_b_v, w_in, rel_bias_table, b_sinks,
           w_proj_a, w_proj_b, w_out, ln1_g, ln1_b, peer_w_query, peer_sub_keys_1, peer_sub_keys_2,
           peer_u, peer_v, ln2_g, ln2_b):
    batch, seq, _ = x_prompt.shape
    dbatch, dseq, _ = x_sample.shape
    assert seq % ATT_BLOCK == 0 and seq >= ATT_BLOCK

    w_in_p = _permute_in_proj(w_in)
    wpa, wpb, wout = w_proj_a.astype(BF16), w_proj_b.astype(BF16), w_out.astype(BF16)
    wq = peer_w_query.astype(BF16)
    sk1, sk2 = peer_sub_keys_1.astype(BF16), peer_sub_keys_2.astype(BF16)
    uv = _pack_expert_tables(peer_u, peer_v)
    ln1 = (ln1_g.reshape(1, D_MODEL), ln1_b.reshape(1, D_MODEL))
    ln2 = (ln2_g.reshape(1, D_MODEL), ln2_b.reshape(1, D_MODEL))
    sinks = b_sinks.astype(F32)

    def tail(xf, ya, yb, p, tm):
        h, idx, gate = _merge_route(ya, yb, p, xf, wpa, wpb, wout, *ln1, wq, sk1, sk2, tm=tm)
        return _peer(h, idx, gate, uv, *ln2)

    ns = dbatch * dseq
    xs = x_sample.reshape(ns, D_MODEL)
    p_s, kv_s = _in_proj(xs, w_in_p, tm=ns)
    ka_s, va_s, kb_s, vb_s = _split_kv(kv_s, dbatch, dseq)

    def with_cache(cache, new_cols, width):
        new = p_s[:, new_cols:new_cols + width].reshape(dbatch, dseq, width)
        return jnp.concatenate([cache.reshape(dbatch, -1, width).astype(BF16), new], axis=1)

    kcat_a = with_cache(cache_a_k, COL_KA, A_WIDTH)
    vcat_a = with_cache(cache_a_v, COL_VA, A_WIDTH)
    kcat_b = with_cache(cache_b_k, COL_KB, B_KV_WIDTH)
    vcat_b = with_cache(cache_b_v, COL_VB, B_KV_WIDTH)
    bias_s = _rel_bias_pairs(rel_bias_table, dseq, kcat_a.shape[1])
    ya_s, yb_s = _attn_sample(p_s, kcat_a, vcat_a, kcat_b, vcat_b, bias_s, sinks, dbatch, dseq)
    y_sample = tail(xs, ya_s, yb_s, p_s, ns).reshape(dbatch, dseq, D_MODEL)

    xp = x_prompt.reshape(batch * seq, D_MODEL)
    p, kv = _in_proj(xp, w_in_p, tm=1024)
    bias_p = _rel_bias_pairs(rel_bias_table, CHUNK, (A_PREV_CHUNKS + 1) * CHUNK)
    ya, yb = _attn_prompt(p, bias_p, sinks, batch, seq)
    y_prompt = tail(xp, ya, yb, p, 256).reshape(batch, seq, D_MODEL)
    la = min(A_PREV_CHUNKS * CHUNK, seq)
    lb = min(B_PREV_CHUNKS * CHUNK, seq)
    ka_last, va_last = _cache_rows_a(kv, batch, seq, la)
    _, _, kb, vb = _split_kv(kv, batch, seq)

    return (y_prompt, y_sample, ka_last, va_last, kb[:, -lb:], vb[:, -lb:], ka_s, va_s, kb_s, vb_s)
```

```python
import math

import jax
import jax.numpy as jnp
import numpy as np
from jax import lax
from jax.experimental import pallas as pl
from jax.experimental.pallas import tpu as pltpu

F32 = jnp.float32
BF16 = jnp.bfloat16

D_MODEL = 2048
CHUNK = 64
HEAD_DIM = 64
A_HEADS = 16
A_PREV_CHUNKS = 8
A_REL_CLIP = 128
B_Q_HEADS = 16
B_KV_HEADS = 2
B_PREV_CHUNKS = 2
A_WIDTH = A_HEADS * HEAD_DIM
B_WIDTH = B_Q_HEADS * HEAD_DIM
B_KV_WIDTH = B_KV_HEADS * HEAD_DIM
PEER_HEADS = 8
PEER_KEYS = 128
PEER_QDIM = 256
PEER_TOPK = 16
PEER_PICKS = PEER_HEADS * PEER_TOPK
LN_EPS = 1e-5
DEEPNORM_ALPHA = 2.0 ** 0.25

LANES = 128
SUBLANES = 8
HEAD_PAIR = 2 * HEAD_DIM
VMEM_LIMIT = 56 * 1024 * 1024

COL_GA = 0
COL_GB = COL_GA + D_MODEL
COL_QA = COL_GB + D_MODEL
COL_QB = COL_QA + A_WIDTH
COL_KA = COL_QB + B_WIDTH
COL_VA = COL_KA + A_WIDTH
COL_KB = COL_VA + A_WIDTH
COL_VB = COL_KB + B_KV_WIDTH
IN_COLS = COL_VB + B_KV_WIDTH
KV_COLS = IN_COLS - COL_KA
PROJ_TN = 768
ATT_BLOCK = A_PREV_CHUNKS * CHUNK

NT_DIMS = (((1,), (1,)), ((), ()))
TN_DIMS = (((0,), (0,)), ((), ()))


def _compiler_params(semantics):
    return pltpu.CompilerParams(dimension_semantics=semantics, vmem_limit_bytes=VMEM_LIMIT)


def _const_spec(shape):
    zeros = (0,) * len(shape)
    return pl.BlockSpec(shape, lambda *_: zeros, pipeline_mode=pl.Buffered(1))


def _in_proj_kernel(x_ref, w_ref, p_ref, kv_ref, xb_ref):
    j = pl.program_id(1)

    @pl.when(j == 0)
    def _():
        xb_ref[...] = x_ref[...].astype(BF16)

    acc = jnp.dot(xb_ref[...], w_ref[...], preferred_element_type=F32)
    p_ref[...] = acc.astype(BF16)

    @pl.when(j >= COL_KA // PROJ_TN)
    def _():
        kv_ref[...] = acc


def _in_proj(x, w_bf16, tm):
    n = x.shape[0]
    first_kv = COL_KA // PROJ_TN
    return pl.pallas_call(
        _in_proj_kernel,
        grid=(n // tm, IN_COLS // PROJ_TN),
        in_specs=[
            pl.BlockSpec((tm, D_MODEL), lambda i, j: (i, 0)),
            pl.BlockSpec((D_MODEL, PROJ_TN), lambda i, j: (0, j)),
        ],
        out_specs=[
            pl.BlockSpec((tm, PROJ_TN), lambda i, j: (i, j)),
            pl.BlockSpec((tm, PROJ_TN), lambda i, j: (i, jnp.maximum(j - first_kv, 0))),
        ],
        out_shape=[
            jax.ShapeDtypeStruct((n, IN_COLS), BF16),
            jax.ShapeDtypeStruct((n, KV_COLS), F32),
        ],
        scratch_shapes=[pltpu.VMEM((tm, D_MODEL), BF16)],
        compiler_params=_compiler_params(("parallel", "arbitrary")),
    )(x, w_bf16)


def _softmax_pv(s, v2, sink):
    m = jnp.max(s, axis=0, keepdims=True)
    if sink is not None:
        m = jnp.maximum(m, sink)
    p = jnp.exp(s - m)
    den = jnp.sum(p, axis=0, keepdims=True)
    if sink is not None:
        den = den + jnp.exp(sink - m)
    p = (p * (1.0 / den)).astype(BF16)
    return lax.dot_general(p, v2, TN_DIMS, preferred_element_type=F32)


def _scores(qq, k2, bias, key_thresh):
    s = lax.dot_general(k2, qq, NT_DIMS, preferred_element_type=F32) * (HEAD_DIM ** -0.5) + bias
    if key_thresh is not None:
        key = lax.broadcasted_iota(jnp.int32, s.shape, 0)
        s = jnp.where(key >= key_thresh, s, -jnp.inf)
    return s


def _mixer_a(q_of, k_of, v_of, bias_of, key_thresh, store):
    for j in range(A_HEADS // 2):
        q2 = q_of(j)
        tq = q2.shape[0]
        lane_hi = lax.broadcasted_iota(jnp.int32, q2.shape, 1) >= HEAD_DIM
        zero = jnp.zeros_like(q2)
        qq = jnp.concatenate([jnp.where(lane_hi, zero, q2), jnp.where(lane_hi, q2, zero)], axis=0)
        o = _softmax_pv(_scores(qq, k_of(j), bias_of(j), key_thresh), v_of(j), None)
        store(j, jnp.where(lane_hi, o[tq:2 * tq], o[0:tq]))


def _alibi_bias(tq, tk):
    shape = (tk, B_Q_HEADS * tq)
    key = lax.broadcasted_iota(jnp.int32, shape, 0)
    lane = lax.broadcasted_iota(jnp.int32, shape, 1)
    head = lax.shift_right_logical(lane, int(math.log2(tq)))
    q = lax.bitwise_and(lane, tq - 1)
    slope = jnp.exp((head + 1).astype(F32) * (-8.0 / B_Q_HEADS * math.log(2.0)))
    return -slope * jnp.abs(q + (tk - tq) - key).astype(F32)


def _sink_row(sink_ref, tq):
    lane = lax.broadcasted_iota(jnp.int32, (1, B_Q_HEADS * tq), 1)
    head = lax.shift_right_logical(lane, int(math.log2(tq)))
    row = jnp.zeros(lane.shape, F32)
    for h in range(B_Q_HEADS):
        row = jnp.where(head == h, sink_ref[h], row)
    return row


def _mixer_b(q_of, k2, v2, bias, sink_row, key_thresh, store):
    group = B_Q_HEADS // B_KV_HEADS
    pieces = []
    for j in range(B_Q_HEADS // 2):
        q2 = q_of(j)
        tq = q2.shape[0]
        lane_hi = lax.broadcasted_iota(jnp.int32, q2.shape, 1) >= HEAD_DIM
        zero = jnp.zeros_like(q2)
        q2_swapped = pltpu.roll(q2.astype(F32), HEAD_DIM, 1).astype(BF16)
        for hh in range(2):
            kv = (2 * j + hh) // group
            src = q2 if hh == kv else q2_swapped
            pieces.append(jnp.where(lane_hi, src, zero) if kv else jnp.where(lane_hi, zero, src))
    qq = jnp.concatenate(pieces, axis=0)
    o = _softmax_pv(_scores(qq, k2, bias, key_thresh), v2, sink_row)
    lane_hi = lax.broadcasted_iota(jnp.int32, (tq, HEAD_PAIR), 1) >= HEAD_DIM
    for j in range(B_Q_HEADS // 2):
        halves = []
        for hh in range(2):
            h = 2 * j + hh
            o_h = o[h * tq:(h + 1) * tq]
            halves.append(o_h if hh == h // group else pltpu.roll(o_h, HEAD_DIM, 1))
        store(j, jnp.where(lane_hi, halves[1], halves[0]))


def _attn_prompt_kernel(sink_ref, qa_ref, kap_ref, kac_ref, vap_ref, vac_ref,
                        qb_ref, kbp_ref, kbc_ref, vbp_ref, vbc_ref, bias_ref,
                        ya_ref, yb_ref, ka_s, va_s, kb_s, vb_s, bias_b_s, sink_s):
    i = pl.program_id(1)
    blk = ATT_BLOCK
    ka_s[0:blk, :] = kap_ref[...]
    ka_s[blk:2 * blk, :] = kac_ref[...]
    va_s[0:blk, :] = vap_ref[...]
    va_s[blk:2 * blk, :] = vac_ref[...]
    kb_s[0:blk, :] = kbp_ref[...]
    kb_s[blk:2 * blk, :] = kbc_ref[...]
    vb_s[0:blk, :] = vbp_ref[...]
    vb_s[blk:2 * blk, :] = vbc_ref[...]

    tk_a = (A_PREV_CHUNKS + 1) * CHUNK
    tk_b = (B_PREV_CHUNKS + 1) * CHUNK
    b_off = blk - B_PREV_CHUNKS * CHUNK
    bias_b_s[...] = _alibi_bias(CHUNK, tk_b)
    sink_s[...] = _sink_row(sink_ref, CHUNK)

    def chunk_body(c, carry):
        row0 = pl.multiple_of(c * CHUNK, CHUNK)
        thresh_a = jnp.where(i > 0, 0, blk - c * CHUNK)
        thresh_b = jnp.where(i > 0, 0, B_PREV_CHUNKS * CHUNK - c * CHUNK)

        def store_a(j, val):
            ya_ref[pl.ds(row0, CHUNK), j * HEAD_PAIR:(j + 1) * HEAD_PAIR] = val.astype(ya_ref.dtype)

        def store_b(j, val):
            yb_ref[pl.ds(row0, CHUNK), j * HEAD_PAIR:(j + 1) * HEAD_PAIR] = val.astype(yb_ref.dtype)

        _mixer_a(
            lambda j: qa_ref[pl.ds(row0, CHUNK), j * HEAD_PAIR:(j + 1) * HEAD_PAIR],
            lambda j: ka_s[pl.ds(row0, tk_a), j * HEAD_PAIR:(j + 1) * HEAD_PAIR],
            lambda j: va_s[pl.ds(row0, tk_a), j * HEAD_PAIR:(j + 1) * HEAD_PAIR],
            lambda j: bias_ref[j], thresh_a, store_a)

        rowb = pl.multiple_of(b_off + c * CHUNK, CHUNK)
        _mixer_b(
            lambda j: qb_ref[pl.ds(row0, CHUNK), j * HEAD_PAIR:(j + 1) * HEAD_PAIR],
            kb_s[pl.ds(rowb, tk_b), :], vb_s[pl.ds(rowb, tk_b), :],
            bias_b_s[...], sink_s[...], thresh_b, store_b)
        return carry

    lax.fori_loop(0, blk // CHUNK, chunk_body, 0)


def _attn_prompt(p, bias_a, sinks, batch, seq):
    blk = ATT_BLOCK
    nb = seq // blk
    n = batch * seq

    def cur(col_block):
        return lambda b, i: (b * nb + i, col_block)

    def prev(col_block):
        return lambda b, i: (b * nb + jnp.maximum(i - 1, 0), col_block)

    wa, wb = A_WIDTH, B_KV_WIDTH
    in_specs = [
        pl.BlockSpec(memory_space=pltpu.SMEM),
        pl.BlockSpec((blk, wa), cur(COL_QA // wa)),
        pl.BlockSpec((blk, wa), prev(COL_KA // wa)),
        pl.BlockSpec((blk, wa), cur(COL_KA // wa)),
        pl.BlockSpec((blk, wa), prev(COL_VA // wa)),
        pl.BlockSpec((blk, wa), cur(COL_VA // wa)),
        pl.BlockSpec((blk, wa), cur(COL_QB // wa)),
        pl.BlockSpec((blk, wb), prev(COL_KB // wb)),
        pl.BlockSpec((blk, wb), cur(COL_KB // wb)),
        pl.BlockSpec((blk, wb), prev(COL_VB // wb)),
        pl.BlockSpec((blk, wb), cur(COL_VB // wb)),
        _const_spec(bias_a.shape),
    ]
    out_spec = pl.BlockSpec((blk, wa), lambda b, i: (b * nb + i, 0))
    rows_b = B_Q_HEADS * CHUNK
    return pl.pallas_call(
        _attn_prompt_kernel,
        grid=(batch, nb),
        in_specs=in_specs,
        out_specs=[out_spec, out_spec],
        out_shape=[jax.ShapeDtypeStruct((n, A_WIDTH), BF16), jax.ShapeDtypeStruct((n, B_WIDTH), BF16)],
        scratch_shapes=[
            pltpu.VMEM((2 * blk, wa), BF16), pltpu.VMEM((2 * blk, wa), BF16),
            pltpu.VMEM((2 * blk, wb), BF16), pltpu.VMEM((2 * blk, wb), BF16),
            pltpu.VMEM(((B_PREV_CHUNKS + 1) * CHUNK, rows_b), F32), pltpu.VMEM((1, rows_b), F32),
        ],
        compiler_params=_compiler_params(("parallel", "arbitrary")),
    )(sinks, p, p, p, p, p, p, p, p, p, p, bias_a)


def _attn_sample_kernel(sink_ref, qa_ref, ka_ref, va_ref, qb_ref, kb_ref, vb_ref, bias_ref, ya_ref, yb_ref):
    t, tk_b = qb_ref.shape[0], kb_ref.shape[1]

    def store_a(j, val):
        ya_ref[:, j * HEAD_PAIR:(j + 1) * HEAD_PAIR] = val.astype(ya_ref.dtype)

    def store_b(j, val):
        yb_ref[:, j * HEAD_PAIR:(j + 1) * HEAD_PAIR] = val.astype(yb_ref.dtype)

    _mixer_a(
        lambda j: qa_ref[:, j * HEAD_PAIR:(j + 1) * HEAD_PAIR],
        lambda j: ka_ref[0, :, j * HEAD_PAIR:(j + 1) * HEAD_PAIR],
        lambda j: va_ref[0, :, j * HEAD_PAIR:(j + 1) * HEAD_PAIR],
        lambda j: bias_ref[j], None, store_a)

    _mixer_b(
        lambda j: qb_ref[:, j * HEAD_PAIR:(j + 1) * HEAD_PAIR],
        kb_ref[0], vb_ref[0], _alibi_bias(t, tk_b), _sink_row(sink_ref, t), None, store_b)


def _attn_sample(p, kcat_a, vcat_a, kcat_b, vcat_b, bias_a, sinks, batch, t):
    n = batch * t
    wa, wb = A_WIDTH, B_KV_WIDTH
    tk_a, tk_b = kcat_a.shape[1], kcat_b.shape[1]
    in_specs = [
        pl.BlockSpec(memory_space=pltpu.SMEM),
        pl.BlockSpec((t, wa), lambda b: (b, COL_QA // wa)),
        pl.BlockSpec((1, tk_a, wa), lambda b: (b, 0, 0)),
        pl.BlockSpec((1, tk_a, wa), lambda b: (b, 0, 0)),
        pl.BlockSpec((t, wa), lambda b: (b, COL_QB // wa)),
        pl.BlockSpec((1, tk_b, wb), lambda b: (b, 0, 0)),
        pl.BlockSpec((1, tk_b, wb), lambda b: (b, 0, 0)),
        _const_spec(bias_a.shape),
    ]
    out_spec = pl.BlockSpec((t, wa), lambda b: (b, 0))
    return pl.pallas_call(
        _attn_sample_kernel,
        grid=(batch,),
        in_specs=in_specs,
        out_specs=[out_spec, out_spec],
        out_shape=[jax.ShapeDtypeStruct((n, A_WIDTH), BF16), jax.ShapeDtypeStruct((n, B_WIDTH), BF16)],
        compiler_params=_compiler_params(("parallel",)),
    )(sinks, p, kcat_a, vcat_a, p, kcat_b, vcat_b, bias_a)


def _layer_norm(x, g, b):
    mu = jnp.mean(x, axis=-1, keepdims=True)
    xc = x - mu
    var = jnp.mean(xc * xc, axis=-1, keepdims=True)
    return xc * lax.rsqrt(var + LN_EPS) * g + b


def _merge_route_kernel(ya_ref, yb_ref, ga_ref, gb_ref, x_ref, wpa_ref, wpb_ref, wout_ref, g_ref, b_ref,
                        wq_ref, sk1_ref, sk2_ref, h_ref, idx_ref, gate_ref):
    za = jnp.dot(ya_ref[...], wpa_ref[...], preferred_element_type=F32)
    zb = jnp.dot(yb_ref[...], wpb_ref[...], preferred_element_type=F32)
    m = jax.nn.sigmoid(ga_ref[...].astype(F32)) * za + jax.nn.sigmoid(gb_ref[...].astype(F32)) * zb
    r = jnp.dot(m.astype(BF16), wout_ref[...], preferred_element_type=F32)
    h_ref[...] = _layer_norm(DEEPNORM_ALPHA * x_ref[...] + r, g_ref[...], b_ref[...])
    for t0 in range(0, h_ref.shape[0], ROUTE_TM):
        q = _route_query(h_ref[t0:t0 + ROUTE_TM, :].astype(BF16), wq_ref)
        routed = [_route_head(q, hd, sk1_ref, sk2_ref) for hd in range(PEER_HEADS)]
        experts = jnp.concatenate([e for e, _ in routed], axis=0)
        gates = jnp.concatenate([g for _, g in routed], axis=0)
        idx_ref[t0:t0 + ROUTE_TM, :] = experts.astype(F32).T.astype(jnp.int32)
        gate_ref[t0:t0 + ROUTE_TM, :] = gates.T


def _merge_route(ya, yb, p, x, wpa, wpb, wout, ln_g, ln_b, wq, sk1, sk2, tm):
    n = x.shape[0]
    assert n % tm == 0 and tm % ROUTE_TM == 0
    row = lambda i: (i, 0)
    return pl.pallas_call(
        _merge_route_kernel,
        grid=(n // tm,),
        in_specs=[
            pl.BlockSpec((tm, A_WIDTH), row),
            pl.BlockSpec((tm, B_WIDTH), row),
            pl.BlockSpec((tm, D_MODEL), lambda i: (i, COL_GA // D_MODEL)),
            pl.BlockSpec((tm, D_MODEL), lambda i: (i, COL_GB // D_MODEL)),
            pl.BlockSpec((tm, D_MODEL), row),
            _const_spec(wpa.shape), _const_spec(wpb.shape), _const_spec(wout.shape),
            _const_spec(ln_g.shape), _const_spec(ln_b.shape),
            _const_spec(wq.shape), _const_spec(sk1.shape), _const_spec(sk2.shape),
        ],
        out_specs=[pl.BlockSpec((tm, D_MODEL), row), pl.BlockSpec((tm, PEER_PICKS), row),
                   pl.BlockSpec((tm, PEER_PICKS), row)],
        out_shape=[jax.ShapeDtypeStruct((n, D_MODEL), F32), jax.ShapeDtypeStruct((n, PEER_PICKS), jnp.int32),
                   jax.ShapeDtypeStruct((n, PEER_PICKS), F32)],
        compiler_params=_compiler_params(("parallel",)),
    )(ya, yb, p, p, x, wpa, wpb, wout, ln_g, ln_b, wq, sk1, sk2)


ROUTE_TM = 128
ID_SENTINEL = 1 << 30


def _topk_rows(s, ids, k):
    vals, picked = [], []
    for _ in range(k):
        m = jnp.max(s, axis=0, keepdims=True)
        ix = jnp.min(jnp.where(s == m, ids, ID_SENTINEL), axis=0, keepdims=True)
        s = jnp.where(ids == ix, -jnp.inf, s)
        vals.append(m)
        picked.append(ix)
    return jnp.concatenate(vals, axis=0), jnp.concatenate(picked, axis=0)


def _sorting_network(n):
    def merge(lo, hi, r):
        step = r * 2
        if step < hi - lo:
            yield from merge(lo, hi, step)
            yield from merge(lo + r, hi, step)
            yield from [(i, i + r) for i in range(lo + r, hi - r, step)]
        else:
            yield (lo, lo + r)

    def sort(lo, hi):
        if hi - lo >= 1:
            mid = lo + (hi - lo) // 2
            yield from sort(lo, mid)
            yield from sort(mid + 1, hi)
            yield from merge(lo, hi, 1)

    return list(sort(0, n - 1))


def _topk_keys(s, k):
    wires = s.shape[0] // SUBLANES
    assert wires == k
    sub = lax.broadcasted_iota(jnp.int32, (SUBLANES, s.shape[1]), 0)
    v = [s[g * SUBLANES:(g + 1) * SUBLANES] for g in range(wires)]
    ids = [sub + g * SUBLANES for g in range(wires)]
    for a, b in _sorting_network(wires):
        first = (v[a] > v[b]) | ((v[a] == v[b]) & (ids[a] < ids[b]))
        v[a], v[b] = jnp.where(first, v[a], v[b]), jnp.where(first, v[b], v[a])
        ids[a], ids[b] = jnp.where(first, ids[a], ids[b]), jnp.where(first, ids[b], ids[a])
    vals, picked = [], []
    for t in range(k):
        m = jnp.max(v[0], axis=0, keepdims=True)
        ix = jnp.min(jnp.where(v[0] == m, ids[0], ID_SENTINEL), axis=0, keepdims=True)
        vals.append(m)
        picked.append(ix)
        hit = ids[0] == ix
        for w in range(k - 1 - t):
            v[w] = jnp.where(hit, v[w + 1], v[w])
            ids[w] = jnp.where(hit, ids[w + 1], ids[w])
    return jnp.concatenate(vals, axis=0), jnp.concatenate(picked, axis=0)


def _candidate_blocks(v1, v2):
    t = v1.shape[1]
    sub = lax.broadcasted_iota(jnp.int32, (SUBLANES, t), 0)
    neg = jnp.full((SUBLANES, t), -jnp.inf, F32)
    k = PEER_TOPK
    blocks = []
    blocks.append((v1[0:1] + v2[0:8], sub))
    blocks.append((v1[0:1] + v2[8:16], sub + 8))
    blocks.append((v1[1:2] + v2[0:8], sub + k))
    blocks.append((jnp.where(sub < 5, v1[2:3] + v2[0:8], neg), sub + 2 * k))
    blocks.append((jnp.where(sub < 4, v1[3:4] + v2[0:8], neg), sub + 3 * k))
    blocks.append((v1[8:16] + v2[0:1], (sub + 8) * k))
    a47 = jnp.concatenate([v1[4:8], v1[4:8]], axis=0)
    blocks.append((a47 + jnp.where(sub < 4, v2[0:1], v2[1:2]),
                   (4 + lax.bitwise_and(sub, 3)) * k + lax.shift_right_logical(sub, 2)))
    blocks.append((jnp.where(sub == 0, v1[4:5] + v2[2:3], neg), sub + 4 * k + 2))
    vals = jnp.concatenate([b[0] for b in blocks], axis=0)
    ids = jnp.concatenate([b[1] for b in blocks], axis=0)
    return vals, ids


def _route_query(h, wq_ref):
    return jnp.dot(h, wq_ref[...], preferred_element_type=F32).astype(BF16)


def _route_head(q, hd, sk1_ref, sk2_ref):
    half = PEER_QDIM // 2
    q1 = q[:, hd * PEER_QDIM:hd * PEER_QDIM + half]
    q2 = q[:, hd * PEER_QDIM + half:(hd + 1) * PEER_QDIM]
    s1 = lax.dot_general(sk1_ref[...], q1, NT_DIMS, preferred_element_type=F32)
    s2 = lax.dot_general(sk2_ref[...], q2, NT_DIMS, preferred_element_type=F32)
    v1, i1 = _topk_keys(s1, PEER_TOPK)
    v2, i2 = _topk_keys(s2, PEER_TOPK)
    sc, ci = _topk_rows(*_candidate_blocks(v1, v2), PEER_TOPK)
    ca = lax.shift_right_logical(ci, int(math.log2(PEER_TOPK)))
    cb = lax.bitwise_and(ci, PEER_TOPK - 1)
    e1 = jnp.zeros_like(ci)
    e2 = jnp.zeros_like(ci)
    for a in range(PEER_TOPK):
        e1 = jnp.where(ca == a, i1[a:a + 1], e1)
        e2 = jnp.where(cb == a, i2[a:a + 1], e2)
    e = jnp.exp(sc - sc[0:1])
    return e1 * PEER_KEYS + e2, e / jnp.sum(e, axis=0, keepdims=True)


PEER_TB = 64
PEER_ROUND = 8
PEER_SLOTS = 2 * PEER_ROUND
PEER_LOOKAHEAD = PEER_ROUND
ROW_TILES = D_MODEL // LANES
HALF_TILES = ROW_TILES // 2

def _gelu_exact(x):
    return 0.5 * x * (1.0 + lax.erf(x * (2.0 ** -0.5)))


def _peer_kernel(idx_ref, gate_ref, h_ref, uv_ref, g_ref, b_ref, y_ref, *scratch):
    bufs, (sem, ffn, pre, wgt) = scratch[:PEER_SLOTS], scratch[PEER_SLOTS:]
    i = pl.program_id(0)
    last = pl.num_programs(0) - 1
    tb = h_ref.shape[0]

    def issue(t, slot, part=None):
        per_part = PEER_PICKS // ROW_TILES
        picks = range(PEER_PICKS) if part is None else range(part * per_part, (part + 1) * per_part)
        for k in picks:
            dst = bufs[slot].at[:, k // SUBLANES, pl.ds(k % SUBLANES, 1), :]
            pltpu.make_async_copy(uv_ref.at[idx_ref[0, t, k]], dst, sem.at[slot]).start(priority=k % 2)

    def wait_slot(slot):
        pltpu.make_async_copy(bufs[slot], bufs[slot], sem.at[slot]).wait()

    def picks_tile(slot, c):
        return bufs[slot][c].reshape(PEER_PICKS, LANES)

    @pl.when(i == 0)
    def _():
        for t in range(PEER_LOOKAHEAD):
            issue(t, t)

    eye = (lax.broadcasted_iota(jnp.int32, (PEER_PICKS, PEER_PICKS), 0)
           == lax.broadcasted_iota(jnp.int32, (PEER_PICKS, PEER_PICKS), 1))

    def halves(words):
        lo = lax.bitcast_convert_type(lax.shift_left(words, jnp.int32(16)), F32)
        hi = lax.bitcast_convert_type(lax.bitwise_and(words, jnp.int32(-(1 << 16))), F32)
        return lo, hi

    def pre_activation(t, slot, between):
        h_row = h_ref[pl.ds(t, 1), :]
        acc = None
        for c in range(HALF_TILES):
            between(c)
            lo, hi = halves(picks_tile(slot, c))
            term = (lo * h_row[:, c * LANES:(c + 1) * LANES]
                    + hi * h_row[:, (HALF_TILES + c) * LANES:(HALF_TILES + c + 1) * LANES])
            acc = term if acc is None else acc + term
        a_col = jnp.sum(acc, axis=1, keepdims=True)
        return jnp.sum(jnp.where(eye, a_col, 0.0), axis=0, keepdims=True)

    def weighted_sum(t, slot, w_row, between):
        w_col = jnp.sum(jnp.where(eye, w_row, 0.0), axis=1, keepdims=True)
        out_lo, out_hi = [], []
        for c in range(HALF_TILES):
            between(HALF_TILES + c)
            lo, hi = halves(picks_tile(slot, HALF_TILES + c))
            out_lo.append(jnp.sum(w_col * lo, axis=0, keepdims=True))
            out_hi.append(jnp.sum(w_col * hi, axis=0, keepdims=True))
        ffn[pl.ds(t, 1), :] = jnp.concatenate(out_lo + out_hi, axis=1)

    def round_pair(rr, carry):
        for half in range(2):
            t0 = pl.multiple_of((2 * rr + half) * PEER_ROUND, PEER_ROUND)
            here, there = half * PEER_ROUND, (1 - half) * PEER_ROUND
            def issue_ahead(j):
                return lambda part: issue(t0 + PEER_ROUND + j, there + j, part)

            for j in range(PEER_ROUND):
                wait_slot(here + j)
                pre[j:j + 1, :] = pre_activation(t0 + j, here + j, issue_ahead(j))
            wgt[...] = gate_ref[pl.ds(t0, PEER_ROUND), :] * _gelu_exact(pre[...])
            for j in range(PEER_ROUND):
                weighted_sum(t0 + j, here + j, wgt[j:j + 1, :], issue_ahead(j))
        return carry

    lax.fori_loop(0, tb // PEER_SLOTS, round_pair, 0)

    @pl.when(i == last)
    def _():
        for t in range(PEER_LOOKAHEAD):
            wait_slot(t)

    y_ref[...] = _layer_norm(DEEPNORM_ALPHA * h_ref[...] + ffn[...], g_ref[...], b_ref[...])


def _peer(h, idx, gate, uv, ln_g, ln_b):
    n = h.shape[0]
    tb = min(PEER_TB, n)
    assert n % tb == 0 and tb % PEER_SLOTS == 0
    nb = n // tb
    row = lambda i: (i, 0)
    idx_pad = jnp.concatenate([idx, jnp.zeros((tb, PEER_PICKS), idx.dtype)], axis=0)
    idx_next = idx_pad[tb:].reshape(nb, tb, PEER_PICKS)[:, :PEER_LOOKAHEAD]
    idx_ext = jnp.concatenate([idx.reshape(nb, tb, PEER_PICKS), idx_next], axis=1)
    return pl.pallas_call(
        _peer_kernel,
        grid=(nb,),
        in_specs=[
            pl.BlockSpec((1, tb + PEER_LOOKAHEAD, PEER_PICKS), lambda i: (i, 0, 0), memory_space=pltpu.SMEM),
            pl.BlockSpec((tb, PEER_PICKS), row),
            pl.BlockSpec((tb, D_MODEL), row),
            pl.BlockSpec(memory_space=pl.ANY),
            _const_spec(ln_g.shape), _const_spec(ln_b.shape),
        ],
        out_specs=pl.BlockSpec((tb, D_MODEL), row),
        out_shape=jax.ShapeDtypeStruct((n, D_MODEL), F32),
        scratch_shapes=(
            [pltpu.VMEM((ROW_TILES, PEER_PICKS // SUBLANES, SUBLANES, LANES), jnp.int32) for _ in range(PEER_SLOTS)]
            + [pltpu.SemaphoreType.DMA((PEER_SLOTS,)), pltpu.VMEM((tb, D_MODEL), F32),
               pltpu.VMEM((PEER_ROUND, PEER_PICKS), F32), pltpu.VMEM((PEER_ROUND, PEER_PICKS), F32)]
        ),
        compiler_params=_compiler_params(("arbitrary",)),
    )(idx_ext, gate, h, uv, ln_g, ln_b)


def _rel_bias_pairs(rel_table, t, tk):
    dist = (tk - 1) - jnp.arange(tk + t - 1)
    line = jnp.take(rel_table, jnp.clip(dist, -A_REL_CLIP, A_REL_CLIP) + A_REL_CLIP, axis=1).astype(F32)
    bias = jnp.stack([line[:, t - 1 - q:t - 1 - q + tk] for q in range(t)], axis=1)
    return jnp.swapaxes(bias.reshape(A_HEADS // 2, 2 * t, tk), 1, 2)


PACK_ROWS = 256


def _pack_kernel(u_ref, v_ref, out_ref, stage, sem):
    i = pl.program_id(0)
    buf = lax.rem(i, 2)
    rows = u_ref.shape[0]
    half = D_MODEL // 2

    def drain(b):
        pltpu.make_async_copy(stage.at[b], stage.at[b], sem.at[b]).wait()

    @pl.when(i >= 2)
    def _():
        drain(buf)

    def words(x_ref):
        def bf16_bits(x):
            return lax.bitcast_convert_type(x.astype(BF16).astype(F32), jnp.int32)
        lo = lax.shift_right_logical(bf16_bits(x_ref[:, 0:half]), jnp.int32(16))
        hi = lax.bitwise_and(bf16_bits(x_ref[:, half:D_MODEL]), jnp.int32(-(1 << 16)))
        return lax.bitwise_or(hi, lo)

    for part, x_ref in enumerate((u_ref, v_ref)):
        w = words(x_ref)
        for c in range(HALF_TILES):
            tile = w[:, c * LANES:(c + 1) * LANES].reshape(rows // SUBLANES, SUBLANES, LANES)
            stage[buf, part * HALF_TILES + c] = tile

    def row_copy(r, carry):
        for s in range(SUBLANES):
            src = stage.at[buf, :, r, pl.ds(s, 1), :]
            pltpu.make_async_copy(src, out_ref.at[i * rows + r * SUBLANES + s], sem.at[buf]).start()
        return carry

    lax.fori_loop(0, rows // SUBLANES, row_copy, 0)

    @pl.when(i == pl.num_programs(0) - 1)
    def _():
        drain(buf)

        @pl.when(i >= 1)
        def _():
            drain(1 - buf)


def _pack_expert_tables(u, v):
    n = u.shape[0]
    rows = PACK_ROWS
    assert n % rows == 0
    return pl.pallas_call(
        _pack_kernel,
        grid=(n // rows,),
        in_specs=[pl.BlockSpec((rows, D_MODEL), lambda i: (i, 0)), pl.BlockSpec((rows, D_MODEL), lambda i: (i, 0))],
        out_specs=pl.BlockSpec(memory_space=pl.ANY),
        out_shape=jax.ShapeDtypeStruct((n, ROW_TILES, 1, LANES), jnp.int32),
        scratch_shapes=[pltpu.VMEM((2, ROW_TILES, rows // SUBLANES, SUBLANES, LANES), jnp.int32),
                        pltpu.SemaphoreType.DMA((2,))],
        compiler_params=_compiler_params(("arbitrary",)),
    )(u, v)


def _permute_in_proj(w_in):
    offs = np.cumsum([0, A_WIDTH, A_WIDTH, A_WIDTH, B_WIDTH, B_KV_WIDTH, B_KV_WIDTH, D_MODEL, D_MODEL])
    qa, ka, va, qb, kb, vb, ga, gb = [w_in[:, offs[s]:offs[s + 1]] for s in range(8)]
    return jnp.concatenate([ga, gb, qa, qb, ka, va, kb, vb], axis=1).astype(BF16)


def _split_kv(kv, batch, t):
    ka = kv[:, 0:A_WIDTH].reshape(batch, t, A_HEADS, HEAD_DIM)
    va = kv[:, A_WIDTH:2 * A_WIDTH].reshape(batch, t, A_HEADS, HEAD_DIM)
    kb = kv[:, 2 * A_WIDTH:2 * A_WIDTH + B_KV_WIDTH].reshape(batch, t, B_KV_HEADS, HEAD_DIM)
    vb = kv[:, 2 * A_WIDTH + B_KV_WIDTH:].reshape(batch, t, B_KV_HEADS, HEAD_DIM)
    return ka, va, kb, vb


def _cache_heads_kernel(k_ref, v_ref, ko_ref, vo_ref):
    for src, dst in ((k_ref, ko_ref), (v_ref, vo_ref)):
        for hd in range(A_HEADS):
            dst[0, :, hd, :] = src[:, hd * HEAD_DIM:(hd + 1) * HEAD_DIM]


def _cache_rows_a(kv, batch, seq, rows):
    assert seq % rows == 0
    per_seq = seq // rows
    out = jax.ShapeDtypeStruct((batch, rows, A_HEADS, HEAD_DIM), F32)
    out_spec = pl.BlockSpec((1, rows, A_HEADS, HEAD_DIM), lambda b: (b, 0, 0, 0))
    return pl.pallas_call(
        _cache_heads_kernel,
        grid=(batch,),
        in_specs=[pl.BlockSpec((rows, A_WIDTH), lambda b: (b * per_seq + per_seq - 1, 0)),
                  pl.BlockSpec((rows, A_WIDTH), lambda b: (b * per_seq + per_seq - 1, 1))],
        out_specs=[out_spec, out_spec],
        out_shape=[out, out],
        compiler_params=_compiler_params(("parallel",)),
    )(kv, kv)


def kernel(x_prompt, x_sample, cache_a_k, cache_a_v, cache_b_k, cache_b_v, w_in, rel_bias_table, b_sinks,
           w_proj_a, w_proj_b, w_out, ln1_g, ln1_b, peer_w_query, peer_sub_keys_1, peer_sub_keys_2,
           peer_u, peer_v, ln2_g, ln2_b):
    batch, seq, _ = x_prompt.shape
    dbatch, dseq, _ = x_sample.shape
    assert seq % ATT_BLOCK == 0 and seq >= ATT_BLOCK

    w_in_p = _permute_in_proj(w_in)
    wpa, wpb, wout = w_proj_a.astype(BF16), w_proj_b.astype(BF16), w_out.astype(BF16)
    wq = peer_w_query.astype(BF16)
    sk1, sk2 = peer_sub_keys_1.astype(BF16), peer_sub_keys_2.astype(BF16)
    uv = _pack_expert_tables(peer_u, peer_v)
    ln1 = (ln1_g.reshape(1, D_MODEL), ln1_b.reshape(1, D_MODEL))
    ln2 = (ln2_g.reshape(1, D_MODEL), ln2_b.reshape(1, D_MODEL))
    sinks = b_sinks.astype(F32)

    def tail(xf, ya, yb, p, tm):
        h, idx, gate = _merge_route(ya, yb, p, xf, wpa, wpb, wout, *ln1, wq, sk1, sk2, tm=tm)
        return _peer(h, idx, gate, uv, *ln2)

    ns = dbatch * dseq
    xs = x_sample.reshape(ns, D_MODEL)
    p_s, kv_s = _in_proj(xs, w_in_p, tm=ns)
    ka_s, va_s, kb_s, vb_s = _split_kv(kv_s, dbatch, dseq)

    def with_cache(cache, new_cols, width):
        new = p_s[:, new_cols:new_cols + width].reshape(dbatch, dseq, width)
        return jnp.concatenate([cache.reshape(dbatch, -1, width).astype(BF16), new], axis=1)

    kcat_a = with_cache(cache_a_k, COL_KA, A_WIDTH)
    vcat_a = with_cache(cache_a_v, COL_VA, A_WIDTH)
    kcat_b = with_cache(cache_b_k, COL_KB, B_KV_WIDTH)
    vcat_b = with_cache(cache_b_v, COL_VB, B_KV_WIDTH)
    bias_s = _rel_bias_pairs(rel_bias_table, dseq, kcat_a.shape[1])
    ya_s, yb_s = _attn_sample(p_s, kcat_a, vcat_a, kcat_b, vcat_b, bias_s, sinks, dbatch, dseq)
    y_sample = tail(xs, ya_s, yb_s, p_s, ns).reshape(dbatch, dseq, D_MODEL)

    xp = x_prompt.reshape(batch * seq, D_MODEL)
    p, kv = _in_proj(xp, w_in_p, tm=1024)
    bias_p = _rel_bias_pairs(rel_bias_table, CHUNK, (A_PREV_CHUNKS + 1) * CHUNK)
    ya, yb = _attn_prompt(p, bias_p, sinks, batch, seq)
    y_prompt = tail(xp, ya, yb, p, 256).reshape(batch, seq, D_MODEL)
    la = min(A_PREV_CHUNKS * CHUNK, seq)
    lb = min(B_PREV_CHUNKS * CHUNK, seq)
    ka_last, va_last = _cache_rows_a(kv, batch, seq, la)
    _, _, kb, vb = _split_kv(kv, batch, seq)

    return (y_prompt, y_sample, ka_last, va_last, kb[:, -lb:], vb[:, -lb:], ka_s, va_s, kb_s, vb_s)
```

```python
import math

import jax
import jax.numpy as jnp
import numpy as np
from jax import lax
from jax.experimental import pallas as pl
from jax.experimental.pallas import tpu as pltpu

F32 = jnp.float32
BF16 = jnp.bfloat16

D_MODEL = 2048
CHUNK = 64
HEAD_DIM = 64
A_HEADS = 16
A_PREV_CHUNKS = 8
A_REL_CLIP = 128
B_Q_HEADS = 16
B_KV_HEADS = 2
B_PREV_CHUNKS = 2
A_WIDTH = A_HEADS * HEAD_DIM
B_WIDTH = B_Q_HEADS * HEAD_DIM
B_KV_WIDTH = B_KV_HEADS * HEAD_DIM
PEER_HEADS = 8
PEER_KEYS = 128
PEER_QDIM = 256
PEER_TOPK = 16
PEER_PICKS = PEER_HEADS * PEER_TOPK
LN_EPS = 1e-5
DEEPNORM_ALPHA = 2.0 ** 0.25

LANES = 128
SUBLANES = 8
HEAD_PAIR = 2 * HEAD_DIM
VMEM_LIMIT = 56 * 1024 * 1024

COL_GA = 0
COL_GB = COL_GA + D_MODEL
COL_QA = COL_GB + D_MODEL
COL_QB = COL_QA + A_WIDTH
COL_KA = COL_QB + B_WIDTH
COL_VA = COL_KA + A_WIDTH
COL_KB = COL_VA + A_WIDTH
COL_VB = COL_KB + B_KV_WIDTH
IN_COLS = COL_VB + B_KV_WIDTH
KV_COLS = IN_COLS - COL_KA
PROJ_TN = 768
ATT_BLOCK = A_PREV_CHUNKS * CHUNK

NT_DIMS = (((1,), (1,)), ((), ()))
TN_DIMS = (((0,), (0,)), ((), ()))


def _compiler_params(semantics):
    return pltpu.CompilerParams(dimension_semantics=semantics, vmem_limit_bytes=VMEM_LIMIT)


def _const_spec(shape):
    zeros = (0,) * len(shape)
    return pl.BlockSpec(shape, lambda *_: zeros, pipeline_mode=pl.Buffered(1))


def _in_proj_kernel(x_ref, w_ref, p_ref, kv_ref, xb_ref):
    j = pl.program_id(1)

    @pl.when(j == 0)
    def _():
        xb_ref[...] = x_ref[...].astype(BF16)

    acc = jnp.dot(xb_ref[...], w_ref[...], preferred_element_type=F32)
    p_ref[...] = acc.astype(BF16)

    @pl.when(j >= COL_KA // PROJ_TN)
    def _():
        kv_ref[...] = acc


def _in_proj(x, w_bf16, tm):
    n = x.shape[0]
    first_kv = COL_KA // PROJ_TN
    return pl.pallas_call(
        _in_proj_kernel,
        grid=(n // tm, IN_COLS // PROJ_TN),
        in_specs=[
            pl.BlockSpec((tm, D_MODEL), lambda i, j: (i, 0)),
            pl.BlockSpec((D_MODEL, PROJ_TN), lambda i, j: (0, j)),
        ],
        out_specs=[
            pl.BlockSpec((tm, PROJ_TN), lambda i, j: (i, j)),
            pl.BlockSpec((tm, PROJ_TN), lambda i, j: (i, jnp.maximum(j - first_kv, 0))),
        ],
        out_shape=[
            jax.ShapeDtypeStruct((n, IN_COLS), BF16),
            jax.ShapeDtypeStruct((n, KV_COLS), F32),
        ],
        scratch_shapes=[pltpu.VMEM((tm, D_MODEL), BF16)],
        compiler_params=_compiler_params(("parallel", "arbitrary")),
    )(x, w_bf16)


def _softmax_pv(s, v2, sink):
    m = jnp.max(s, axis=0, keepdims=True)
    if sink is not None:
        m = jnp.maximum(m, sink)
    p = jnp.exp(s - m)
    den = jnp.sum(p, axis=0, keepdims=True)
    if sink is not None:
        den = den + jnp.exp(sink - m)
    p = (p * (1.0 / den)).astype(BF16)
    return lax.dot_general(p, v2, TN_DIMS, preferred_element_type=F32)


def _scores(qq, k2, bias, key_thresh):
    s = lax.dot_general(k2, qq, NT_DIMS, preferred_element_type=F32) * (HEAD_DIM ** -0.5) + bias
    if key_thresh is not None:
        key = lax.broadcasted_iota(jnp.int32, s.shape, 0)
        s = jnp.where(key >= key_thresh, s, -jnp.inf)
    return s


def _mixer_a(q_of, k_of, v_of, bias_of, key_thresh, store):
    def pair_scores(j):
        q2 = q_of(j)
        lane_hi = lax.broadcasted_iota(jnp.int32, q2.shape, 1) >= HEAD_DIM
        zero = jnp.zeros_like(q2)
        qq = jnp.concatenate([jnp.where(lane_hi, zero, q2), jnp.where(lane_hi, q2, zero)], axis=0)
        return _scores(qq, k_of(j), bias_of(j), key_thresh)

    pairs = A_HEADS // 2
    s = pair_scores(0)
    for j in range(pairs):
        s_next = pair_scores(j + 1) if j + 1 < pairs else None
        o = _softmax_pv(s, v_of(j), None)
        tq = o.shape[0] // 2
        lane_hi = lax.broadcasted_iota(jnp.int32, (tq, HEAD_PAIR), 1) >= HEAD_DIM
        store(j, jnp.where(lane_hi, o[tq:2 * tq], o[0:tq]))
        s = s_next


def _alibi_bias(tq, tk):
    shape = (tk, B_Q_HEADS * tq)
    key = lax.broadcasted_iota(jnp.int32, shape, 0)
    lane = lax.broadcasted_iota(jnp.int32, shape, 1)
    head = lax.shift_right_logical(lane, int(math.log2(tq)))
    q = lax.bitwise_and(lane, tq - 1)
    slope = jnp.exp((head + 1).astype(F32) * (-8.0 / B_Q_HEADS * math.log(2.0)))
    return -slope * jnp.abs(q + (tk - tq) - key).astype(F32)


def _sink_row(sink_ref, tq):
    lane = lax.broadcasted_iota(jnp.int32, (1, B_Q_HEADS * tq), 1)
    head = lax.shift_right_logical(lane, int(math.log2(tq)))
    row = jnp.zeros(lane.shape, F32)
    for h in range(B_Q_HEADS):
        row = jnp.where(head == h, sink_ref[h], row)
    return row


def _mixer_b(q_of, k2, v2, bias, sink_row, key_thresh, store):
    group = B_Q_HEADS // B_KV_HEADS
    pieces = []
    for j in range(B_Q_HEADS // 2):
        q2 = q_of(j)
        tq = q2.shape[0]
        lane_hi = lax.broadcasted_iota(jnp.int32, q2.shape, 1) >= HEAD_DIM
        zero = jnp.zeros_like(q2)
        q2_swapped = pltpu.roll(q2.astype(F32), HEAD_DIM, 1).astype(BF16)
        for hh in range(2):
            kv = (2 * j + hh) // group
            src = q2 if hh == kv else q2_swapped
            pieces.append(jnp.where(lane_hi, src, zero) if kv else jnp.where(lane_hi, zero, src))
    rows = group * tq

    def group_scores(g):
        qq = jnp.concatenate(pieces[g * group:(g + 1) * group], axis=0)
        return _scores(qq, k2, bias[:, g * rows:(g + 1) * rows], key_thresh)

    s, outs = group_scores(0), []
    for g in range(B_KV_HEADS):
        s_next = group_scores(g + 1) if g + 1 < B_KV_HEADS else None
        outs.append(_softmax_pv(s, v2, sink_row[:, g * rows:(g + 1) * rows]))
        s = s_next
    o = jnp.concatenate(outs, axis=0)
    lane_hi = lax.broadcasted_iota(jnp.int32, (tq, HEAD_PAIR), 1) >= HEAD_DIM
    for j in range(B_Q_HEADS // 2):
        halves = []
        for hh in range(2):
            h = 2 * j + hh
            o_h = o[h * tq:(h + 1) * tq]
            halves.append(o_h if hh == h // group else pltpu.roll(o_h, HEAD_DIM, 1))
        store(j, jnp.where(lane_hi, halves[1], halves[0]))


def _attn_prompt_kernel(sink_ref, qa_ref, kap_ref, kac_ref, vap_ref, vac_ref,
                        qb_ref, kbp_ref, kbc_ref, vbp_ref, vbc_ref, bias_ref,
                        ya_ref, yb_ref, ka_s, va_s, kb_s, vb_s, bias_b_s, sink_s):
    i = pl.program_id(1)
    blk = ATT_BLOCK
    ka_s[0:blk, :] = kap_ref[...]
    ka_s[blk:2 * blk, :] = kac_ref[...]
    va_s[0:blk, :] = vap_ref[...]
    va_s[blk:2 * blk, :] = vac_ref[...]
    kb_s[0:blk, :] = kbp_ref[...]
    kb_s[blk:2 * blk, :] = kbc_ref[...]
    vb_s[0:blk, :] = vbp_ref[...]
    vb_s[blk:2 * blk, :] = vbc_ref[...]

    tk_a = (A_PREV_CHUNKS + 1) * CHUNK
    tk_b = (B_PREV_CHUNKS + 1) * CHUNK
    b_off = blk - B_PREV_CHUNKS * CHUNK
    bias_b_s[...] = _alibi_bias(CHUNK, tk_b)
    sink_s[...] = _sink_row(sink_ref, CHUNK)

    def chunk_body(c, carry):
        row0 = pl.multiple_of(c * CHUNK, CHUNK)
        thresh_a = jnp.where(i > 0, 0, blk - c * CHUNK)
        thresh_b = jnp.where(i > 0, 0, B_PREV_CHUNKS * CHUNK - c * CHUNK)

        def store_a(j, val):
            ya_ref[pl.ds(row0, CHUNK), j * HEAD_PAIR:(j + 1) * HEAD_PAIR] = val.astype(ya_ref.dtype)

        def store_b(j, val):
            yb_ref[pl.ds(row0, CHUNK), j * HEAD_PAIR:(j + 1) * HEAD_PAIR] = val.astype(yb_ref.dtype)

        _mixer_a(
            lambda j: qa_ref[pl.ds(row0, CHUNK), j * HEAD_PAIR:(j + 1) * HEAD_PAIR],
            lambda j: ka_s[pl.ds(row0, tk_a), j * HEAD_PAIR:(j + 1) * HEAD_PAIR],
            lambda j: va_s[pl.ds(row0, tk_a), j * HEAD_PAIR:(j + 1) * HEAD_PAIR],
            lambda j: bias_ref[j], thresh_a, store_a)

        rowb = pl.multiple_of(b_off + c * CHUNK, CHUNK)
        _mixer_b(
            lambda j: qb_ref[pl.ds(row0, CHUNK), j * HEAD_PAIR:(j + 1) * HEAD_PAIR],
            kb_s[pl.ds(rowb, tk_b), :], vb_s[pl.ds(rowb, tk_b), :],
            bias_b_s[...], sink_s[...], thresh_b, store_b)
        return carry

    lax.fori_loop(0, blk // CHUNK, chunk_body, 0)


def _attn_prompt(p, bias_a, sinks, batch, seq):
    blk = ATT_BLOCK
    nb = seq // blk
    n = batch * seq

    def cur(col_block):
        return lambda b, i: (b * nb + i, col_block)

    def prev(col_block):
        return lambda b, i: (b * nb + jnp.maximum(i - 1, 0), col_block)

    wa, wb = A_WIDTH, B_KV_WIDTH
    in_specs = [
        pl.BlockSpec(memory_space=pltpu.SMEM),
        pl.BlockSpec((blk, wa), cur(COL_QA // wa)),
        pl.BlockSpec((blk, wa), prev(COL_KA // wa)),
        pl.BlockSpec((blk, wa), cur(COL_KA // wa)),
        pl.BlockSpec((blk, wa), prev(COL_VA // wa)),
        pl.BlockSpec((blk, wa), cur(COL_VA // wa)),
        pl.BlockSpec((blk, wa), cur(COL_QB // wa)),
        pl.BlockSpec((blk, wb), prev(COL_KB // wb)),
        pl.BlockSpec((blk, wb), cur(COL_KB // wb)),
        pl.BlockSpec((blk, wb), prev(COL_VB // wb)),
        pl.BlockSpec((blk, wb), cur(COL_VB // wb)),
        _const_spec(bias_a.shape),
    ]
    out_spec = pl.BlockSpec((blk, wa), lambda b, i: (b * nb + i, 0))
    rows_b = B_Q_HEADS * CHUNK
    return pl.pallas_call(
        _attn_prompt_kernel,
        grid=(batch, nb),
        in_specs=in_specs,
        out_specs=[out_spec, out_spec],
        out_shape=[jax.ShapeDtypeStruct((n, A_WIDTH), BF16), jax.ShapeDtypeStruct((n, B_WIDTH), BF16)],
        scratch_shapes=[
            pltpu.VMEM((2 * blk, wa), BF16), pltpu.VMEM((2 * blk, wa), BF16),
            pltpu.VMEM((2 * blk, wb), BF16), pltpu.VMEM((2 * blk, wb), BF16),
            pltpu.VMEM(((B_PREV_CHUNKS + 1) * CHUNK, rows_b), F32), pltpu.VMEM((1, rows_b), F32),
        ],
        compiler_params=_compiler_params(("parallel", "arbitrary")),
    )(sinks, p, p, p, p, p, p, p, p, p, p, bias_a)


def _attn_sample_kernel(sink_ref, qa_ref, ka_ref, va_ref, qb_ref, kb_ref, vb_ref, bias_ref, ya_ref, yb_ref):
    t, tk_b = qb_ref.shape[0], kb_ref.shape[1]

    def store_a(j, val):
        ya_ref[:, j * HEAD_PAIR:(j + 1) * HEAD_PAIR] = val.astype(ya_ref.dtype)

    def store_b(j, val):
        yb_ref[:, j * HEAD_PAIR:(j + 1) * HEAD_PAIR] = val.astype(yb_ref.dtype)

    _mixer_a(
        lambda j: qa_ref[:, j * HEAD_PAIR:(j + 1) * HEAD_PAIR],
        lambda j: ka_ref[0, :, j * HEAD_PAIR:(j + 1) * HEAD_PAIR],
        lambda j: va_ref[0, :, j * HEAD_PAIR:(j + 1) * HEAD_PAIR],
        lambda j: bias_ref[j], None, store_a)

    _mixer_b(
        lambda j: qb_ref[:, j * HEAD_PAIR:(j + 1) * HEAD_PAIR],
        kb_ref[0], vb_ref[0], _alibi_bias(t, tk_b), _sink_row(sink_ref, t), None, store_b)


def _attn_sample(p, kcat_a, vcat_a, kcat_b, vcat_b, bias_a, sinks, batch, t):
    n = batch * t
    wa, wb = A_WIDTH, B_KV_WIDTH
    tk_a, tk_b = kcat_a.shape[1], kcat_b.shape[1]
    in_specs = [
        pl.BlockSpec(memory_space=pltpu.SMEM),
        pl.BlockSpec((t, wa), lambda b: (b, COL_QA // wa)),
        pl.BlockSpec((1, tk_a, wa), lambda b: (b, 0, 0)),
        pl.BlockSpec((1, tk_a, wa), lambda b: (b, 0, 0)),
        pl.BlockSpec((t, wa), lambda b: (b, COL_QB // wa)),
        pl.BlockSpec((1, tk_b, wb), lambda b: (b, 0, 0)),
        pl.BlockSpec((1, tk_b, wb), lambda b: (b, 0, 0)),
        _const_spec(bias_a.shape),
    ]
    out_spec = pl.BlockSpec((t, wa), lambda b: (b, 0))
    return pl.pallas_call(
        _attn_sample_kernel,
        grid=(batch,),
        in_specs=in_specs,
        out_specs=[out_spec, out_spec],
        out_shape=[jax.ShapeDtypeStruct((n, A_WIDTH), BF16), jax.ShapeDtypeStruct((n, B_WIDTH), BF16)],
        compiler_params=_compiler_params(("parallel",)),
    )(sinks, p, kcat_a, vcat_a, p, kcat_b, vcat_b, bias_a)


def _layer_norm(x, g, b):
    mu = jnp.mean(x, axis=-1, keepdims=True)
    xc = x - mu
    var = jnp.mean(xc * xc, axis=-1, keepdims=True)
    return xc * lax.rsqrt(var + LN_EPS) * g + b


def _merge_route_kernel(ya_ref, yb_ref, ga_ref, gb_ref, x_ref, wpa_ref, wpb_ref, wout_ref, g_ref, b_ref,
                        wq_ref, sk1_ref, sk2_ref, h_ref, idx_ref, gate_ref):
    za = jnp.dot(ya_ref[...], wpa_ref[...], preferred_element_type=F32)
    zb = jnp.dot(yb_ref[...], wpb_ref[...], preferred_element_type=F32)
    m = jax.nn.sigmoid(ga_ref[...].astype(F32)) * za + jax.nn.sigmoid(gb_ref[...].astype(F32)) * zb
    r = jnp.dot(m.astype(BF16), wout_ref[...], preferred_element_type=F32)
    h_ref[...] = _layer_norm(DEEPNORM_ALPHA * x_ref[...] + r, g_ref[...], b_ref[...])
    for t0 in range(0, h_ref.shape[0], ROUTE_TM):
        q = _route_query(h_ref[t0:t0 + ROUTE_TM, :].astype(BF16), wq_ref)
        routed = [_route_head(q, hd, sk1_ref, sk2_ref) for hd in range(PEER_HEADS)]
        experts = jnp.concatenate([e for e, _ in routed], axis=0)
        gates = jnp.concatenate([g for _, g in routed], axis=0)
        idx_ref[t0:t0 + ROUTE_TM, :] = experts.astype(F32).T.astype(jnp.int32)
        gate_ref[t0:t0 + ROUTE_TM, :] = gates.T


def _merge_route(ya, yb, p, x, wpa, wpb, wout, ln_g, ln_b, wq, sk1, sk2, tm):
    n = x.shape[0]
    assert n % tm == 0 and tm % ROUTE_TM == 0
    row = lambda i: (i, 0)
    return pl.pallas_call(
        _merge_route_kernel,
        grid=(n // tm,),
        in_specs=[
            pl.BlockSpec((tm, A_WIDTH), row),
            pl.BlockSpec((tm, B_WIDTH), row),
            pl.BlockSpec((tm, D_MODEL), lambda i: (i, COL_GA // D_MODEL)),
            pl.BlockSpec((tm, D_MODEL), lambda i: (i, COL_GB // D_MODEL)),
            pl.BlockSpec((tm, D_MODEL), row),
            _const_spec(wpa.shape), _const_spec(wpb.shape), _const_spec(wout.shape),
            _const_spec(ln_g.shape), _const_spec(ln_b.shape),
            _const_spec(wq.shape), _const_spec(sk1.shape), _const_spec(sk2.shape),
        ],
        out_specs=[pl.BlockSpec((tm, D_MODEL), row), pl.BlockSpec((tm, PEER_PICKS), row),
                   pl.BlockSpec((tm, PEER_PICKS), row)],
        out_shape=[jax.ShapeDtypeStruct((n, D_MODEL), F32), jax.ShapeDtypeStruct((n, PEER_PICKS), jnp.int32),
                   jax.ShapeDtypeStruct((n, PEER_PICKS), F32)],
        compiler_params=_compiler_params(("parallel",)),
    )(ya, yb, p, p, x, wpa, wpb, wout, ln_g, ln_b, wq, sk1, sk2)


ROUTE_TM = 128
ID_SENTINEL = 1 << 30


def _topk_rows(s, ids, k):
    vals, picked = [], []
    for _ in range(k):
        m = jnp.max(s, axis=0, keepdims=True)
        ix = jnp.min(jnp.where(s == m, ids, ID_SENTINEL), axis=0, keepdims=True)
        s = jnp.where(ids == ix, -jnp.inf, s)
        vals.append(m)
        picked.append(ix)
    return jnp.concatenate(vals, axis=0), jnp.concatenate(picked, axis=0)


def _sorting_network(n):
    def merge(lo, hi, r):
        step = r * 2
        if step < hi - lo:
            yield from merge(lo, hi, step)
            yield from merge(lo + r, hi, step)
            yield from [(i, i + r) for i in range(lo + r, hi - r, step)]
        else:
            yield (lo, lo + r)

    def sort(lo, hi):
        if hi - lo >= 1:
            mid = lo + (hi - lo) // 2
            yield from sort(lo, mid)
            yield from sort(mid + 1, hi)
            yield from merge(lo, hi, 1)

    return list(sort(0, n - 1))


def _topk_keys(s, k):
    wires = s.shape[0] // SUBLANES
    assert wires == k
    sub = lax.broadcasted_iota(jnp.int32, (SUBLANES, s.shape[1]), 0)
    v = [s[g * SUBLANES:(g + 1) * SUBLANES] for g in range(wires)]
    ids = [sub + g * SUBLANES for g in range(wires)]
    for a, b in _sorting_network(wires):
        first = (v[a] > v[b]) | ((v[a] == v[b]) & (ids[a] < ids[b]))
        v[a], v[b] = jnp.where(first, v[a], v[b]), jnp.where(first, v[b], v[a])
        ids[a], ids[b] = jnp.where(first, ids[a], ids[b]), jnp.where(first, ids[b], ids[a])
    vals, picked = [], []
    for t in range(k):
        m = jnp.max(v[0], axis=0, keepdims=True)
        ix = jnp.min(jnp.where(v[0] == m, ids[0], ID_SENTINEL), axis=0, keepdims=True)
        vals.append(m)
        picked.append(ix)
        hit = ids[0] == ix
        for w in range(k - 1 - t):
            v[w] = jnp.where(hit, v[w + 1], v[w])
            ids[w] = jnp.where(hit, ids[w + 1], ids[w])
    return jnp.concatenate(vals, axis=0), jnp.concatenate(picked, axis=0)


def _candidate_blocks(v1, v2):
    t = v1.shape[1]
    sub = lax.broadcasted_iota(jnp.int32, (SUBLANES, t), 0)
    neg = jnp.full((SUBLANES, t), -jnp.inf, F32)
    k = PEER_TOPK
    blocks = []
    blocks.append((v1[0:1] + v2[0:8], sub))
    blocks.append((v1[0:1] + v2[8:16], sub + 8))
    blocks.append((v1[1:2] + v2[0:8], sub + k))
    blocks.append((jnp.where(sub < 5, v1[2:3] + v2[0:8], neg), sub + 2 * k))
    blocks.append((jnp.where(sub < 4, v1[3:4] + v2[0:8], neg), sub + 3 * k))
    blocks.append((v1[8:16] + v2[0:1], (sub + 8) * k))
    a47 = jnp.concatenate([v1[4:8], v1[4:8]], axis=0)
    blocks.append((a47 + jnp.where(sub < 4, v2[0:1], v2[1:2]),
                   (4 + lax.bitwise_and(sub, 3)) * k + lax.shift_right_logical(sub, 2)))
    blocks.append((jnp.where(sub == 0, v1[4:5] + v2[2:3], neg), sub + 4 * k + 2))
    vals = jnp.concatenate([b[0] for b in blocks], axis=0)
    ids = jnp.concatenate([b[1] for b in blocks], axis=0)
    return vals, ids


def _route_query(h, wq_ref):
    return jnp.dot(h, wq_ref[...], preferred_element_type=F32).astype(BF16)


def _route_head(q, hd, sk1_ref, sk2_ref):
    half = PEER_QDIM // 2
    q1 = q[:, hd * PEER_QDIM:hd * PEER_QDIM + half]
    q2 = q[:, hd * PEER_QDIM + half:(hd + 1) * PEER_QDIM]
    s1 = lax.dot_general(sk1_ref[...], q1, NT_DIMS, preferred_element_type=F32)
    s2 = lax.dot_general(sk2_ref[...], q2, NT_DIMS, preferred_element_type=F32)
    v1, i1 = _topk_keys(s1, PEER_TOPK)
    v2, i2 = _topk_keys(s2, PEER_TOPK)
    sc, ci = _topk_rows(*_candidate_blocks(v1, v2), PEER_TOPK)
    ca = lax.shift_right_logical(ci, int(math.log2(PEER_TOPK)))
    cb = lax.bitwise_and(ci, PEER_TOPK - 1)
    e1 = jnp.zeros_like(ci)
    e2 = jnp.zeros_like(ci)
    for a in range(PEER_TOPK):
        e1 = jnp.where(ca == a, i1[a:a + 1], e1)
        e2 = jnp.where(cb == a, i2[a:a + 1], e2)
    e = jnp.exp(sc - sc[0:1])
    return e1 * PEER_KEYS + e2, e / jnp.sum(e, axis=0, keepdims=True)


PEER_TB = 64
PEER_ROUND = 8
PEER_SLOTS = 2 * PEER_ROUND
PEER_LOOKAHEAD = PEER_ROUND
ROW_TILES = 2 * D_MODEL // LANES
HALF_TILES = ROW_TILES // 2

def _gelu_exact(x):
    return 0.5 * x * (1.0 + lax.erf(x * (2.0 ** -0.5)))


def _peer_kernel(idx_ref, gate_ref, h_ref, uv_ref, g_ref, b_ref, y_ref, *scratch):
    bufs, (sem, ffn, pre, wgt) = scratch[:PEER_SLOTS], scratch[PEER_SLOTS:]
    i = pl.program_id(0)
    last = pl.num_programs(0) - 1
    tb = h_ref.shape[0]

    def issue(t, slot, part=None):
        per_part = PEER_PICKS // ROW_TILES
        picks = range(PEER_PICKS) if part is None else range(part * per_part, (part + 1) * per_part)
        for k in picks:
            dst = bufs[slot].at[:, k // SUBLANES, pl.ds(k % SUBLANES, 1), :]
            pltpu.make_async_copy(uv_ref.at[idx_ref[0, t, k]], dst, sem.at[slot]).start(priority=k % 2)

    def wait_slot(slot):
        pltpu.make_async_copy(bufs[slot], bufs[slot], sem.at[slot]).wait()

    def picks_tile(slot, c):
        return bufs[slot][c].reshape(PEER_PICKS, LANES)

    @pl.when(i == 0)
    def _():
        for t in range(PEER_LOOKAHEAD):
            issue(t, t)

    eye = (lax.broadcasted_iota(jnp.int32, (PEER_PICKS, PEER_PICKS), 0)
           == lax.broadcasted_iota(jnp.int32, (PEER_PICKS, PEER_PICKS), 1))

    def pre_activation(t, slot, between):
        h_row = h_ref[pl.ds(t, 1), :]
        acc = None
        for c in range(HALF_TILES):
            between(c)
            term = picks_tile(slot, c) * h_row[:, c * LANES:(c + 1) * LANES]
            acc = term if acc is None else acc + term
        a_col = jnp.sum(acc, axis=1, keepdims=True)
        return jnp.sum(jnp.where(eye, a_col, 0.0), axis=0, keepdims=True)

    def weighted_sum(t, slot, w_row, between):
        w_col = jnp.sum(jnp.where(eye, w_row, 0.0), axis=1, keepdims=True)
        out = []
        for c in range(HALF_TILES):
            between(HALF_TILES + c)
            out.append(jnp.sum(w_col * picks_tile(slot, HALF_TILES + c), axis=0, keepdims=True))
        ffn[pl.ds(t, 1), :] = jnp.concatenate(out, axis=1)

    def round_pair(rr, carry):
        for half in range(2):
            t0 = pl.multiple_of((2 * rr + half) * PEER_ROUND, PEER_ROUND)
            here, there = half * PEER_ROUND, (1 - half) * PEER_ROUND
            def issue_ahead(j):
                return lambda part: issue(t0 + PEER_ROUND + j, there + j, part)

            for j in range(PEER_ROUND):
                wait_slot(here + j)
                pre[j:j + 1, :] = pre_activation(t0 + j, here + j, issue_ahead(j))
            wgt[...] = gate_ref[pl.ds(t0, PEER_ROUND), :] * _gelu_exact(pre[...])
            for j in range(PEER_ROUND):
                weighted_sum(t0 + j, here + j, wgt[j:j + 1, :], issue_ahead(j))
        return carry

    lax.fori_loop(0, tb // PEER_SLOTS, round_pair, 0)

    @pl.when(i == last)
    def _():
        for t in range(PEER_LOOKAHEAD):
            wait_slot(t)

    y_ref[...] = _layer_norm(DEEPNORM_ALPHA * h_ref[...] + ffn[...], g_ref[...], b_ref[...])


def _peer(h, idx, gate, uv, ln_g, ln_b):
    n = h.shape[0]
    tb = min(PEER_TB, n)
    assert n % tb == 0 and tb % PEER_SLOTS == 0
    nb = n // tb
    row = lambda i: (i, 0)
    idx_pad = jnp.concatenate([idx, jnp.zeros((tb, PEER_PICKS), idx.dtype)], axis=0)
    idx_next = idx_pad[tb:].reshape(nb, tb, PEER_PICKS)[:, :PEER_LOOKAHEAD]
    idx_ext = jnp.concatenate([idx.reshape(nb, tb, PEER_PICKS), idx_next], axis=1)
    return pl.pallas_call(
        _peer_kernel,
        grid=(nb,),
        in_specs=[
            pl.BlockSpec((1, tb + PEER_LOOKAHEAD, PEER_PICKS), lambda i: (i, 0, 0), memory_space=pltpu.SMEM),
            pl.BlockSpec((tb, PEER_PICKS), row),
            pl.BlockSpec((tb, D_MODEL), row),
            pl.BlockSpec(memory_space=pl.ANY),
            _const_spec(ln_g.shape), _const_spec(ln_b.shape),
        ],
        out_specs=pl.BlockSpec((tb, D_MODEL), row),
        out_shape=jax.ShapeDtypeStruct((n, D_MODEL), F32),
        scratch_shapes=(
            [pltpu.VMEM((ROW_TILES, PEER_PICKS // SUBLANES, SUBLANES, LANES), F32) for _ in range(PEER_SLOTS)]
            + [pltpu.SemaphoreType.DMA((PEER_SLOTS,)), pltpu.VMEM((tb, D_MODEL), F32),
               pltpu.VMEM((PEER_ROUND, PEER_PICKS), F32), pltpu.VMEM((PEER_ROUND, PEER_PICKS), F32)]
        ),
        compiler_params=_compiler_params(("arbitrary",)),
    )(idx_ext, gate, h, uv, ln_g, ln_b)


def _rel_bias_pairs(rel_table, t, tk):
    dist = (tk - 1) - jnp.arange(tk + t - 1)
    line = jnp.take(rel_table, jnp.clip(dist, -A_REL_CLIP, A_REL_CLIP) + A_REL_CLIP, axis=1).astype(F32)
    bias = jnp.stack([line[:, t - 1 - q:t - 1 - q + tk] for q in range(t)], axis=1)
    return jnp.swapaxes(bias.reshape(A_HEADS // 2, 2 * t, tk), 1, 2)


PACK_ROWS = 256


def _pack_kernel(u_ref, v_ref, out_ref, stage, sem):
    i = pl.program_id(0)
    buf = lax.rem(i, 2)
    rows = u_ref.shape[0]

    def drain(b):
        pltpu.make_async_copy(stage.at[b], stage.at[b], sem.at[b]).wait()

    @pl.when(i >= 2)
    def _():
        drain(buf)

    for part, x_ref in enumerate((u_ref, v_ref)):
        for c in range(HALF_TILES):
            tile = x_ref[:, c * LANES:(c + 1) * LANES].reshape(rows // SUBLANES, SUBLANES, LANES)
            stage[buf, part * HALF_TILES + c] = tile

    def row_copy(r, carry):
        for s in range(SUBLANES):
            src = stage.at[buf, :, r, pl.ds(s, 1), :]
            pltpu.make_async_copy(src, out_ref.at[i * rows + r * SUBLANES + s], sem.at[buf]).start()
        return carry

    lax.fori_loop(0, rows // SUBLANES, row_copy, 0)

    @pl.when(i == pl.num_programs(0) - 1)
    def _():
        drain(buf)

        @pl.when(i >= 1)
        def _():
            drain(1 - buf)


def _pack_expert_tables(u, v):
    n = u.shape[0]
    rows = PACK_ROWS
    assert n % rows == 0
    return pl.pallas_call(
        _pack_kernel,
        grid=(n // rows,),
        in_specs=[pl.BlockSpec((rows, D_MODEL), lambda i: (i, 0)), pl.BlockSpec((rows, D_MODEL), lambda i: (i, 0))],
        out_specs=pl.BlockSpec(memory_space=pl.ANY),
        out_shape=jax.ShapeDtypeStruct((n, ROW_TILES, 1, LANES), F32),
        scratch_shapes=[pltpu.VMEM((2, ROW_TILES, rows // SUBLANES, SUBLANES, LANES), F32),
                        pltpu.SemaphoreType.DMA((2,))],
        compiler_params=_compiler_params(("arbitrary",)),
    )(u, v)


def _permute_in_proj(w_in):
    offs = np.cumsum([0, A_WIDTH, A_WIDTH, A_WIDTH, B_WIDTH, B_KV_WIDTH, B_KV_WIDTH, D_MODEL, D_MODEL])
    qa, ka, va, qb, kb, vb, ga, gb = [w_in[:, offs[s]:offs[s + 1]] for s in range(8)]
    return jnp.concatenate([ga, gb, qa, qb, ka, va, kb, vb], axis=1).astype(BF16)


def _split_kv(kv, batch, t):
    ka = kv[:, 0:A_WIDTH].reshape(batch, t, A_HEADS, HEAD_DIM)
    va = kv[:, A_WIDTH:2 * A_WIDTH].reshape(batch, t, A_HEADS, HEAD_DIM)
    kb = kv[:, 2 * A_WIDTH:2 * A_WIDTH + B_KV_WIDTH].reshape(batch, t, B_KV_HEADS, HEAD_DIM)
    vb = kv[:, 2 * A_WIDTH + B_KV_WIDTH:].reshape(batch, t, B_KV_HEADS, HEAD_DIM)
    return ka, va, kb, vb


def _cache_heads_kernel(k_ref, v_ref, ko_ref, vo_ref):
    for src, dst in ((k_ref, ko_ref), (v_ref, vo_ref)):
        for hd in range(A_HEADS):
            dst[0, :, hd, :] = src[:, hd * HEAD_DIM:(hd + 1) * HEAD_DIM]


def _cache_rows_a(kv, batch, seq, rows):
    assert seq % rows == 0
    per_seq = seq // rows
    out = jax.ShapeDtypeStruct((batch, rows, A_HEADS, HEAD_DIM), F32)
    out_spec = pl.BlockSpec((1, rows, A_HEADS, HEAD_DIM), lambda b: (b, 0, 0, 0))
    return pl.pallas_call(
        _cache_heads_kernel,
        grid=(batch,),
        in_specs=[pl.BlockSpec((rows, A_WIDTH), lambda b: (b * per_seq + per_seq - 1, 0)),
                  pl.BlockSpec((rows, A_WIDTH), lambda b: (b * per_seq + per_seq - 1, 1))],
        out_specs=[out_spec, out_spec],
        out_shape=[out, out],
        compiler_params=_compiler_params(("parallel",)),
    )(kv, kv)


def kernel(x_prompt, x_sample, cache_a_k, cache_a_v, cache_b_k, cache_b_v, w_in, rel_bias_table, b_sinks,
           w_proj_a, w_proj_b, w_out, ln1_g, ln1_b, peer_w_query, peer_sub_keys_1, peer_sub_keys_2,
           peer_u, peer_v, ln2_g, ln2_b):
    batch, seq, _ = x_prompt.shape
    dbatch, dseq, _ = x_sample.shape
    assert seq % ATT_BLOCK == 0 and seq >= ATT_BLOCK

    w_in_p = _permute_in_proj(w_in)
    wpa, wpb, wout = w_proj_a.astype(BF16), w_proj_b.astype(BF16), w_out.astype(BF16)
    wq = peer_w_query.astype(BF16)
    sk1, sk2 = peer_sub_keys_1.astype(BF16), peer_sub_keys_2.astype(BF16)
    uv = _pack_expert_tables(peer_u, peer_v)
    ln1 = (ln1_g.reshape(1, D_MODEL), ln1_b.reshape(1, D_MODEL))
    ln2 = (ln2_g.reshape(1, D_MODEL), ln2_b.reshape(1, D_MODEL))
    sinks = b_sinks.astype(F32)

    def tail(xf, ya, yb, p, tm):
        h, idx, gate = _merge_route(ya, yb, p, xf, wpa, wpb, wout, *ln1, wq, sk1, sk2, tm=tm)
        return _peer(h, idx, gate, uv, *ln2)

    ns = dbatch * dseq
    xs = x_sample.reshape(ns, D_MODEL)
    p_s, kv_s = _in_proj(xs, w_in_p, tm=ns)
    ka_s, va_s, kb_s, vb_s = _split_kv(kv_s, dbatch, dseq)

    def with_cache(cache, new_cols, width):
        new = p_s[:, new_cols:new_cols + width].reshape(dbatch, dseq, width)
        return jnp.concatenate([cache.reshape(dbatch, -1, width).astype(BF16), new], axis=1)

    kcat_a = with_cache(cache_a_k, COL_KA, A_WIDTH)
    vcat_a = with_cache(cache_a_v, COL_VA, A_WIDTH)
    kcat_b = with_cache(cache_b_k, COL_KB, B_KV_WIDTH)
    vcat_b = with_cache(cache_b_v, COL_VB, B_KV_WIDTH)
    bias_s = _rel_bias_pairs(rel_bias_table, dseq, kcat_a.shape[1])
    ya_s, yb_s = _attn_sample(p_s, kcat_a, vcat_a, kcat_b, vcat_b, bias_s, sinks, dbatch, dseq)
    y_sample = tail(xs, ya_s, yb_s, p_s, ns).reshape(dbatch, dseq, D_MODEL)

    xp = x_prompt.reshape(batch * seq, D_MODEL)
    p, kv = _in_proj(xp, w_in_p, tm=1024)
    bias_p = _rel_bias_pairs(rel_bias_table, CHUNK, (A_PREV_CHUNKS + 1) * CHUNK)
    ya, yb = _attn_prompt(p, bias_p, sinks, batch, seq)
    y_prompt = tail(xp, ya, yb, p, 256).reshape(batch, seq, D_MODEL)
    la = min(A_PREV_CHUNKS * CHUNK, seq)
    lb = min(B_PREV_CHUNKS * CHUNK, seq)
    ka_last, va_last = _cache_rows_a(kv, batch, seq, la)
    _, _, kb, vb = _split_kv(kv, batch, seq)

    return (y_prompt, y_sample, ka_last, va_last, kb[:, -lb:], vb[:, -lb:], ka_s, va_s, kb_s, vb_s)
```

```python
import math

import jax
import jax.numpy as jnp
import numpy as np
from jax import lax
from jax.experimental import pallas as pl
from jax.experimental.pallas import tpu as pltpu

F32 = jnp.float32
BF16 = jnp.bfloat16

D_MODEL = 2048
CHUNK = 64
HEAD_DIM = 64
A_HEADS = 16
A_PREV_CHUNKS = 8
A_REL_CLIP = 128
B_Q_HEADS = 16
B_KV_HEADS = 2
B_PREV_CHUNKS = 2
A_WIDTH = A_HEADS * HEAD_DIM
B_WIDTH = B_Q_HEADS * HEAD_DIM
B_KV_WIDTH = B_KV_HEADS * HEAD_DIM
PEER_HEADS = 8
PEER_KEYS = 128
PEER_QDIM = 256
PEER_TOPK = 16
PEER_PICKS = PEER_HEADS * PEER_TOPK
LN_EPS = 1e-5
DEEPNORM_ALPHA = 2.0 ** 0.25

LANES = 128
SUBLANES = 8
HEAD_PAIR = 2 * HEAD_DIM
VMEM_LIMIT = 56 * 1024 * 1024

COL_GA = 0
COL_GB = COL_GA + D_MODEL
COL_QA = COL_GB + D_MODEL
COL_QB = COL_QA + A_WIDTH
COL_KA = COL_QB + B_WIDTH
COL_VA = COL_KA + A_WIDTH
COL_KB = COL_VA + A_WIDTH
COL_VB = COL_KB + B_KV_WIDTH
IN_COLS = COL_VB + B_KV_WIDTH
KV_COLS = IN_COLS - COL_KA
PROJ_TN = 768
ATT_BLOCK = A_PREV_CHUNKS * CHUNK

NT_DIMS = (((1,), (1,)), ((), ()))
TN_DIMS = (((0,), (0,)), ((), ()))


def _compiler_params(semantics):
    return pltpu.CompilerParams(dimension_semantics=semantics, vmem_limit_bytes=VMEM_LIMIT)


def _const_spec(shape):
    zeros = (0,) * len(shape)
    return pl.BlockSpec(shape, lambda *_: zeros, pipeline_mode=pl.Buffered(1))


def _in_proj_kernel(x_ref, w_ref, p_ref, kv_ref, xb_ref):
    j = pl.program_id(1)

    @pl.when(j == 0)
    def _():
        xb_ref[...] = x_ref[...].astype(BF16)

    acc = jnp.dot(xb_ref[...], w_ref[...], preferred_element_type=F32)
    p_ref[...] = acc.astype(BF16)

    @pl.when(j >= COL_KA // PROJ_TN)
    def _():
        kv_ref[...] = acc


def _in_proj(x, w_bf16, tm):
    n = x.shape[0]
    first_kv = COL_KA // PROJ_TN
    return pl.pallas_call(
        _in_proj_kernel,
        grid=(n // tm, IN_COLS // PROJ_TN),
        in_specs=[
            pl.BlockSpec((tm, D_MODEL), lambda i, j: (i, 0)),
            pl.BlockSpec((D_MODEL, PROJ_TN), lambda i, j: (0, j)),
        ],
        out_specs=[
            pl.BlockSpec((tm, PROJ_TN), lambda i, j: (i, j)),
            pl.BlockSpec((tm, PROJ_TN), lambda i, j: (i, jnp.maximum(j - first_kv, 0))),
        ],
        out_shape=[
            jax.ShapeDtypeStruct((n, IN_COLS), BF16),
            jax.ShapeDtypeStruct((n, KV_COLS), F32),
        ],
        scratch_shapes=[pltpu.VMEM((tm, D_MODEL), BF16)],
        compiler_params=_compiler_params(("parallel", "arbitrary")),
    )(x, w_bf16)


def _softmax_pv(s, v2, sink):
    m = jnp.max(s, axis=0, keepdims=True)
    if sink is not None:
        m = jnp.maximum(m, sink)
    p = jnp.exp(s - m)
    den = jnp.sum(p, axis=0, keepdims=True)
    if sink is not None:
        den = den + jnp.exp(sink - m)
    p = (p * (1.0 / den)).astype(BF16)
    return lax.dot_general(p, v2, TN_DIMS, preferred_element_type=F32)


def _scores(qq, k2, bias, key_thresh):
    s = lax.dot_general(k2, qq, NT_DIMS, preferred_element_type=F32) * (HEAD_DIM ** -0.5) + bias
    if key_thresh is not None:
        key = lax.broadcasted_iota(jnp.int32, s.shape, 0)
        s = jnp.where(key >= key_thresh, s, -jnp.inf)
    return s


def _mixer_a(q_of, k_of, v_of, bias_of, key_thresh, store):
    def pair_scores(j):
        q2 = q_of(j)
        lane_hi = lax.broadcasted_iota(jnp.int32, q2.shape, 1) >= HEAD_DIM
        zero = jnp.zeros_like(q2)
        qq = jnp.concatenate([jnp.where(lane_hi, zero, q2), jnp.where(lane_hi, q2, zero)], axis=0)
        return _scores(qq, k_of(j), bias_of(j), key_thresh)

    pairs = A_HEADS // 2
    s = pair_scores(0)
    for j in range(pairs):
        s_next = pair_scores(j + 1) if j + 1 < pairs else None
        o = _softmax_pv(s, v_of(j), None)
        tq = o.shape[0] // 2
        lane_hi = lax.broadcasted_iota(jnp.int32, (tq, HEAD_PAIR), 1) >= HEAD_DIM
        store(j, jnp.where(lane_hi, o[tq:2 * tq], o[0:tq]))
        s = s_next


def _alibi_bias(tq, tk):
    shape = (tk, B_Q_HEADS * tq)
    key = lax.broadcasted_iota(jnp.int32, shape, 0)
    lane = lax.broadcasted_iota(jnp.int32, shape, 1)
    head = lax.shift_right_logical(lane, int(math.log2(tq)))
    q = lax.bitwise_and(lane, tq - 1)
    slope = jnp.exp((head + 1).astype(F32) * (-8.0 / B_Q_HEADS * math.log(2.0)))
    return -slope * jnp.abs(q + (tk - tq) - key).astype(F32)


def _sink_row(sink_ref, tq):
    lane = lax.broadcasted_iota(jnp.int32, (1, B_Q_HEADS * tq), 1)
    head = lax.shift_right_logical(lane, int(math.log2(tq)))
    row = jnp.zeros(lane.shape, F32)
    for h in range(B_Q_HEADS):
        row = jnp.where(head == h, sink_ref[h], row)
    return row


def _mixer_b(q_of, k2, v2, bias, sink_row, key_thresh, store):
    group = B_Q_HEADS // B_KV_HEADS
    pieces = []
    for j in range(B_Q_HEADS // 2):
        q2 = q_of(j)
        tq = q2.shape[0]
        lane_hi = lax.broadcasted_iota(jnp.int32, q2.shape, 1) >= HEAD_DIM
        zero = jnp.zeros_like(q2)
        q2_swapped = pltpu.roll(q2.astype(F32), HEAD_DIM, 1).astype(BF16)
        for hh in range(2):
            kv = (2 * j + hh) // group
            src = q2 if hh == kv else q2_swapped
            pieces.append(jnp.where(lane_hi, src, zero) if kv else jnp.where(lane_hi, zero, src))
    rows = group * tq

    def group_scores(g):
        qq = jnp.concatenate(pieces[g * group:(g + 1) * group], axis=0)
        return _scores(qq, k2, bias[:, g * rows:(g + 1) * rows], key_thresh)

    s, outs = group_scores(0), []
    for g in range(B_KV_HEADS):
        s_next = group_scores(g + 1) if g + 1 < B_KV_HEADS else None
        outs.append(_softmax_pv(s, v2, sink_row[:, g * rows:(g + 1) * rows]))
        s = s_next
    o = jnp.concatenate(outs, axis=0)
    lane_hi = lax.broadcasted_iota(jnp.int32, (tq, HEAD_PAIR), 1) >= HEAD_DIM
    for j in range(B_Q_HEADS // 2):
        halves = []
        for hh in range(2):
            h = 2 * j + hh
            o_h = o[h * tq:(h + 1) * tq]
            halves.append(o_h if hh == h // group else pltpu.roll(o_h, HEAD_DIM, 1))
        store(j, jnp.where(lane_hi, halves[1], halves[0]))


def _attn_prompt_kernel(sink_ref, qa_ref, kap_ref, kac_ref, vap_ref, vac_ref,
                        qb_ref, kbp_ref, kbc_ref, vbp_ref, vbc_ref, bias_ref,
                        ya_ref, yb_ref, ka_s, va_s, kb_s, vb_s, bias_b_s, sink_s):
    i = pl.program_id(1)
    blk = ATT_BLOCK
    ka_s[0:blk, :] = kap_ref[...]
    ka_s[blk:2 * blk, :] = kac_ref[...]
    va_s[0:blk, :] = vap_ref[...]
    va_s[blk:2 * blk, :] = vac_ref[...]
    kb_s[0:blk, :] = kbp_ref[...]
    kb_s[blk:2 * blk, :] = kbc_ref[...]
    vb_s[0:blk, :] = vbp_ref[...]
    vb_s[blk:2 * blk, :] = vbc_ref[...]

    tk_a = (A_PREV_CHUNKS + 1) * CHUNK
    tk_b = (B_PREV_CHUNKS + 1) * CHUNK
    b_off = blk - B_PREV_CHUNKS * CHUNK
    bias_b_s[...] = _alibi_bias(CHUNK, tk_b)
    sink_s[...] = _sink_row(sink_ref, CHUNK)

    def chunk_body(c, carry):
        row0 = pl.multiple_of(c * CHUNK, CHUNK)
        thresh_a = jnp.where(i > 0, 0, blk - c * CHUNK)
        thresh_b = jnp.where(i > 0, 0, B_PREV_CHUNKS * CHUNK - c * CHUNK)

        def store_a(j, val):
            ya_ref[pl.ds(row0, CHUNK), j * HEAD_PAIR:(j + 1) * HEAD_PAIR] = val.astype(ya_ref.dtype)

        def store_b(j, val):
            yb_ref[pl.ds(row0, CHUNK), j * HEAD_PAIR:(j + 1) * HEAD_PAIR] = val.astype(yb_ref.dtype)

        _mixer_a(
            lambda j: qa_ref[pl.ds(row0, CHUNK), j * HEAD_PAIR:(j + 1) * HEAD_PAIR],
            lambda j: ka_s[pl.ds(row0, tk_a), j * HEAD_PAIR:(j + 1) * HEAD_PAIR],
            lambda j: va_s[pl.ds(row0, tk_a), j * HEAD_PAIR:(j + 1) * HEAD_PAIR],
            lambda j: bias_ref[j], thresh_a, store_a)

        rowb = pl.multiple_of(b_off + c * CHUNK, CHUNK)
        _mixer_b(
            lambda j: qb_ref[pl.ds(row0, CHUNK), j * HEAD_PAIR:(j + 1) * HEAD_PAIR],
            kb_s[pl.ds(rowb, tk_b), :], vb_s[pl.ds(rowb, tk_b), :],
            bias_b_s[...], sink_s[...], thresh_b, store_b)
        return carry

    lax.fori_loop(0, blk // CHUNK, chunk_body, 0)


def _attn_prompt(p, bias_a, sinks, batch, seq):
    blk = ATT_BLOCK
    nb = seq // blk
    n = batch * seq

    def cur(col_block):
        return lambda b, i: (b * nb + i, col_block)

    def prev(col_block):
        return lambda b, i: (b * nb + jnp.maximum(i - 1, 0), col_block)

    wa, wb = A_WIDTH, B_KV_WIDTH
    in_specs = [
        pl.BlockSpec(memory_space=pltpu.SMEM),
        pl.BlockSpec((blk, wa), cur(COL_QA // wa)),
        pl.BlockSpec((blk, wa), prev(COL_KA // wa)),
        pl.BlockSpec((blk, wa), cur(COL_KA // wa)),
        pl.BlockSpec((blk, wa), prev(COL_VA // wa)),
        pl.BlockSpec((blk, wa), cur(COL_VA // wa)),
        pl.BlockSpec((blk, wa), cur(COL_QB // wa)),
        pl.BlockSpec((blk, wb), prev(COL_KB // wb)),
        pl.BlockSpec((blk, wb), cur(COL_KB // wb)),
        pl.BlockSpec((blk, wb), prev(COL_VB // wb)),
        pl.BlockSpec((blk, wb), cur(COL_VB // wb)),
        _const_spec(bias_a.shape),
    ]
    out_spec = pl.BlockSpec((blk, wa), lambda b, i: (b * nb + i, 0))
    rows_b = B_Q_HEADS * CHUNK
    return pl.pallas_call(
        _attn_prompt_kernel,
        grid=(batch, nb),
        in_specs=in_specs,
        out_specs=[out_spec, out_spec],
        out_shape=[jax.ShapeDtypeStruct((n, A_WIDTH), BF16), jax.ShapeDtypeStruct((n, B_WIDTH), BF16)],
        scratch_shapes=[
            pltpu.VMEM((2 * blk, wa), BF16), pltpu.VMEM((2 * blk, wa), BF16),
            pltpu.VMEM((2 * blk, wb), BF16), pltpu.VMEM((2 * blk, wb), BF16),
            pltpu.VMEM(((B_PREV_CHUNKS + 1) * CHUNK, rows_b), F32), pltpu.VMEM((1, rows_b), F32),
        ],
        compiler_params=_compiler_params(("parallel", "arbitrary")),
    )(sinks, p, p, p, p, p, p, p, p, p, p, bias_a)


def _attn_sample_kernel(sink_ref, qa_ref, ka_ref, va_ref, qb_ref, kb_ref, vb_ref, bias_ref, ya_ref, yb_ref):
    t, tk_b = qb_ref.shape[0], kb_ref.shape[1]

    def store_a(j, val):
        ya_ref[:, j * HEAD_PAIR:(j + 1) * HEAD_PAIR] = val.astype(ya_ref.dtype)

    def store_b(j, val):
        yb_ref[:, j * HEAD_PAIR:(j + 1) * HEAD_PAIR] = val.astype(yb_ref.dtype)

    _mixer_a(
        lambda j: qa_ref[:, j * HEAD_PAIR:(j + 1) * HEAD_PAIR],
        lambda j: ka_ref[0, :, j * HEAD_PAIR:(j + 1) * HEAD_PAIR],
        lambda j: va_ref[0, :, j * HEAD_PAIR:(j + 1) * HEAD_PAIR],
        lambda j: bias_ref[j], None, store_a)

    _mixer_b(
        lambda j: qb_ref[:, j * HEAD_PAIR:(j + 1) * HEAD_PAIR],
        kb_ref[0], vb_ref[0], _alibi_bias(t, tk_b), _sink_row(sink_ref, t), None, store_b)


def _attn_sample(p, kcat_a, vcat_a, kcat_b, vcat_b, bias_a, sinks, batch, t):
    n = batch * t
    wa, wb = A_WIDTH, B_KV_WIDTH
    tk_a, tk_b = kcat_a.shape[1], kcat_b.shape[1]
    in_specs = [
        pl.BlockSpec(memory_space=pltpu.SMEM),
        pl.BlockSpec((t, wa), lambda b: (b, COL_QA // wa)),
        pl.BlockSpec((1, tk_a, wa), lambda b: (b, 0, 0)),
        pl.BlockSpec((1, tk_a, wa), lambda b: (b, 0, 0)),
        pl.BlockSpec((t, wa), lambda b: (b, COL_QB // wa)),
        pl.BlockSpec((1, tk_b, wb), lambda b: (b, 0, 0)),
        pl.BlockSpec((1, tk_b, wb), lambda b: (b, 0, 0)),
        _const_spec(bias_a.shape),
    ]
    out_spec = pl.BlockSpec((t, wa), lambda b: (b, 0))
    return pl.pallas_call(
        _attn_sample_kernel,
        grid=(batch,),
        in_specs=in_specs,
        out_specs=[out_spec, out_spec],
        out_shape=[jax.ShapeDtypeStruct((n, A_WIDTH), BF16), jax.ShapeDtypeStruct((n, B_WIDTH), BF16)],
        compiler_params=_compiler_params(("parallel",)),
    )(sinks, p, kcat_a, vcat_a, p, kcat_b, vcat_b, bias_a)


def _layer_norm(x, g, b):
    mu = jnp.mean(x, axis=-1, keepdims=True)
    xc = x - mu
    var = jnp.mean(xc * xc, axis=-1, keepdims=True)
    return xc * lax.rsqrt(var + LN_EPS) * g + b


def _merge_route_kernel(ya_ref, yb_ref, ga_ref, gb_ref, x_ref, wpa_ref, wpb_ref, wout_ref, g_ref, b_ref,
                        wq_ref, sk1_ref, sk2_ref, h_ref, idx_ref, gate_ref):
    za = jnp.dot(ya_ref[...], wpa_ref[...], preferred_element_type=F32)
    zb = jnp.dot(yb_ref[...], wpb_ref[...], preferred_element_type=F32)
    m = jax.nn.sigmoid(ga_ref[...].astype(F32)) * za + jax.nn.sigmoid(gb_ref[...].astype(F32)) * zb
    r = jnp.dot(m.astype(BF16), wout_ref[...], preferred_element_type=F32)
    h_ref[...] = _layer_norm(DEEPNORM_ALPHA * x_ref[...] + r, g_ref[...], b_ref[...])
    for t0 in range(0, h_ref.shape[0], ROUTE_TM):
        q = _route_query(h_ref[t0:t0 + ROUTE_TM, :].astype(BF16), wq_ref)
        routed = [_route_head(q, hd, sk1_ref, sk2_ref) for hd in range(PEER_HEADS)]
        experts = jnp.concatenate([e for e, _ in routed], axis=0)
        gates = jnp.concatenate([g for _, g in routed], axis=0)
        idx_ref[t0:t0 + ROUTE_TM, :] = experts.astype(F32).T.astype(jnp.int32)
        gate_ref[t0:t0 + ROUTE_TM, :] = gates.T


def _merge_route(ya, yb, p, x, wpa, wpb, wout, ln_g, ln_b, wq, sk1, sk2, tm):
    n = x.shape[0]
    assert n % tm == 0 and tm % ROUTE_TM == 0
    row = lambda i: (i, 0)
    return pl.pallas_call(
        _merge_route_kernel,
        grid=(n // tm,),
        in_specs=[
            pl.BlockSpec((tm, A_WIDTH), row),
            pl.BlockSpec((tm, B_WIDTH), row),
            pl.BlockSpec((tm, D_MODEL), lambda i: (i, COL_GA // D_MODEL)),
            pl.BlockSpec((tm, D_MODEL), lambda i: (i, COL_GB // D_MODEL)),
            pl.BlockSpec((tm, D_MODEL), row),
            _const_spec(wpa.shape), _const_spec(wpb.shape), _const_spec(wout.shape),
            _const_spec(ln_g.shape), _const_spec(ln_b.shape),
            _const_spec(wq.shape), _const_spec(sk1.shape), _const_spec(sk2.shape),
        ],
        out_specs=[pl.BlockSpec((tm, D_MODEL), row), pl.BlockSpec((tm, PEER_PICKS), row),
                   pl.BlockSpec((tm, PEER_PICKS), row)],
        out_shape=[jax.ShapeDtypeStruct((n, D_MODEL), F32), jax.ShapeDtypeStruct((n, PEER_PICKS), jnp.int32),
                   jax.ShapeDtypeStruct((n, PEER_PICKS), F32)],
        compiler_params=_compiler_params(("parallel",)),
    )(ya, yb, p, p, x, wpa, wpb, wout, ln_g, ln_b, wq, sk1, sk2)


ROUTE_TM = 128
ID_SENTINEL = 1 << 30


def _topk_rows(s, ids, k):
    vals, picked = [], []
    for _ in range(k):
        m = jnp.max(s, axis=0, keepdims=True)
        ix = jnp.min(jnp.where(s == m, ids, ID_SENTINEL), axis=0, keepdims=True)
        s = jnp.where(ids == ix, -jnp.inf, s)
        vals.append(m)
        picked.append(ix)
    return jnp.concatenate(vals, axis=0), jnp.concatenate(picked, axis=0)


def _sorting_network(n):
    def merge(lo, hi, r):
        step = r * 2
        if step < hi - lo:
            yield from merge(lo, hi, step)
            yield from merge(lo + r, hi, step)
            yield from [(i, i + r) for i in range(lo + r, hi - r, step)]
        else:
            yield (lo, lo + r)

    def sort(lo, hi):
        if hi - lo >= 1:
            mid = lo + (hi - lo) // 2
            yield from sort(lo, mid)
            yield from sort(mid + 1, hi)
            yield from merge(lo, hi, 1)

    return list(sort(0, n - 1))


def _topk_keys(s, k):
    wires = s.shape[0] // SUBLANES
    assert wires == k
    sub = lax.broadcasted_iota(jnp.int32, (SUBLANES, s.shape[1]), 0)
    v = [s[g * SUBLANES:(g + 1) * SUBLANES] for g in range(wires)]
    ids = [sub + g * SUBLANES for g in range(wires)]
    for a, b in _sorting_network(wires):
        first = (v[a] > v[b]) | ((v[a] == v[b]) & (ids[a] < ids[b]))
        v[a], v[b] = jnp.where(first, v[a], v[b]), jnp.where(first, v[b], v[a])
        ids[a], ids[b] = jnp.where(first, ids[a], ids[b]), jnp.where(first, ids[b], ids[a])
    vals, picked = [], []
    for t in range(k):
        m = jnp.max(v[0], axis=0, keepdims=True)
        ix = jnp.min(jnp.where(v[0] == m, ids[0], ID_SENTINEL), axis=0, keepdims=True)
        vals.append(m)
        picked.append(ix)
        hit = ids[0] == ix
        for w in range(k - 1 - t):
            v[w] = jnp.where(hit, v[w + 1], v[w])
            ids[w] = jnp.where(hit, ids[w + 1], ids[w])
    return jnp.concatenate(vals, axis=0), jnp.concatenate(picked, axis=0)


def _candidate_blocks(v1, v2):
    t = v1.shape[1]
    sub = lax.broadcasted_iota(jnp.int32, (SUBLANES, t), 0)
    neg = jnp.full((SUBLANES, t), -jnp.inf, F32)
    k = PEER_TOPK
    blocks = []
    blocks.append((v1[0:1] + v2[0:8], sub))
    blocks.append((v1[0:1] + v2[8:16], sub + 8))
    blocks.append((v1[1:2] + v2[0:8], sub + k))
    blocks.append((jnp.where(sub < 5, v1[2:3] + v2[0:8], neg), sub + 2 * k))
    blocks.append((jnp.where(sub < 4, v1[3:4] + v2[0:8], neg), sub + 3 * k))
    blocks.append((v1[8:16] + v2[0:1], (sub + 8) * k))
    a47 = jnp.concatenate([v1[4:8], v1[4:8]], axis=0)
    blocks.append((a47 + jnp.where(sub < 4, v2[0:1], v2[1:2]),
                   (4 + lax.bitwise_and(sub, 3)) * k + lax.shift_right_logical(sub, 2)))
    blocks.append((jnp.where(sub == 0, v1[4:5] + v2[2:3], neg), sub + 4 * k + 2))
    vals = jnp.concatenate([b[0] for b in blocks], axis=0)
    ids = jnp.concatenate([b[1] for b in blocks], axis=0)
    return vals, ids


def _route_query(h, wq_ref):
    return jnp.dot(h, wq_ref[...], preferred_element_type=F32).astype(BF16)


def _route_head(q, hd, sk1_ref, sk2_ref):
    half = PEER_QDIM // 2
    q1 = q[:, hd * PEER_QDIM:hd * PEER_QDIM + half]
    q2 = q[:, hd * PEER_QDIM + half:(hd + 1) * PEER_QDIM]
    s1 = lax.dot_general(sk1_ref[...], q1, NT_DIMS, preferred_element_type=F32)
    s2 = lax.dot_general(sk2_ref[...], q2, NT_DIMS, preferred_element_type=F32)
    v1, i1 = _topk_keys(s1, PEER_TOPK)
    v2, i2 = _topk_keys(s2, PEER_TOPK)
    sc, ci = _topk_rows(*_candidate_blocks(v1, v2), PEER_TOPK)
    ca = lax.shift_right_logical(ci, int(math.log2(PEER_TOPK)))
    cb = lax.bitwise_and(ci, PEER_TOPK - 1)
    e1 = jnp.zeros_like(ci)
    e2 = jnp.zeros_like(ci)
    for a in range(PEER_TOPK):
        e1 = jnp.where(ca == a, i1[a:a + 1], e1)
        e2 = jnp.where(cb == a, i2[a:a + 1], e2)
    e = jnp.exp(sc - sc[0:1])
    return e1 * PEER_KEYS + e2, e / jnp.sum(e, axis=0, keepdims=True)


PEER_TB = 64
PEER_ROUND = 8
PEER_SLOTS = 2 * PEER_ROUND
PEER_LOOKAHEAD = PEER_ROUND
ROW_TILES = D_MODEL // LANES
HALF_TILES = ROW_TILES // 2

def _gelu_exact(x):
    return 0.5 * x * (1.0 + lax.erf(x * (2.0 ** -0.5)))


def _peer_kernel(idx_ref, gate_ref, h_ref, uv_ref, g_ref, b_ref, y_ref, *scratch):
    bufs, (sem, ffn, pre, wgt) = scratch[:PEER_SLOTS], scratch[PEER_SLOTS:]
    i = pl.program_id(0)
    last = pl.num_programs(0) - 1
    tb = h_ref.shape[0]

    def issue(t, slot, part=None):
        per_part = PEER_PICKS // ROW_TILES
        picks = range(PEER_PICKS) if part is None else range(part * per_part, (part + 1) * per_part)
        for k in picks:
            dst = bufs[slot].at[:, k // SUBLANES, pl.ds(2 * (k % SUBLANES), 2), :]
            pltpu.make_async_copy(uv_ref.at[idx_ref[0, t, k]], dst, sem.at[slot]).start(priority=k % 2)

    def wait_slot(slot):
        pltpu.make_async_copy(bufs[slot], bufs[slot], sem.at[slot]).wait()

    def picks_tile(slot, c):
        tile = bufs[slot][c].reshape(2 * PEER_PICKS, LANES).astype(F32)
        return tile.reshape(2 * PEER_PICKS // SUBLANES, SUBLANES, LANES)

    @pl.when(i == 0)
    def _():
        for t in range(PEER_LOOKAHEAD):
            issue(t, t)

    pair_eye = (lax.shift_right_logical(lax.broadcasted_iota(jnp.int32, (2 * PEER_PICKS, PEER_PICKS), 0), 1)
                == lax.broadcasted_iota(jnp.int32, (2 * PEER_PICKS, PEER_PICKS), 1))
    odd_row = lax.bitwise_and(lax.broadcasted_iota(jnp.int32, (SUBLANES, LANES), 0), 1) == 1
    half_d = D_MODEL // 2

    def pre_activation(t, slot, between):
        h_row = h_ref[pl.ds(t, 1), :]
        acc = None
        for c in range(HALF_TILES):
            between(c)
            h_pat = jnp.where(odd_row, h_row[:, half_d + c * LANES:half_d + (c + 1) * LANES],
                              h_row[:, c * LANES:(c + 1) * LANES])
            term = picks_tile(slot, c) * h_pat
            acc = term if acc is None else acc + term
        a_col = jnp.sum(acc.reshape(2 * PEER_PICKS, LANES), axis=1, keepdims=True)
        return jnp.sum(jnp.where(pair_eye, a_col, 0.0), axis=0, keepdims=True)

    def weighted_sum(t, slot, w_row, between):
        w_col = jnp.sum(jnp.where(pair_eye, w_row, 0.0), axis=1, keepdims=True)
        w_col = w_col.reshape(2 * PEER_PICKS // SUBLANES, SUBLANES, 1)
        lo, hi = [], []
        for c in range(HALF_TILES):
            between(HALF_TILES + c)
            s = jnp.sum(w_col * picks_tile(slot, HALF_TILES + c), axis=0)
            lo.append(jnp.sum(jnp.where(odd_row, 0.0, s), axis=0, keepdims=True))
            hi.append(jnp.sum(jnp.where(odd_row, s, 0.0), axis=0, keepdims=True))
        ffn[pl.ds(t, 1), :] = jnp.concatenate(lo + hi, axis=1)

    def round_pair(rr, carry):
        for half in range(2):
            t0 = pl.multiple_of((2 * rr + half) * PEER_ROUND, PEER_ROUND)
            here, there = half * PEER_ROUND, (1 - half) * PEER_ROUND
            def issue_ahead(j):
                return lambda part: issue(t0 + PEER_ROUND + j, there + j, part)

            for j in range(PEER_ROUND):
                wait_slot(here + j)
                pre[j:j + 1, :] = pre_activation(t0 + j, here + j, issue_ahead(j))
            wgt[...] = gate_ref[pl.ds(t0, PEER_ROUND), :] * _gelu_exact(pre[...])
            for j in range(PEER_ROUND):
                weighted_sum(t0 + j, here + j, wgt[j:j + 1, :], issue_ahead(j))
        return carry

    lax.fori_loop(0, tb // PEER_SLOTS, round_pair, 0)

    @pl.when(i == last)
    def _():
        for t in range(PEER_LOOKAHEAD):
            wait_slot(t)

    y_ref[...] = _layer_norm(DEEPNORM_ALPHA * h_ref[...] + ffn[...], g_ref[...], b_ref[...])


def _peer(h, idx, gate, uv, ln_g, ln_b):
    n = h.shape[0]
    tb = min(PEER_TB, n)
    assert n % tb == 0 and tb % PEER_SLOTS == 0
    nb = n // tb
    row = lambda i: (i, 0)
    idx_pad = jnp.concatenate([idx, jnp.zeros((tb, PEER_PICKS), idx.dtype)], axis=0)
    idx_next = idx_pad[tb:].reshape(nb, tb, PEER_PICKS)[:, :PEER_LOOKAHEAD]
    idx_ext = jnp.concatenate([idx.reshape(nb, tb, PEER_PICKS), idx_next], axis=1)
    return pl.pallas_call(
        _peer_kernel,
        grid=(nb,),
        in_specs=[
            pl.BlockSpec((1, tb + PEER_LOOKAHEAD, PEER_PICKS), lambda i: (i, 0, 0), memory_space=pltpu.SMEM),
            pl.BlockSpec((tb, PEER_PICKS), row),
            pl.BlockSpec((tb, D_MODEL), row),
            pl.BlockSpec(memory_space=pl.ANY),
            _const_spec(ln_g.shape), _const_spec(ln_b.shape),
        ],
        out_specs=pl.BlockSpec((tb, D_MODEL), row),
        out_shape=jax.ShapeDtypeStruct((n, D_MODEL), F32),
        scratch_shapes=(
            [pltpu.VMEM((ROW_TILES, PEER_PICKS // SUBLANES, 2 * SUBLANES, LANES), BF16) for _ in range(PEER_SLOTS)]
            + [pltpu.SemaphoreType.DMA((PEER_SLOTS,)), pltpu.VMEM((tb, D_MODEL), F32),
               pltpu.VMEM((PEER_ROUND, PEER_PICKS), F32), pltpu.VMEM((PEER_ROUND, PEER_PICKS), F32)]
        ),
        compiler_params=_compiler_params(("arbitrary",)),
    )(idx_ext, gate, h, uv, ln_g, ln_b)


def _rel_bias_pairs(rel_table, t, tk):
    dist = (tk - 1) - jnp.arange(tk + t - 1)
    line = jnp.take(rel_table, jnp.clip(dist, -A_REL_CLIP, A_REL_CLIP) + A_REL_CLIP, axis=1).astype(F32)
    bias = jnp.stack([line[:, t - 1 - q:t - 1 - q + tk] for q in range(t)], axis=1)
    return jnp.swapaxes(bias.reshape(A_HEADS // 2, 2 * t, tk), 1, 2)


PACK_ROWS = 256


def _pack_kernel(u_ref, v_ref, out_ref, stage, sem):
    i = pl.program_id(0)
    buf = lax.rem(i, 2)
    rows = u_ref.shape[0]

    def drain(b):
        pltpu.make_async_copy(stage.at[b], stage.at[b], sem.at[b]).wait()

    @pl.when(i >= 2)
    def _():
        drain(buf)

    for part, x_ref in enumerate((u_ref, v_ref)):
        for c in range(HALF_TILES):
            tile = x_ref[:, c * LANES:(c + 1) * LANES].reshape(rows // SUBLANES, SUBLANES, LANES)
            stage[buf, part * HALF_TILES + c] = tile

    def row_copy(r, carry):
        for s in range(SUBLANES):
            src = stage.at[buf, :, r, pl.ds(s, 1), :]
            pltpu.make_async_copy(src, out_ref.at[i * rows + r * SUBLANES + s], sem.at[buf]).start()
        return carry

    lax.fori_loop(0, rows // SUBLANES, row_copy, 0)

    @pl.when(i == pl.num_programs(0) - 1)
    def _():
        drain(buf)

        @pl.when(i >= 1)
        def _():
            drain(1 - buf)


def _pack_expert_tables(u, v):
    n = u.shape[0]
    rows = PACK_ROWS
    assert n % rows == 0
    return pl.pallas_call(
        _pack_kernel,
        grid=(n // rows,),
        in_specs=[pl.BlockSpec((rows, D_MODEL), lambda i: (i, 0)), pl.BlockSpec((rows, D_MODEL), lambda i: (i, 0))],
        out_specs=pl.BlockSpec(memory_space=pl.ANY),
        out_shape=jax.ShapeDtypeStruct((n, ROW_TILES, 1, LANES), F32),
        scratch_shapes=[pltpu.VMEM((2, ROW_TILES, rows // SUBLANES, SUBLANES, LANES), F32),
                        pltpu.SemaphoreType.DMA((2,))],
        compiler_params=_compiler_params(("arbitrary",)),
    )(u, v)


def _expert_table_bf16(u, v):
    def tiles(x):
        n = x.shape[0]
        return x.astype(BF16).reshape(n, 2, HALF_TILES, LANES).transpose(0, 2, 1, 3)
    return jnp.concatenate([tiles(u), tiles(v)], axis=1)


def _permute_in_proj(w_in):
    offs = np.cumsum([0, A_WIDTH, A_WIDTH, A_WIDTH, B_WIDTH, B_KV_WIDTH, B_KV_WIDTH, D_MODEL, D_MODEL])
    qa, ka, va, qb, kb, vb, ga, gb = [w_in[:, offs[s]:offs[s + 1]] for s in range(8)]
    return jnp.concatenate([ga, gb, qa, qb, ka, va, kb, vb], axis=1).astype(BF16)


def _split_kv(kv, batch, t):
    ka = kv[:, 0:A_WIDTH].reshape(batch, t, A_HEADS, HEAD_DIM)
    va = kv[:, A_WIDTH:2 * A_WIDTH].reshape(batch, t, A_HEADS, HEAD_DIM)
    kb = kv[:, 2 * A_WIDTH:2 * A_WIDTH + B_KV_WIDTH].reshape(batch, t, B_KV_HEADS, HEAD_DIM)
    vb = kv[:, 2 * A_WIDTH + B_KV_WIDTH:].reshape(batch, t, B_KV_HEADS, HEAD_DIM)
    return ka, va, kb, vb


def _cache_heads_kernel(k_ref, v_ref, ko_ref, vo_ref):
    for src, dst in ((k_ref, ko_ref), (v_ref, vo_ref)):
        for hd in range(A_HEADS):
            dst[0, :, hd, :] = src[:, hd * HEAD_DIM:(hd + 1) * HEAD_DIM]


def _cache_rows_a(kv, batch, seq, rows):
    assert seq % rows == 0
    per_seq = seq // rows
    out = jax.ShapeDtypeStruct((batch, rows, A_HEADS, HEAD_DIM), F32)
    out_spec = pl.BlockSpec((1, rows, A_HEADS, HEAD_DIM), lambda b: (b, 0, 0, 0))
    return pl.pallas_call(
        _cache_heads_kernel,
        grid=(batch,),
        in_specs=[pl.BlockSpec((rows, A_WIDTH), lambda b: (b * per_seq + per_seq - 1, 0)),
                  pl.BlockSpec((rows, A_WIDTH), lambda b: (b * per_seq + per_seq - 1, 1))],
        out_specs=[out_spec, out_spec],
        out_shape=[out, out],
        compiler_params=_compiler_params(("parallel",)),
    )(kv, kv)


def kernel(x_prompt, x_sample, cache_a_k, cache_a_v, cache_b_k, cache_b_v, w_in, rel_bias_table, b_sinks,
           w_proj_a, w_proj_b, w_out, ln1_g, ln1_b, peer_w_query, peer_sub_keys_1, peer_sub_keys_2,
           peer_u, peer_v, ln2_g, ln2_b):
    batch, seq, _ = x_prompt.shape
    dbatch, dseq, _ = x_sample.shape
    assert seq % ATT_BLOCK == 0 and seq >= ATT_BLOCK

    w_in_p = _permute_in_proj(w_in)
    wpa, wpb, wout = w_proj_a.astype(BF16), w_proj_b.astype(BF16), w_out.astype(BF16)
    wq = peer_w_query.astype(BF16)
    sk1, sk2 = peer_sub_keys_1.astype(BF16), peer_sub_keys_2.astype(BF16)
    uv = _expert_table_bf16(peer_u, peer_v)
    ln1 = (ln1_g.reshape(1, D_MODEL), ln1_b.reshape(1, D_MODEL))
    ln2 = (ln2_g.reshape(1, D_MODEL), ln2_b.reshape(1, D_MODEL))
    sinks = b_sinks.astype(F32)

    def tail(xf, ya, yb, p, tm):
        h, idx, gate = _merge_route(ya, yb, p, xf, wpa, wpb, wout, *ln1, wq, sk1, sk2, tm=tm)
        return _peer(h, idx, gate, uv, *ln2)

    ns = dbatch * dseq
    xs = x_sample.reshape(ns, D_MODEL)
    p_s, kv_s = _in_proj(xs, w_in_p, tm=ns)
    ka_s, va_s, kb_s, vb_s = _split_kv(kv_s, dbatch, dseq)

    def with_cache(cache, new_cols, width):
        new = p_s[:, new_cols:new_cols + width].reshape(dbatch, dseq, width)
        return jnp.concatenate([cache.reshape(dbatch, -1, width).astype(BF16), new], axis=1)

    kcat_a = with_cache(cache_a_k, COL_KA, A_WIDTH)
    vcat_a = with_cache(cache_a_v, COL_VA, A_WIDTH)
    kcat_b = with_cache(cache_b_k, COL_KB, B_KV_WIDTH)
    vcat_b = with_cache(cache_b_v, COL_VB, B_KV_WIDTH)
    bias_s = _rel_bias_pairs(rel_bias_table, dseq, kcat_a.shape[1])
    ya_s, yb_s = _attn_sample(p_s, kcat_a, vcat_a, kcat_b, vcat_b, bias_s, sinks, dbatch, dseq)
    y_sample = tail(xs, ya_s, yb_s, p_s, ns).reshape(dbatch, dseq, D_MODEL)

    xp = x_prompt.reshape(batch * seq, D_MODEL)
    p, kv = _in_proj(xp, w_in_p, tm=1024)
    bias_p = _rel_bias_pairs(rel_bias_table, CHUNK, (A_PREV_CHUNKS + 1) * CHUNK)
    ya, yb = _attn_prompt(p, bias_p, sinks, batch, seq)
    y_prompt = tail(xp, ya, yb, p, 256).reshape(batch, seq, D_MODEL)
    la = min(A_PREV_CHUNKS * CHUNK, seq)
    lb = min(B_PREV_CHUNKS * CHUNK, seq)
    ka_last, va_last = _cache_rows_a(kv, batch, seq, la)
    _, _, kb, vb = _split_kv(kv, batch, seq)

    return (y_prompt, y_sample, ka_last, va_last, kb[:, -lb:], vb[:, -lb:], ka_s, va_s, kb_s, vb_s)
```
